```python
import jax, jax.numpy as jnp
from jax import lax
import numpy as np

D_MODEL = 1024
BATCH = 8
SEQ = 2048
DEPTH = 2
DEC_BATCH = 32
DEC_SEQ = 4
PAST_LEN = 8192
PAGE_SIZE = 128

HEAD_DIM = 64
N_HEADS = D_MODEL // HEAD_DIM
H_SB = N_HEADS // 2
H_MOBA = N_HEADS - H_SB
MOBA_BLOCK = 256
MOBA_TOPK = 3
NSA_HQ = N_HEADS
NSA_HKV = 4
NSA_GROUP = NSA_HQ // NSA_HKV
CMP_BLOCK = 32
CMP_STRIDE = 16
CMP_HIDDEN = 256
SLC_BLOCK = 64
SLC_TOPK = 16
WINDOW = 512
D_FF = ((8 * D_MODEL + 3 * 256 - 1) // (3 * 256)) * 256
ROPE_THETA = 10000.0
NORM_EPS = 1e-6
SB_QBLOCK = 128
GATHER_QBLOCK = 16
N_EVEN = (DEPTH + 1) // 2
N_ODD = DEPTH // 2
EVEN_IN = 3 * N_HEADS * HEAD_DIM
ODD_IN = NSA_HQ * HEAD_DIM + 6 * NSA_HKV * HEAD_DIM + 3 * NSA_HQ
NEG_BIG = -1e30
FORCE_SCORE = 1e9

kernel_name = 'hybrid_stickbreak_moba_nsa_step'


def rms_norm(x, g):
    xf = x.astype(jnp.float32)
    y = xf * lax.rsqrt(jnp.mean(xf * xf, axis=-1, keepdims=True) + NORM_EPS)
    return (y * g.astype(jnp.float32)).astype(x.dtype)


def rope(x, pos):
    half = HEAD_DIM // 2
    inv = jnp.power(ROPE_THETA, -jnp.arange(half, dtype=jnp.float32) / half)
    ang = pos.astype(jnp.float32)[:, None] * inv[None, :]
    cos = jnp.cos(ang)[:, None, :]
    sin = jnp.sin(ang)[:, None, :]
    xf = x.astype(jnp.float32)
    x1, x2 = xf[..., :half], xf[..., half:]
    return jnp.concatenate([x1 * cos - x2 * sin, x2 * cos + x1 * sin], axis=-1).astype(x.dtype)


def masked_softmax(s, mask):
    s = jnp.where(mask, s, NEG_BIG)
    m = jnp.max(s, axis=-1, keepdims=True)
    p = jnp.where(mask, jnp.exp(s - m), 0.0)
    return p / jnp.maximum(jnp.sum(p, axis=-1, keepdims=True), 1e-30)


def swiglu(x, w_gu, w_down):
    g, u = jnp.split(x @ w_gu, 2, axis=-1)
    return (jax.nn.silu(g) * u) @ w_down


def append_rows(past, new, multiple):
    parts = [new] if past is None else [past, new.astype(past.dtype)]
    total = sum(p.shape[1] for p in parts)
    pad = -total % multiple
    if pad:
        parts.append(jnp.zeros((new.shape[0], pad) + new.shape[2:], parts[0].dtype))
    return jnp.concatenate(parts, axis=1) if len(parts) > 1 else parts[0]


def map_query_chunks(fn, xs, block):
    nq = xs[0].shape[1]
    qb = min(block, nq)
    n = -(-nq // qb)
    pad = n * qb - nq

    def prep(a):
        a = jnp.pad(a, [(0, 0), (0, pad)] + [(0, 0)] * (a.ndim - 2))
        a = a.reshape((a.shape[0], n, qb) + a.shape[2:])
        return jnp.moveaxis(a, 1, 0)

    out = lax.map(lambda args: fn(*args), (jnp.arange(n, dtype=jnp.int32),) + tuple(prep(a) for a in xs))
    out = jnp.moveaxis(out, 0, 1)
    out = out.reshape((out.shape[0], n * qb) + out.shape[3:])
    return out[:, :nq]


def stick_breaking_attention(q, k, v, q0):
    kpos = jnp.arange(k.shape[1])
    vf = v.astype(jnp.float32)
    scale = HEAD_DIM ** -0.5

    def block(i, qc):
        qb = qc.shape[1]
        qpos = q0 + i * qb + jnp.arange(qb)
        z = jnp.einsum('bqhd,bkhd->bhqk', qc, k, preferred_element_type=jnp.float32) * scale
        past = (kpos[None, :] < qpos[:, None])[None, None]
        log_keep = jnp.where(past, jax.nn.log_sigmoid(-z), 0.0)
        log_gap = lax.cumsum(log_keep, axis=3, reverse=True) - log_keep
        w = jnp.where(past, jnp.exp(jax.nn.log_sigmoid(z) + log_gap), 0.0)
        return jnp.einsum('bhqk,bkhd->bqhd', w, vf).astype(qc.dtype)

    return map_query_chunks(block, (q,), SB_QBLOCK)


def moba_attention(q, k, v, q0):
    B, Tk, H, D = k.shape
    n_blk = Tk // MOBA_BLOCK
    top = min(MOBA_TOPK, n_blk)
    scale = D ** -0.5
    kb = k.reshape(B, n_blk, MOBA_BLOCK, H, D)
    k_mean = jnp.mean(kb.astype(jnp.float32), axis=2)
    k_blk = jnp.moveaxis(kb, 3, 1)
    v_blk = jnp.moveaxis(v.reshape(B, n_blk, MOBA_BLOCK, H, D), 3, 1)
    b_ix = jnp.arange(B)[:, None, None, None]
    h_ix = jnp.arange(H)[None, None, :, None]
    blk_ids = jnp.arange(n_blk)

    def chunk(i, qc):
        qb = qc.shape[1]
        start = q0 + i * qb
        qpos = start + jnp.arange(qb)
        cur = qpos // MOBA_BLOCK
        gate = jnp.einsum('bqhd,bnhd->bqhn', qc.astype(jnp.float32), k_mean)
        fully_past = (blk_ids[None, :] < cur[:, None])[None, :, None, :]
        gate = jnp.where(fully_past, gate, -jnp.inf)
        _, idx = lax.top_k(gate, top)
        k_sel = k_blk[b_ix, h_ix, idx]
        v_sel = v_blk[b_ix, h_ix, idx]
        ok_sel = jnp.repeat(idx < cur[None, :, None, None], MOBA_BLOCK, axis=-1)
        s_sel = jnp.einsum('bqhd,bqhjsd->bqhjs', qc, k_sel, preferred_element_type=jnp.float32)
        s_sel = s_sel.reshape(B, qb, H, top * MOBA_BLOCK) * scale
        own = start // MOBA_BLOCK
        k_own = lax.dynamic_slice_in_dim(k, own * MOBA_BLOCK, MOBA_BLOCK, axis=1)
        v_own = lax.dynamic_slice_in_dim(v, own * MOBA_BLOCK, MOBA_BLOCK, axis=1)
        own_pos = own * MOBA_BLOCK + jnp.arange(MOBA_BLOCK)
        ok_own = jnp.broadcast_to((own_pos[None, :] <= qpos[:, None])[None, :, None, :], (B, qb, H, MOBA_BLOCK))
        s_own = jnp.einsum('bqhd,bkhd->bqhk', qc, k_own, preferred_element_type=jnp.float32) * scale
        p = masked_softmax(jnp.concatenate([s_sel, s_own], axis=-1), jnp.concatenate([ok_sel, ok_own], axis=-1))
        p_sel = p[..., :top * MOBA_BLOCK].reshape(B, qb, H, top, MOBA_BLOCK)
        p_own = p[..., top * MOBA_BLOCK:]
        o = (jnp.einsum('bqhjs,bqhjsd->bqhd', p_sel, v_sel.astype(jnp.float32))
             + jnp.einsum('bqhk,bkhd->bqhd', p_own, v_own.astype(jnp.float32)))
        return o.astype(qc.dtype)

    return map_query_chunks(chunk, (q,), GATHER_QBLOCK)


def compress_blocks(x, pe, w1, w2):
    B, Tk, H, D = x.shape
    r = CMP_BLOCK // CMP_STRIDE
    c = x.reshape(B, Tk // CMP_STRIDE, CMP_STRIDE, H, D)
    n_cmp = Tk // CMP_STRIDE - r + 1
    blocks = jnp.concatenate([c[:, j:j + n_cmp] for j in range(r)], axis=2)
    blocks = blocks + pe[None, None, :, None, :].astype(x.dtype)
    flat = jnp.moveaxis(blocks, 3, 2).reshape(B, n_cmp, H, CMP_BLOCK * D)
    return jax.nn.silu(flat @ w1) @ w2


def nsa_attention(q, gates, kv, win, q0, cmp_pe, cmp_w1, cmp_w2):
    B, Tk = kv.shape[0], kv.shape[1]
    nq = q.shape[1]
    scale = HEAD_DIM ** -0.5
    k_cmp = compress_blocks(kv[:, :, 0], cmp_pe[0], cmp_w1[0], cmp_w2[0])
    v_cmp = compress_blocks(kv[:, :, 1], cmp_pe[1], cmp_w1[1], cmp_w2[1])
    n_cmp = k_cmp.shape[1]
    cmp_start = jnp.arange(n_cmp) * CMP_STRIDE
    cmp_last = cmp_start + CMP_BLOCK - 1
    n_slc = Tk // SLC_BLOCK
    top = min(SLC_TOPK, n_slc)
    slc_ids = jnp.arange(n_slc)
    slc_start = slc_ids * SLC_BLOCK
    cover = ((cmp_start[:, None] < slc_start[None, :] + SLC_BLOCK)
             & (cmp_last[:, None] >= slc_start[None, :])).astype(jnp.float32)
    k_blk = jnp.moveaxis(kv[:, :, 2].reshape(B, n_slc, SLC_BLOCK, NSA_HKV, HEAD_DIM), 3, 1)
    v_blk = jnp.moveaxis(kv[:, :, 3].reshape(B, n_slc, SLC_BLOCK, NSA_HKV, HEAD_DIM), 3, 1)
    b_ix = jnp.arange(B)[:, None, None, None]
    h_ix = jnp.arange(NSA_HKV)[None, None, :, None]
    qb = min(GATHER_QBLOCK, nq)
    nq_pad = -(-nq // qb) * qb
    win = jnp.pad(win, [(0, 0), (0, nq_pad - nq), (0, 0), (0, 0), (0, 0)])
    k_win, v_win = win[:, :, 0], win[:, :, 1]
    band = WINDOW + qb

    def chunk(i, qc, gc):
        qpos = q0 + i * qb + jnp.arange(qb)
        qg = qc.reshape(B, qb, NSA_HKV, NSA_GROUP, HEAD_DIM)
        s_c = jnp.einsum('bqkgd,bnkd->bqkgn', qg, k_cmp, preferred_element_type=jnp.float32) * scale
        ok_c = (cmp_last[None, :] <= qpos[:, None])[None, :, None, None, :]
        p_c = masked_softmax(s_c, ok_c)
        o_c = jnp.einsum('bqkgn,bnkd->bqkgd', p_c, v_cmp.astype(jnp.float32))
        imp = jnp.einsum('bqkgn,nj->bqkj', p_c, cover)
        cur = qpos // SLC_BLOCK
        forced = (slc_ids[None, :] == 0) | (slc_ids[None, :] == cur[:, None]) | (slc_ids[None, :] == cur[:, None] - 1)
        future = slc_ids[None, :] > cur[:, None]
        score = jnp.where(forced[None, :, None, :], FORCE_SCORE,
                          jnp.where(future[None, :, None, :], -FORCE_SCORE, imp))
        _, idx = lax.top_k(score, top)
        k_sel = k_blk[b_ix, h_ix, idx]
        v_sel = v_blk[b_ix, h_ix, idx]
        pos_sel = idx[..., None] * SLC_BLOCK + jnp.arange(SLC_BLOCK)
        ok_s = (pos_sel <= qpos[None, :, None, None, None]).reshape(B, qb, NSA_HKV, 1, top * SLC_BLOCK)
        s_s = jnp.einsum('bqkgd,bqkjsd->bqkgjs', qg, k_sel, preferred_element_type=jnp.float32)
        s_s = s_s.reshape(B, qb, NSA_HKV, NSA_GROUP, top * SLC_BLOCK) * scale
        p_s = masked_softmax(s_s, ok_s)
        o_s = jnp.einsum('bqkgm,bqkmd->bqkgd', p_s,
                         v_sel.reshape(B, qb, NSA_HKV, top * SLC_BLOCK, HEAD_DIM).astype(jnp.float32))
        k_band = lax.dynamic_slice_in_dim(k_win, i * qb, band, axis=1)
        v_band = lax.dynamic_slice_in_dim(v_win, i * qb, band, axis=1)
        pos_w = q0 - WINDOW + i * qb + jnp.arange(band)
        ok_w = ((pos_w[None, :] >= 0) & (pos_w[None, :] <= qpos[:, None])
                & (qpos[:, None] - pos_w[None, :] < WINDOW))[None, :, None, None, :]
        s_w = jnp.einsum('bqkgd,bskd->bqkgs', qg, k_band, preferred_element_type=jnp.float32) * scale
        p_w = masked_softmax(s_w, ok_w)
        o_w = jnp.einsum('bqkgs,bskd->bqkgd', p_w, v_band.astype(jnp.float32))
        g = gc.reshape(B, qb, NSA_HKV, NSA_GROUP, 3)
        o = g[..., 0:1] * o_c + g[..., 1:2] * o_s + g[..., 2:3] * o_w
        return o.reshape(B, qb, NSA_HQ, HEAD_DIM).astype(qc.dtype)

    return map_query_chunks(chunk, (q, gates), GATHER_QBLOCK)


def even_layer_mixer(xn, q0, past, w_in, g_q, g_k, w_out):
    B, T, _ = xn.shape
    pos = q0 + jnp.arange(T)
    proj = (xn @ w_in).reshape(B, T, 3, N_HEADS, HEAD_DIM)
    q, k, v = proj[:, :, 0], proj[:, :, 1], proj[:, :, 2]
    q_sb = q[:, :, :H_SB]
    q_mb = rope(rms_norm(q[:, :, H_SB:], g_q), pos)
    k_mb = rope(rms_norm(k[:, :, H_SB:], g_k), pos)
    k = jnp.concatenate([k[:, :, :H_SB], k_mb], axis=2)
    new_kv = jnp.stack([k, v], axis=2)
    kv = append_rows(past, new_kv, MOBA_BLOCK)
    o_sb = stick_breaking_attention(q_sb, kv[:, :, 0, :H_SB], kv[:, :, 1, :H_SB], q0)
    o_mb = moba_attention(q_mb, kv[:, :, 0, H_SB:], kv[:, :, 1, H_SB:], q0)
    o = jnp.concatenate([o_sb, o_mb], axis=2).reshape(B, T, N_HEADS * HEAD_DIM)
    return o @ w_out, new_kv


def odd_layer_mixer(xn, q0, past, win_buf, w_in, g_q, g_k, cmp_pe, cmp_w1, cmp_w2, w_out):
    B, T, _ = xn.shape
    pos = q0 + jnp.arange(T)
    proj = xn @ w_in
    n_q = NSA_HQ * HEAD_DIM
    n_kv = 6 * NSA_HKV * HEAD_DIM
    q = rope(rms_norm(proj[..., :n_q].reshape(B, T, NSA_HQ, HEAD_DIM), g_q), pos)
    kvs = proj[..., n_q:n_q + n_kv].reshape(B, T, 6, NSA_HKV, HEAD_DIM)
    gates = jax.nn.sigmoid(proj[..., n_q + n_kv:].astype(jnp.float32)).reshape(B, T, NSA_HQ, 3)
    k_cmp = rope(rms_norm(kvs[:, :, 0], g_k[0]), pos)
    k_slc = rope(rms_norm(kvs[:, :, 2], g_k[1]), pos)
    k_win = rope(rms_norm(kvs[:, :, 4], g_k[2]), pos)
    new_kv = jnp.stack([k_cmp, kvs[:, :, 1], k_slc, kvs[:, :, 3]], axis=2)
    new_win = jnp.stack([k_win, kvs[:, :, 5]], axis=2)
    kv = append_rows(past, new_kv, SLC_BLOCK)
    if win_buf is None:
        win_rows = new_win
        n_keep = min(WINDOW, T)
    else:
        win_rows = jnp.concatenate([win_buf, new_win.astype(win_buf.dtype)], axis=1)
        n_keep = win_buf.shape[1]
    n_before = win_rows.shape[1] - T
    win_arr = jnp.pad(win_rows, [(0, 0), (WINDOW - n_before, 0), (0, 0), (0, 0), (0, 0)])
    new_state = win_rows[:, win_rows.shape[1] - n_keep:]
    o = nsa_attention(q, gates, kv, win_arr, q0, cmp_pe, cmp_w1, cmp_w2)
    return o.reshape(B, T, NSA_HQ * HEAD_DIM) @ w_out, new_kv, new_state


def run_trunk(x, q0, past_kv0, past_kv1, win_bufs,
              g_mix0, w_in0, g_q0, g_k0, w_out0,
              g_mix1, w_in1, g_q1, g_k1, cmp_pe, cmp_w1, cmp_w2, w_out1,
              g_ffn, w_gu, w_down):
    kv0_rows, kv1_rows, win_rows = [], [], []
    h = x
    for layer in range(DEPTH):
        j = layer // 2
        if layer % 2 == 0:
            o, kv_new = even_layer_mixer(rms_norm(h, g_mix0[j]), q0, past_kv0[j],
                                         w_in0[j], g_q0[j], g_k0[j], w_out0[j])
            kv0_rows.append(kv_new)
        else:
            o, kv_new, w_new = odd_layer_mixer(rms_norm(h, g_mix1[j]), q0, past_kv1[j], win_bufs[j],
                                               w_in1[j], g_q1[j], g_k1[j], cmp_pe[j], cmp_w1[j],
                                               cmp_w2[j], w_out1[j])
            kv1_rows.append(kv_new)
            win_rows.append(w_new)
        h = h + o
        h = h + swiglu(rms_norm(h, g_ffn[layer]), w_gu[layer], w_down[layer])
    return h, jnp.stack(kv0_rows), jnp.stack(kv1_rows), jnp.stack(win_rows)


def setup_inputs(seed: int = 0) -> dict:
    key = jax.random.key(seed)
    ks = jax.random.split(key, 24)
    n_pages = PAST_LEN // PAGE_SIZE
    n_pool = (5 * DEC_BATCH * n_pages + 3) // 4
    win_eff = min(WINDOW, PAST_LEN)
    f32 = jnp.float32

    def nrm(k, shape, scale=1.0):
        return jax.random.normal(k, shape, f32) * scale

    def gain(k, shape):
        return 1.0 + 0.02 * jax.random.normal(k, shape, f32)

    page_table = jax.random.permutation(ks[0], n_pool)[:DEC_BATCH * n_pages].reshape(DEC_BATCH, n_pages).astype(jnp.int32)
    return {
        'x_prompt': nrm(ks[1], (BATCH, SEQ, D_MODEL)),
        'x_sample': nrm(ks[2], (DEC_BATCH, DEC_SEQ, D_MODEL)),
        'cache_kv0': nrm(ks[3], (N_EVEN, n_pool, PAGE_SIZE, 2, N_HEADS, HEAD_DIM)),
        'cache_kv1': nrm(ks[4], (N_ODD, n_pool, PAGE_SIZE, 4, NSA_HKV, HEAD_DIM)),
        'state_win': nrm(ks[5], (N_ODD, DEC_BATCH, win_eff, 2, NSA_HKV, HEAD_DIM)),
        'page_table': page_table,
        'g_mix0': gain(ks[6], (N_EVEN, D_MODEL)),
        'w_in0': nrm(ks[7], (N_EVEN, D_MODEL, EVEN_IN), D_MODEL ** -0.5),
        'g_q0': gain(ks[8], (N_EVEN, HEAD_DIM)),
        'g_k0': gain(ks[9], (N_EVEN, HEAD_DIM)),
        'w_out0': nrm(ks[10], (N_EVEN, N_HEADS * HEAD_DIM, D_MODEL), (N_HEADS * HEAD_DIM) ** -0.5),
        'g_mix1': gain(ks[11], (N_ODD, D_MODEL)),
        'w_in1': nrm(ks[12], (N_ODD, D_MODEL, ODD_IN), D_MODEL ** -0.5),
        'g_q1': gain(ks[13], (N_ODD, HEAD_DIM)),
        'g_k1': gain(ks[14], (N_ODD, 3, HEAD_DIM)),
        'cmp_pe': nrm(ks[15], (N_ODD, 2, CMP_BLOCK, HEAD_DIM), 0.1),
        'cmp_w1': nrm(ks[16], (N_ODD, 2, CMP_BLOCK * HEAD_DIM, CMP_HIDDEN), (CMP_BLOCK * HEAD_DIM) ** -0.5),
        'cmp_w2': nrm(ks[17], (N_ODD, 2, CMP_HIDDEN, HEAD_DIM), CMP_HIDDEN ** -0.5),
        'w_out1': nrm(ks[18], (N_ODD, NSA_HQ * HEAD_DIM, D_MODEL), (NSA_HQ * HEAD_DIM) ** -0.5),
        'g_ffn': gain(ks[19], (DEPTH, D_MODEL)),
        'w_gu': nrm(ks[20], (DEPTH, D_MODEL, 2 * D_FF), D_MODEL ** -0.5),
        'w_down': nrm(ks[21], (DEPTH, D_FF, D_MODEL), D_FF ** -0.5),
    }


def reference(x_prompt, x_sample, cache_kv0, cache_kv1, state_win, page_table,
              g_mix0, w_in0, g_q0, g_k0, w_out0,
              g_mix1, w_in1, g_q1, g_k1, cmp_pe, cmp_w1, cmp_w2, w_out1,
              g_ffn, w_gu, w_down):
    y_prompt, kv0_prompt, kv1_prompt, win_prompt = run_trunk(
        x_prompt, 0, [None] * N_EVEN, [None] * N_ODD, [None] * N_ODD,
        g_mix0, w_in0, g_q0, g_k0, w_out0,
        g_mix1, w_in1, g_q1, g_k1, cmp_pe, cmp_w1, cmp_w2, w_out1,
        g_ffn, w_gu, w_down)

    def gather_pages(pool):
        rows = pool[page_table]
        return rows.reshape((rows.shape[0], rows.shape[1] * rows.shape[2]) + rows.shape[3:])

    past0 = [gather_pages(cache_kv0[j]) for j in range(N_EVEN)]
    past1 = [gather_pages(cache_kv1[j]) for j in range(N_ODD)]
    wins = [state_win[j] for j in range(N_ODD)]
    y_sample, kv0_sample, kv1_sample, win_sample = run_trunk(
        x_sample, PAST_LEN, past0, past1, wins,
        g_mix0, w_in0, g_q0, g_k0, w_out0,
        g_mix1, w_in1, g_q1, g_k1, cmp_pe, cmp_w1, cmp_w2, w_out1,
        g_ffn, w_gu, w_down)
    return (y_prompt, y_sample, kv0_prompt, kv0_sample, kv1_prompt, kv1_sample, win_prompt, win_sample)
```

```python
import functools

import numpy as np
import jax
import jax.numpy as jnp
from jax import lax
from jax.experimental import pallas as pl
from jax.experimental.pallas import tpu as pltpu

F32 = jnp.float32
BF16 = jnp.bfloat16

D_MODEL = 1024
HEAD_DIM = 64
N_HEADS = 16
H_SB = 8
PAGE_SIZE = 128
MOBA_BLOCK = 256
MOBA_TOPK = 3
NSA_HKV = 4
NSA_GROUP = 4
CMP_BLOCK = 32
CMP_STRIDE = 16
CMP_HIDDEN = 256
SLC_BLOCK = 64
SLC_TOPK = 16
WINDOW = 512
D_FF = 2816
ROPE_THETA = 10000.0
NORM_EPS = 1e-6
NEG_BIG = -1e30
FORCE_SCORE = 1e9
ATTN_SCALE = HEAD_DIM ** -0.5

LANES = 128
HEADS_PER_SLAB = LANES // HEAD_DIM
VMEM_LIMIT_BYTES = 56 * 1024 * 1024


def _params(*sem):
    return pltpu.CompilerParams(dimension_semantics=sem, vmem_limit_bytes=VMEM_LIMIT_BYTES)


def _iota(shape, dim):
    return lax.broadcasted_iota(jnp.int32, shape, dim)


def _dot(a, b):
    return jnp.dot(a, b, preferred_element_type=F32)


def _dot_nt(a, b):
    return lax.dot_general(a, b, (((1,), (1,)), ((), ())), preferred_element_type=F32)


def _split_bf16(x):
    hi = x.astype(BF16)
    lo = (x - hi.astype(F32)).astype(BF16)
    return hi, lo


def _dot_exact_rhs(x, rhs_bf16):
    hi, lo = _split_bf16(x)
    return _dot(hi, rhs_bf16) + _dot(lo, rhs_bf16)


def _log_sigmoid_pair(z):
    t = jnp.log1p(jnp.exp(-jnp.abs(z)))
    return jnp.minimum(z, 0.0) - t, jnp.minimum(-z, 0.0) - t


def _osm_init(rows, width):
    return (jnp.full((rows, 1), NEG_BIG, F32), jnp.zeros((rows, 1), F32), jnp.zeros((rows, width), F32))


def _osm_update(state, s, mask, v_bf16):
    m, l, acc = state
    sm = jnp.where(mask, s, NEG_BIG)
    m_new = jnp.maximum(m, jnp.max(sm, axis=1, keepdims=True))
    p = jnp.where(mask, jnp.exp(sm - m_new), 0.0)
    a = jnp.exp(m - m_new)
    return (m_new, a * l + jnp.sum(p, axis=1, keepdims=True), a * acc + _dot(p.astype(BF16), v_bf16))


def _osm_final(state):
    _, l, acc = state
    return acc / jnp.maximum(l, 1e-30)


def _top_rank(score, n_real):
    j = _iota(score.shape, 1)
    rank = jnp.zeros(score.shape, jnp.int32)
    for m in range(n_real):
        col = score[:, m:m + 1]
        ahead = (col > score) | ((col == score) & (j > m))
        rank = rank + jnp.where(ahead, 1, 0)
    return rank


def _norm_matmul_kernel(x_ref, g_ref, w_ref, o_ref, xn_ref):
    @pl.when(pl.program_id(1) == 0)
    def _():
        x = x_ref[...]
        ms = jnp.mean(x * x, axis=-1, keepdims=True)
        xn_ref[...] = (x * lax.rsqrt(ms + NORM_EPS) * g_ref[...]).astype(BF16)

    o_ref[...] = _dot(xn_ref[...], w_ref[...])


def norm_matmul(x, g, w_bf16, tm, tn):
    m, k = x.shape
    n = w_bf16.shape[1]
    return pl.pallas_call(
        _norm_matmul_kernel,
        grid=(m // tm, n // tn),
        in_specs=[pl.BlockSpec((tm, k), lambda i, j: (i, 0)),
                  pl.BlockSpec((1, k), lambda i, j: (0, 0)),
                  pl.BlockSpec((k, tn), lambda i, j: (0, j))],
        out_specs=pl.BlockSpec((tm, tn), lambda i, j: (i, j)),
        out_shape=jax.ShapeDtypeStruct((m, n), F32),
        scratch_shapes=[pltpu.VMEM((tm, k), BF16)],
        compiler_params=_params("parallel", "arbitrary"),
        name="norm_matmul",
    )(x, g.reshape(1, k), w_bf16)


def _head_post_kernel(p_ref, cos_ref, sin_ref, gain_ref, *out_refs, plan):
    lane = _iota((1, LANES), 1)
    first_half = (lane & (HEAD_DIM - 1)) < HEAD_DIM // 2
    same_head = (_iota((LANES, LANES), 0) // HEAD_DIM) == (_iota((LANES, LANES), 1) // HEAD_DIM)
    head_mean = jnp.where(same_head, 1.0 / HEAD_DIM, 0.0).astype(BF16)
    cos = cos_ref[...]
    sin = sin_ref[...]
    for src, width, mode, gi, oi, dst in plan:
        if mode == "copy":
            out_refs[oi][:, dst:dst + width] = p_ref[:, src:src + width]
        elif mode == "sigmoid":
            x = p_ref[:, src:src + width]
            out_refs[oi][:, dst:dst + width] = 1.0 / (1.0 + jnp.exp(-x))
        else:
            for s in range(width // LANES):
                x = p_ref[:, src + s * LANES:src + (s + 1) * LANES]
                ms = _dot_exact_rhs(x * x, head_mean)
                y = x * lax.rsqrt(ms + NORM_EPS) * gain_ref[gi:gi + 1, :]
                other = jnp.where(first_half, pltpu.roll(y, LANES - HEAD_DIM // 2, 1),
                                  pltpu.roll(y, HEAD_DIM // 2, 1))
                out_refs[oi][:, dst + s * LANES:dst + (s + 1) * LANES] = y * cos + other * sin


def head_post(proj, cos, sin, gains, plan, out_widths, tm):
    m, n = proj.shape
    period = cos.shape[0] // tm
    return pl.pallas_call(
        functools.partial(_head_post_kernel, plan=plan),
        grid=(m // tm,),
        in_specs=[pl.BlockSpec((tm, n), lambda i: (i, 0)),
                  pl.BlockSpec((tm, LANES), lambda i: (i % period, 0)),
                  pl.BlockSpec((tm, LANES), lambda i: (i % period, 0)),
                  pl.BlockSpec(gains.shape, lambda i: (0, 0))],
        out_specs=[pl.BlockSpec((tm, w), lambda i: (i, 0)) for w in out_widths],
        out_shape=[jax.ShapeDtypeStruct((m, w), F32) for w in out_widths],
        compiler_params=_params("parallel"),
        name="head_post",
    )(proj, cos, sin, gains)


def _matmul_residual_kernel(*refs, n_in):
    a_refs, w_refs, r_ref, o_ref = refs[:n_in], refs[n_in:2 * n_in], refs[2 * n_in], refs[2 * n_in + 1]
    acc = r_ref[...]
    for a_ref, w_ref in zip(a_refs, w_refs):
        acc = acc + _dot(a_ref[...].astype(BF16), w_ref[...])
    o_ref[...] = acc


def matmul_residual(a_list, w_list, res, tm):
    m, n = res.shape
    n_in = len(a_list)
    in_specs = ([pl.BlockSpec((tm, a.shape[1]), lambda i: (i, 0)) for a in a_list]
                + [pl.BlockSpec(w.shape, lambda i: (0, 0)) for w in w_list]
                + [pl.BlockSpec((tm, n), lambda i: (i, 0))])
    return pl.pallas_call(
        functools.partial(_matmul_residual_kernel, n_in=n_in),
        grid=(m // tm,),
        in_specs=in_specs,
        out_specs=pl.BlockSpec((tm, n), lambda i: (i, 0)),
        out_shape=jax.ShapeDtypeStruct((m, n), F32),
        compiler_params=_params("parallel"),
        name="matmul_residual",
    )(*a_list, *w_list, res)


def _ffn_kernel(h_ref, g_ref, wg_ref, wu_ref, wd_ref, o_ref, xn_ref, acc_ref):
    j = pl.program_id(1)

    @pl.when(j == 0)
    def _():
        x = h_ref[...]
        ms = jnp.mean(x * x, axis=-1, keepdims=True)
        xn_ref[...] = (x * lax.rsqrt(ms + NORM_EPS) * g_ref[...]).astype(BF16)
        acc_ref[...] = x

    xn = xn_ref[...]
    gate = _dot(xn, wg_ref[...])
    up = _dot(xn, wu_ref[...])
    act = (gate / (1.0 + jnp.exp(-gate))) * up
    acc_ref[...] += _dot(act.astype(BF16), wd_ref[...])

    @pl.when(j == pl.num_programs(1) - 1)
    def _():
        o_ref[...] = acc_ref[...]


def ffn_residual(h, g, w_gu_bf16, w_down_bf16, tm, tf):
    m, k = h.shape
    n_chunks = D_FF // tf
    return pl.pallas_call(
        _ffn_kernel,
        grid=(m // tm, n_chunks),
        in_specs=[pl.BlockSpec((tm, k), lambda i, j: (i, 0)),
                  pl.BlockSpec((1, k), lambda i, j: (0, 0)),
                  pl.BlockSpec((k, tf), lambda i, j: (0, j)),
                  pl.BlockSpec((k, tf), lambda i, j: (0, j + n_chunks)),
                  pl.BlockSpec((tf, k), lambda i, j: (j, 0))],
        out_specs=pl.BlockSpec((tm, k), lambda i, j: (i, 0)),
        out_shape=jax.ShapeDtypeStruct((m, k), F32),
        scratch_shapes=[pltpu.VMEM((tm, k), BF16), pltpu.VMEM((tm, k), F32)],
        compiler_params=_params("parallel", "arbitrary"),
        name="ffn_residual",
    )(h, g.reshape(1, k), w_gu_bf16, w_gu_bf16, w_down_bf16)


ATTN_TQ = 256
ATTN_TK = 256


def _sb_prefill_kernel(q_ref, k_ref, v_ref, o_ref):
    qi = pl.program_id(2)
    tq, tk = ATTN_TQ, ATTN_TK
    q = q_ref[0]
    lane = _iota((1, LANES), 1)
    rowpos = qi * tq + _iota((tq, 1), 0)
    later = jnp.where(_iota((tk, tk), 0) > _iota((tk, tk), 1), 1.0, 0.0).astype(BF16)
    outs = []
    for h in range(HEADS_PER_SLAB):
        in_head = (lane // HEAD_DIM) == h
        qh = jnp.where(in_head, q, 0.0).astype(BF16)

        def body(i, carry, qh=qh):
            c, acc = carry
            kb = qi - i
            ks = k_ref[0, pl.ds(kb * tk, tk), :].astype(BF16)
            vs = v_ref[0, pl.ds(kb * tk, tk), :].astype(BF16)
            z = _dot_nt(qh, ks) * ATTN_SCALE
            past = (kb * tk + _iota((1, tk), 1)) < rowpos
            ls, lk = _log_sigmoid_pair(z)
            lk = jnp.where(past, lk, 0.0)
            gap = _dot_exact_rhs(lk, later) + c
            w = jnp.where(past, jnp.exp(ls + gap), 0.0)
            acc = acc + _dot(w.astype(BF16), vs)
            return c + jnp.sum(lk, axis=1, keepdims=True), acc

        _, acc = lax.fori_loop(0, qi + 1, body, (jnp.zeros((tq, 1), F32), jnp.zeros((tq, LANES), F32)))
        outs.append(acc)
    o_ref[0] = jnp.where((lane // HEAD_DIM) == 0, outs[0], outs[1])


def sb_prefill(q, kv):
    b, t, _ = q.shape
    n_slab = H_SB // HEADS_PER_SLAB
    v_off = N_HEADS // HEADS_PER_SLAB
    return pl.pallas_call(
        _sb_prefill_kernel,
        grid=(b, n_slab, t // ATTN_TQ),
        in_specs=[pl.BlockSpec((1, ATTN_TQ, LANES), lambda bi, s, qi: (bi, qi, s)),
                  pl.BlockSpec((1, t, LANES), lambda bi, s, qi: (bi, 0, s)),
                  pl.BlockSpec((1, t, LANES), lambda bi, s, qi: (bi, 0, v_off + s))],
        out_specs=pl.BlockSpec((1, ATTN_TQ, LANES), lambda bi, s, qi: (bi, qi, s)),
        out_shape=jax.ShapeDtypeStruct((b, t, H_SB * HEAD_DIM), F32),
        compiler_params=_params("parallel", "parallel", "arbitrary"),
        name="sb_prefill",
    )(q, kv, kv)


def _moba_prefill_kernel(q_ref, k_ref, v_ref, o_ref, kmean_ref, *, n_blk):
    qi = pl.program_id(2)
    tq, tk = ATTN_TQ, ATTN_TK
    t_total = n_blk * MOBA_BLOCK

    @pl.when(qi == 0)
    def _():
        kmean_ref[...] = jnp.zeros_like(kmean_ref)
        for n in range(n_blk):
            blk = k_ref[0, n * MOBA_BLOCK:(n + 1) * MOBA_BLOCK, :]
            kmean_ref[n:n + 1, :] = jnp.sum(blk, axis=0, keepdims=True) * (1.0 / MOBA_BLOCK)

    q = q_ref[0]
    lane = _iota((1, LANES), 1)
    rowpos = qi * tq + _iota((tq, 1), 0)
    blk_id = _iota((1, LANES), 1)
    kmean = kmean_ref[...]
    outs = []
    for h in range(HEADS_PER_SLAB):
        in_head = (lane // HEAD_DIM) == h
        qf = jnp.where(in_head, q, 0.0)
        qh = qf.astype(BF16)
        q_hi, q_lo = _split_bf16(qf)
        k_hi, k_lo = _split_bf16(kmean)
        gate = _dot_nt(q_hi, k_hi) + _dot_nt(q_hi, k_lo) + _dot_nt(q_lo, k_hi)
        fully_past = blk_id < qi
        gate = jnp.where(fully_past, gate, -jnp.inf)
        sel = jnp.where(fully_past & (_top_rank(gate, n_blk) < MOBA_TOPK), 1.0, 0.0)

        def body(kb, st, qh=qh, sel=sel):
            ks = k_ref[0, pl.ds(kb * tk, tk), :].astype(BF16)
            vs = v_ref[0, pl.ds(kb * tk, tk), :].astype(BF16)
            s = _dot_nt(qh, ks) * ATTN_SCALE
            picked = jnp.sum(jnp.where(blk_id == kb, sel, 0.0), axis=1, keepdims=True) > 0.5
            limit = jnp.where(kb == qi, rowpos, jnp.where(picked, t_total, -1))
            mask = (kb * tk + _iota((1, tk), 1)) <= limit
            return _osm_update(st, s, mask, vs)

        st = lax.fori_loop(0, qi + 1, body, _osm_init(tq, LANES))
        outs.append(_osm_final(st))
    o_ref[0] = jnp.where((lane // HEAD_DIM) == 0, outs[0], outs[1])


def moba_prefill(q, kv):
    b, t, _ = q.shape
    n_slab = (N_HEADS - H_SB) // HEADS_PER_SLAB
    q_off = H_SB // HEADS_PER_SLAB
    v_off = N_HEADS // HEADS_PER_SLAB
    assert ATTN_TQ == MOBA_BLOCK and t % MOBA_BLOCK == 0 and t // MOBA_BLOCK <= LANES
    return pl.pallas_call(
        functools.partial(_moba_prefill_kernel, n_blk=t // MOBA_BLOCK),
        grid=(b, n_slab, t // ATTN_TQ),
        in_specs=[pl.BlockSpec((1, ATTN_TQ, LANES), lambda bi, s, qi: (bi, qi, q_off + s)),
                  pl.BlockSpec((1, t, LANES), lambda bi, s, qi: (bi, 0, q_off + s)),
                  pl.BlockSpec((1, t, LANES), lambda bi, s, qi: (bi, 0, v_off + q_off + s))],
        out_specs=pl.BlockSpec((1, ATTN_TQ, LANES), lambda bi, s, qi: (bi, qi, s)),
        out_shape=jax.ShapeDtypeStruct((b, t, (N_HEADS - H_SB) * HEAD_DIM), F32),
        scratch_shapes=[pltpu.VMEM((LANES, LANES), F32)],
        compiler_params=_params("parallel", "parallel", "arbitrary"),
        name="moba_prefill",
    )(q, kv, kv)


CHUNKS_PER_PAGE = PAGE_SIZE // CMP_STRIDE
CMP_HALF = CMP_STRIDE * HEAD_DIM


CMP_SLABS = 2 * NSA_HKV // HEADS_PER_SLAB


def _compress_kernel(pt_ref, *refs, n_pages):
    slab_refs = refs[:CMP_SLABS]
    pe_ref, w1_ref, w2_ref, kc_ref, vc_ref, x_ref = refs[CMP_SLABS:]
    p = pl.program_id(1)
    lane = _iota((1, LANES), 1)
    low = lane < HEAD_DIM
    for kv in range(2):
        for s in range(NSA_HKV // HEADS_PER_SLAB):
            slab_ref = slab_refs[kv * (NSA_HKV // HEADS_PER_SLAB) + s]
            for pp in range(CMP_STRIDE // 2):
                even = slab_ref[0, pl.ds(2 * pp, CHUNKS_PER_PAGE, stride=CMP_STRIDE), :]
                odd = slab_ref[0, pl.ds(2 * pp + 1, CHUNKS_PER_PAGE, stride=CMP_STRIDE), :]
                head0 = jnp.where(low, even, pltpu.roll(odd, HEAD_DIM, 1))
                head1 = jnp.where(low, pltpu.roll(even, HEAD_DIM, 1), odd)
                rows = pl.ds(pl.multiple_of(p * CHUNKS_PER_PAGE, CHUNKS_PER_PAGE), CHUNKS_PER_PAGE)
                x_ref[kv * NSA_HKV + 2 * s, rows, pp * LANES:(pp + 1) * LANES] = head0
                x_ref[kv * NSA_HKV + 2 * s + 1, rows, pp * LANES:(pp + 1) * LANES] = head1

    @pl.when(p == n_pages - 1)
    def _():
        n_chunk = n_pages * CHUNKS_PER_PAGE
        for kv, out_ref in ((0, kc_ref), (1, vc_ref)):
            pe_a = pe_ref[kv, 0:1, :]
            pe_b = pe_ref[kv, 1:2, :]
            w1a = w1_ref[kv, 0:CMP_HALF, :]
            w1b = w1_ref[kv, CMP_HALF:2 * CMP_HALF, :]
            w2 = w2_ref[kv]
            for h in range(NSA_HKV):
                x = x_ref[kv * NSA_HKV + h]
                first = _dot((x + pe_a).astype(BF16), w1a)
                second = _dot((x + pe_b).astype(BF16), w1b)
                hid = first + pltpu.roll(second, n_chunk - 1, 0)
                hid = hid / (1.0 + jnp.exp(-hid))
                out_ref[0, :, h * HEAD_DIM:(h + 1) * HEAD_DIM] = _dot(hid.astype(BF16), w2)


def compress_pages(pages, page_table, pe, w1_bf16, w2_bf16):
    b, n_pages = page_table.shape
    n_chunk = n_pages * CHUNKS_PER_PAGE
    width = NSA_HKV * HEAD_DIM
    grid_spec = pltpu.PrefetchScalarGridSpec(
        num_scalar_prefetch=1,
        grid=(b, n_pages),
        in_specs=[pl.BlockSpec((1, PAGE_SIZE, LANES), functools.partial(lambda bi, p, pt, s: (pt[bi, p], 0, s), s=s))
                  for s in range(CMP_SLABS)] + [
                  pl.BlockSpec(pe.shape, lambda bi, p, pt: (0, 0, 0)),
                  pl.BlockSpec(w1_bf16.shape, lambda bi, p, pt: (0, 0, 0)),
                  pl.BlockSpec(w2_bf16.shape, lambda bi, p, pt: (0, 0, 0))],
        out_specs=[pl.BlockSpec((1, n_chunk, width), lambda bi, p, pt: (bi, 0, 0)),
                   pl.BlockSpec((1, n_chunk, width), lambda bi, p, pt: (bi, 0, 0))],
        scratch_shapes=[pltpu.VMEM((2 * NSA_HKV, n_chunk, CMP_HALF), F32)],
    )
    return pl.pallas_call(
        functools.partial(_compress_kernel, n_pages=n_pages),
        grid_spec=grid_spec,
        out_shape=[jax.ShapeDtypeStruct((b, n_chunk, width), F32)] * 2,
        compiler_params=_params("parallel", "arbitrary"),
        name="nsa_compress",
    )(page_table, *([pages] * CMP_SLABS), pe, w1_bf16, w2_bf16)


NSA_TQ = 128
SLC_TK = 256
WIN_TK = 128


def _cover_matrix(n_cmp_rows, n_cols):
    i = _iota((n_cmp_rows, n_cols), 0)
    j = _iota((n_cmp_rows, n_cols), 1)
    ratio = SLC_BLOCK // CMP_STRIDE
    reach = CMP_BLOCK // CMP_STRIDE - 1
    return jnp.where((i >= ratio * j - reach) & (i <= ratio * j + ratio - 1), 1.0, 0.0).astype(BF16)


def _slc_score(imp, cur, n_slc):
    j = _iota(imp.shape, 1)
    forced = (j == 0) | (j == cur) | (j == cur - 1)
    score = jnp.where(forced, FORCE_SCORE, jnp.where(j > cur, -FORCE_SCORE, imp))
    return jnp.where(j < n_slc, score, -jnp.inf)


def _nsa_prefill_kernel(q_ref, slc_ref, win_ref, kc_ref, vc_ref, g_ref, o_ref, *, t):
    qi = pl.program_id(1)
    tq = NSA_TQ
    n_slc = t // SLC_BLOCK
    top = min(SLC_TOPK, n_slc)
    n_cmp_rows = t // CMP_STRIDE
    q = q_ref[0]
    gates = g_ref[0]
    lane = _iota((1, LANES), 1)
    rowpos = qi * tq + _iota((tq, 1), 0)
    rowpos4 = jnp.concatenate([rowpos] * NSA_GROUP, axis=0)
    cover = _cover_matrix(n_cmp_rows, LANES)
    outs = []
    for hk in range(NSA_HKV):
        half, slab = hk % HEADS_PER_SLAB, hk // HEADS_PER_SLAB
        in_head = (lane // HEAD_DIM) == half
        cols = slice(slab * LANES, (slab + 1) * LANES)
        vcols = slice(NSA_HKV * HEAD_DIM + slab * LANES, NSA_HKV * HEAD_DIM + (slab + 1) * LANES)
        parts = []
        for g in range(NSA_GROUP):
            h = hk * NSA_GROUP + g
            qg = q[:, h * HEAD_DIM:(h + 1) * HEAD_DIM]
            parts.append(jnp.concatenate([qg, qg], axis=1))
        qs = jnp.concatenate(parts, axis=0).astype(BF16)

        kc = jnp.where(in_head, kc_ref[0, :, cols], 0.0).astype(BF16)
        vc = vc_ref[0, :, cols].astype(BF16)
        s_c = _dot_nt(qs, kc) * ATTN_SCALE
        ok_c = (CMP_STRIDE * _iota((1, n_cmp_rows), 1) + CMP_BLOCK - 1) <= rowpos4
        sm = jnp.where(ok_c, s_c, NEG_BIG)
        pc = jnp.where(ok_c, jnp.exp(sm - jnp.max(sm, axis=1, keepdims=True)), 0.0)
        pc = pc / jnp.maximum(jnp.sum(pc, axis=1, keepdims=True), 1e-30)
        o_c = _dot(pc.astype(BF16), vc)

        psum = pc[0:tq] + pc[tq:2 * tq] + pc[2 * tq:3 * tq] + pc[3 * tq:4 * tq]
        imp = _dot_exact_rhs(psum, cover)
        score = _slc_score(imp, rowpos // SLC_BLOCK, n_slc)
        sel = jnp.where(_top_rank(score, n_slc) < top, 1.0, 0.0).astype(BF16)

        def slc_body(kt, st, qs=qs, sel=sel, in_head=in_head, cols=cols, vcols=vcols):
            rows = pl.ds(pl.multiple_of(kt * SLC_TK, SLC_TK), SLC_TK)
            ks = jnp.where(in_head, slc_ref[0, rows, cols], 0.0).astype(BF16)
            vs = slc_ref[0, rows, vcols].astype(BF16)
            s = _dot_nt(qs, ks) * ATTN_SCALE
            blk_of_key = kt * (SLC_TK // SLC_BLOCK) + _iota((LANES, SLC_TK), 1) // SLC_BLOCK
            expand = jnp.where(_iota((LANES, SLC_TK), 0) == blk_of_key, 1.0, 0.0).astype(BF16)
            picked = _dot(sel, expand) > 0.5
            mask = picked & ((kt * SLC_TK + _iota((1, SLC_TK), 1)) <= rowpos)
            return _osm_update(st, s, jnp.concatenate([mask] * NSA_GROUP, axis=0), vs)

        n_slc_tiles = (qi * tq + tq + SLC_TK - 1) // SLC_TK
        o_s = _osm_final(lax.fori_loop(0, n_slc_tiles, slc_body, _osm_init(NSA_GROUP * tq, LANES)))

        first_tile = jnp.maximum(qi * (tq // WIN_TK) - WINDOW // WIN_TK, 0)

        def win_body(i, st, qs=qs, in_head=in_head, cols=cols, vcols=vcols):
            kt = first_tile + i
            rows = pl.ds(pl.multiple_of(kt * WIN_TK, WIN_TK), WIN_TK)
            ks = jnp.where(in_head, win_ref[0, rows, cols], 0.0).astype(BF16)
            vs = win_ref[0, rows, vcols].astype(BF16)
            s = _dot_nt(qs, ks) * ATTN_SCALE
            kpos = kt * WIN_TK + _iota((1, WIN_TK), 1)
            mask = (kpos <= rowpos) & (rowpos - kpos < WINDOW)
            return _osm_update(st, s, jnp.concatenate([mask] * NSA_GROUP, axis=0), vs)

        n_win_tiles = qi * (tq // WIN_TK) + tq // WIN_TK - first_tile
        o_w = _osm_final(lax.fori_loop(0, n_win_tiles, win_body, _osm_init(NSA_GROUP * tq, LANES)))

        for g in range(NSA_GROUP):
            rs = slice(g * tq, (g + 1) * tq)
            hs = slice(half * HEAD_DIM, (half + 1) * HEAD_DIM)
            gc = hk * LANES + g * 3
            outs.append(gates[:, gc:gc + 1] * o_c[rs, hs] + gates[:, gc + 1:gc + 2] * o_s[rs, hs]
                        + gates[:, gc + 2:gc + 3] * o_w[rs, hs])
    o_ref[0] = jnp.concatenate(outs, axis=1)


def nsa_prefill(q, kv, win, kc, vc, gates):
    b, t, _ = q.shape
    width = NSA_HKV * HEAD_DIM
    n_chunk = t // CMP_STRIDE
    assert t % SLC_TK == 0 and n_chunk % LANES == 0 and t // SLC_BLOCK <= LANES and NSA_TQ == WIN_TK
    return pl.pallas_call(
        functools.partial(_nsa_prefill_kernel, t=t),
        grid=(b, t // NSA_TQ),
        in_specs=[pl.BlockSpec((1, NSA_TQ, N_HEADS * HEAD_DIM), lambda bi, qi: (bi, qi, 0)),
                  pl.BlockSpec((1, t, 2 * width), lambda bi, qi: (bi, 0, 1)),
                  pl.BlockSpec((1, t, 2 * width), lambda bi, qi: (bi, 0, 0)),
                  pl.BlockSpec((1, n_chunk, width), lambda bi, qi: (bi, 0, 0)),
                  pl.BlockSpec((1, n_chunk, width), lambda bi, qi: (bi, 0, 0)),
                  pl.BlockSpec((1, NSA_TQ, NSA_HKV * LANES), lambda bi, qi: (bi, qi, 0))],
        out_specs=pl.BlockSpec((1, NSA_TQ, N_HEADS * HEAD_DIM), lambda bi, qi: (bi, qi, 0)),
        out_shape=jax.ShapeDtypeStruct((b, t, N_HEADS * HEAD_DIM), F32),
        compiler_params=_params("parallel", "arbitrary"),
        name="nsa_prefill",
    )(q, kv, win, kc, vc, gates)


SB_WIDTH = H_SB * HEAD_DIM
KV0_V_OFF = N_HEADS * HEAD_DIM


def _load_new_rows(dst_ref, new_ref, col, width, dec_seq):
    dst_ref[...] = jnp.zeros_like(dst_ref)
    dst_ref[0:dec_seq, :] = new_ref[0, :, col:col + width]


def _sb_decode_kernel(pt_ref, qbd_ref, new_ref, page_ref, o_ref, ksum_ref,
                      c_ref, acc_ref, kpad_ref, vpad_ref, *, n_pages, dec_seq):
    p = pl.program_id(1)
    qbd = qbd_ref[0].astype(BF16)
    rows = qbd.shape[0]
    later = jnp.where(_iota((PAGE_SIZE, PAGE_SIZE), 0) > _iota((PAGE_SIZE, PAGE_SIZE), 1), 1.0, 0.0).astype(BF16)

    def block(k, v, valid):
        z = _dot_nt(qbd, k.astype(BF16)) * ATTN_SCALE
        ls, lk = _log_sigmoid_pair(z)
        if valid is not None:
            lk = jnp.where(valid, lk, 0.0)
        c = c_ref[...]
        w = jnp.exp(ls + _dot_exact_rhs(lk, later) + c)
        if valid is not None:
            w = jnp.where(valid, w, 0.0)
        acc_ref[...] += _dot(w.astype(BF16), v.astype(BF16))
        c_ref[...] = c + jnp.sum(lk, axis=1, keepdims=True)

    @pl.when(p == 0)
    def _():
        c_ref[...] = jnp.zeros_like(c_ref)
        acc_ref[...] = jnp.zeros_like(acc_ref)
        _load_new_rows(kpad_ref, new_ref, 0, SB_WIDTH, dec_seq)
        _load_new_rows(vpad_ref, new_ref, KV0_V_OFF, SB_WIDTH, dec_seq)
        t_of_row = lax.rem(_iota((rows, 1), 0), dec_seq)
        block(kpad_ref[...], vpad_ref[...], _iota((1, PAGE_SIZE), 1) < t_of_row)

    block(page_ref[0, :, 0:SB_WIDTH], page_ref[0, :, KV0_V_OFF:KV0_V_OFF + SB_WIDTH], None)
    ksum_ref[0, pl.ds(n_pages - 1 - p, 1), :] = jnp.sum(page_ref[0, :, SB_WIDTH:2 * SB_WIDTH], axis=0, keepdims=True)

    @pl.when(p == n_pages - 1)
    def _():
        lane_head = _iota((1, SB_WIDTH), 1) // HEAD_DIM
        out = jnp.zeros((dec_seq, SB_WIDTH), F32)
        for h in range(H_SB):
            out = out + jnp.where(lane_head == h, acc_ref[h * dec_seq:(h + 1) * dec_seq, :], 0.0)
        o_ref[0] = out


def sb_decode(qbd, new_kv, pool, page_table):
    b, n_pages = page_table.shape
    dec_seq = new_kv.shape[1]
    rows = qbd.shape[1]
    grid_spec = pltpu.PrefetchScalarGridSpec(
        num_scalar_prefetch=1,
        grid=(b, n_pages),
        in_specs=[pl.BlockSpec((1, rows, SB_WIDTH), lambda bi, p, pt: (bi, 0, 0)),
                  pl.BlockSpec((1, dec_seq, new_kv.shape[2]), lambda bi, p, pt: (bi, 0, 0)),
                  pl.BlockSpec((1, PAGE_SIZE, pool.shape[2]), lambda bi, p, pt: (pt[bi, n_pages - 1 - p], 0, 0))],
        out_specs=[pl.BlockSpec((1, dec_seq, SB_WIDTH), lambda bi, p, pt: (bi, 0, 0)),
                   pl.BlockSpec((1, n_pages, SB_WIDTH), lambda bi, p, pt: (bi, 0, 0))],
        scratch_shapes=[pltpu.VMEM((rows, 1), F32), pltpu.VMEM((rows, SB_WIDTH), F32),
                        pltpu.VMEM((PAGE_SIZE, SB_WIDTH), F32), pltpu.VMEM((PAGE_SIZE, SB_WIDTH), F32)],
    )
    return pl.pallas_call(
        functools.partial(_sb_decode_kernel, n_pages=n_pages, dec_seq=dec_seq),
        grid_spec=grid_spec,
        out_shape=[jax.ShapeDtypeStruct((b, dec_seq, SB_WIDTH), F32),
                   jax.ShapeDtypeStruct((b, n_pages, SB_WIDTH), F32)],
        compiler_params=_params("parallel", "arbitrary"),
        name="sb_decode",
    )(page_table, qbd, new_kv, pool)


def _moba_decode_kernel(pt_ref, qbd_ref, ksum_ref, new_ref, kpage_ref, vpage_ref, o_ref,
                        sel_ref, m_ref, l_ref, acc_ref, kpad_ref, vpad_ref, *, n_pages, dec_seq, past_len):
    p = pl.program_id(1)
    pages_per_blk = MOBA_BLOCK // PAGE_SIZE
    n_blk = n_pages // pages_per_blk
    qf = qbd_ref[0]
    qb = qf.astype(BF16)
    rows = qf.shape[0]
    t_of_row = lax.rem(_iota((rows, 1), 0), dec_seq)
    blk_id = _iota((1, LANES), 1)

    @pl.when(p == 0)
    def _():
        pair = jnp.where(_iota((LANES, n_pages), 1) // pages_per_blk == _iota((LANES, n_pages), 0), 1.0, 0.0).astype(BF16)
        ps = ksum_ref[0]
        p_hi = ps.astype(BF16)
        p_mid, p_lo = _split_bf16(ps - p_hi.astype(F32))
        kmean = (_dot(pair, p_hi) + _dot(pair, p_mid) + _dot(pair, p_lo)) * (1.0 / MOBA_BLOCK)
        q_hi, q_lo = _split_bf16(qf)
        k_hi, k_lo = _split_bf16(kmean)
        gate = _dot_nt(q_hi, k_hi) + _dot_nt(q_hi, k_lo) + _dot_nt(q_lo, k_hi)
        fully_past = (blk_id < (past_len + t_of_row) // MOBA_BLOCK) & (blk_id < n_blk)
        gate = jnp.where(fully_past, gate, -jnp.inf)
        sel_ref[...] = jnp.where(fully_past & (_top_rank(gate, n_blk) < MOBA_TOPK), 1.0, 0.0)
        m_ref[...] = jnp.full_like(m_ref, NEG_BIG)
        l_ref[...] = jnp.zeros_like(l_ref)
        acc_ref[...] = jnp.zeros_like(acc_ref)

    def step(k, v, mask):
        st = _osm_update((m_ref[...], l_ref[...], acc_ref[...]), _dot_nt(qb, k.astype(BF16)) * ATTN_SCALE,
                         mask, v.astype(BF16))
        m_ref[...], l_ref[...], acc_ref[...] = st

    picked = jnp.sum(jnp.where(blk_id == p // pages_per_blk, sel_ref[...], 0.0), axis=1, keepdims=True) > 0.5
    step(kpage_ref[0], vpage_ref[0], jnp.broadcast_to(picked, (rows, PAGE_SIZE)))

    @pl.when(p == n_pages - 1)
    def _():
        _load_new_rows(kpad_ref, new_ref, SB_WIDTH, SB_WIDTH, dec_seq)
        _load_new_rows(vpad_ref, new_ref, KV0_V_OFF + SB_WIDTH, SB_WIDTH, dec_seq)
        key = _iota((1, PAGE_SIZE), 1)
        step(kpad_ref[...], vpad_ref[...], (key <= t_of_row) & (key < dec_seq))
        res = _osm_final((m_ref[...], l_ref[...], acc_ref[...]))
        lane_head = _iota((1, SB_WIDTH), 1) // HEAD_DIM
        out = jnp.zeros((dec_seq, SB_WIDTH), F32)
        for h in range(N_HEADS - H_SB):
            out = out + jnp.where(lane_head == h, res[h * dec_seq:(h + 1) * dec_seq, :], 0.0)
        o_ref[0] = out


def moba_decode(qbd, ksum, new_kv, pool, page_table, past_len):
    b, n_pages = page_table.shape
    dec_seq = new_kv.shape[1]
    rows = qbd.shape[1]
    assert past_len % MOBA_BLOCK == 0 and n_pages * PAGE_SIZE == past_len and dec_seq <= PAGE_SIZE
    k_blk = SB_WIDTH // SB_WIDTH
    v_blk = (KV0_V_OFF + SB_WIDTH) // SB_WIDTH
    grid_spec = pltpu.PrefetchScalarGridSpec(
        num_scalar_prefetch=1,
        grid=(b, n_pages),
        in_specs=[pl.BlockSpec((1, rows, SB_WIDTH), lambda bi, p, pt: (bi, 0, 0)),
                  pl.BlockSpec((1, n_pages, SB_WIDTH), lambda bi, p, pt: (bi, 0, 0)),
                  pl.BlockSpec((1, dec_seq, new_kv.shape[2]), lambda bi, p, pt: (bi, 0, 0)),
                  pl.BlockSpec((1, PAGE_SIZE, SB_WIDTH), lambda bi, p, pt: (pt[bi, p], 0, k_blk)),
                  pl.BlockSpec((1, PAGE_SIZE, SB_WIDTH), lambda bi, p, pt: (pt[bi, p], 0, v_blk))],
        out_specs=pl.BlockSpec((1, dec_seq, SB_WIDTH), lambda bi, p, pt: (bi, 0, 0)),
        scratch_shapes=[pltpu.VMEM((rows, LANES), F32), pltpu.VMEM((rows, 1), F32), pltpu.VMEM((rows, 1), F32),
                        pltpu.VMEM((rows, SB_WIDTH), F32),
                        pltpu.VMEM((PAGE_SIZE, SB_WIDTH), F32), pltpu.VMEM((PAGE_SIZE, SB_WIDTH), F32)],
    )
    return pl.pallas_call(
        functools.partial(_moba_decode_kernel, n_pages=n_pages, dec_seq=dec_seq, past_len=past_len),
        grid_spec=grid_spec,
        out_shape=jax.ShapeDtypeStruct((b, dec_seq, SB_WIDTH), F32),
        compiler_params=_params("parallel", "arbitrary"),
        name="moba_decode",
    )(page_table, qbd, ksum, new_kv, pool, pool)


KV_WIDTH = NSA_HKV * HEAD_DIM


def _nsa_decode_kernel(pt_ref, qbd_ref, kc_ref, vc_ref, state_ref, new_kv_ref, new_win_ref, g_ref, page_ref,
                       o_ref, sel_ref, oc_ref, ow_ref, m_ref, l_ref, acc_ref, kpad_ref, vpad_ref,
                       *, n_pages, dec_seq, past_len):
    p = pl.program_id(1)
    qb = qbd_ref[0].astype(BF16)
    rows = qb.shape[0]
    grp_rows = NSA_HKV * dec_seq
    n_cmp_rows = kc_ref.shape[1]
    n_slc = past_len // SLC_BLOCK + 1
    slc_lanes = sel_ref.shape[1]
    t_of_row = lax.rem(_iota((rows, 1), 0), dec_seq)
    qpos = past_len + t_of_row
    key = _iota((1, PAGE_SIZE), 1)

    @pl.when(p == 0)
    def _():
        s_c = _dot_nt(qb, kc_ref[0].astype(BF16)) * ATTN_SCALE
        ok_c = (CMP_STRIDE * _iota((1, n_cmp_rows), 1) + CMP_BLOCK - 1) <= qpos
        sm = jnp.where(ok_c, s_c, NEG_BIG)
        pc = jnp.where(ok_c, jnp.exp(sm - jnp.max(sm, axis=1, keepdims=True)), 0.0)
        pc = pc / jnp.maximum(jnp.sum(pc, axis=1, keepdims=True), 1e-30)
        oc_ref[...] = _dot(pc.astype(BF16), vc_ref[0].astype(BF16))
        psum = pc[0:grp_rows]
        for g in range(1, NSA_GROUP):
            psum = psum + pc[g * grp_rows:(g + 1) * grp_rows]
        imp = _dot_exact_rhs(psum, _cover_matrix(n_cmp_rows, slc_lanes))
        score = _slc_score(imp, qpos[0:grp_rows] // SLC_BLOCK, n_slc)
        sel = jnp.where(_top_rank(score, n_slc) < min(SLC_TOPK, n_slc), 1.0, 0.0)
        sel_ref[...] = jnp.concatenate([sel] * NSA_GROUP, axis=0)
        st = _osm_init(rows, KV_WIDTH)
        s_w = _dot_nt(qb, state_ref[0, :, 0:KV_WIDTH].astype(BF16)) * ATTN_SCALE
        st = _osm_update(st, s_w, _iota((1, WINDOW), 1) > t_of_row, state_ref[0, :, KV_WIDTH:2 * KV_WIDTH].astype(BF16))
        _load_new_rows(kpad_ref, new_win_ref, 0, KV_WIDTH, dec_seq)
        _load_new_rows(vpad_ref, new_win_ref, KV_WIDTH, KV_WIDTH, dec_seq)
        s_n = _dot_nt(qb, kpad_ref[...].astype(BF16)) * ATTN_SCALE
        st = _osm_update(st, s_n, (key <= t_of_row) & (key < dec_seq), vpad_ref[...].astype(BF16))
        ow_ref[...] = _osm_final(st)
        m_ref[...] = jnp.full_like(m_ref, NEG_BIG)
        l_ref[...] = jnp.zeros_like(l_ref)
        acc_ref[...] = jnp.zeros_like(acc_ref)

    def step(k, v, mask):
        st = _osm_update((m_ref[...], l_ref[...], acc_ref[...]), _dot_nt(qb, k.astype(BF16)) * ATTN_SCALE,
                         mask, v.astype(BF16))
        m_ref[...], l_ref[...], acc_ref[...] = st

    blk_of_key = p * (PAGE_SIZE // SLC_BLOCK) + _iota((slc_lanes, PAGE_SIZE), 1) // SLC_BLOCK
    expand = jnp.where(_iota((slc_lanes, PAGE_SIZE), 0) == blk_of_key, 1.0, 0.0).astype(BF16)
    step(page_ref[0, :, 0:KV_WIDTH], page_ref[0, :, KV_WIDTH:2 * KV_WIDTH],
         _dot(sel_ref[...].astype(BF16), expand) > 0.5)

    @pl.when(p == n_pages - 1)
    def _():
        _load_new_rows(kpad_ref, new_kv_ref, 2 * KV_WIDTH, KV_WIDTH, dec_seq)
        _load_new_rows(vpad_ref, new_kv_ref, 3 * KV_WIDTH, KV_WIDTH, dec_seq)
        cur = past_len // SLC_BLOCK
        picked = sel_ref[:, cur:cur + 1] > 0.5
        step(kpad_ref[...], vpad_ref[...], picked & (key <= t_of_row) & (key < dec_seq))
        o_s = _osm_final((m_ref[...], l_ref[...], acc_ref[...]))
        o_c = oc_ref[...]
        o_w = ow_ref[...]
        gates = g_ref[0]
        for hk in range(NSA_HKV):
            for g in range(NSA_GROUP):
                rs = slice(g * grp_rows + hk * dec_seq, g * grp_rows + (hk + 1) * dec_seq)
                hs = slice(hk * HEAD_DIM, (hk + 1) * HEAD_DIM)
                gc = hk * LANES + g * 3
                h = hk * NSA_GROUP + g
                o_ref[0, :, h * HEAD_DIM:(h + 1) * HEAD_DIM] = (
                    gates[:, gc:gc + 1] * o_c[rs, hs] + gates[:, gc + 1:gc + 2] * o_s[rs, hs]
                    + gates[:, gc + 2:gc + 3] * o_w[rs, hs])


def nsa_decode(qbd, kc, vc, state_win, new_kv, new_win, gates, pool, page_table, past_len):
    b, n_pages = page_table.shape
    dec_seq = new_kv.shape[1]
    rows = qbd.shape[1]
    n_chunk = kc.shape[1]
    n_slc = past_len // SLC_BLOCK + 1
    slc_lanes = -(-n_slc // LANES) * LANES
    assert past_len % SLC_BLOCK == 0 and past_len >= WINDOW and state_win.shape[1] == WINDOW and dec_seq <= SLC_BLOCK
    grid_spec = pltpu.PrefetchScalarGridSpec(
        num_scalar_prefetch=1,
        grid=(b, n_pages),
        in_specs=[pl.BlockSpec((1, rows, KV_WIDTH), lambda bi, p, pt: (bi, 0, 0)),
                  pl.BlockSpec((1, n_chunk, KV_WIDTH), lambda bi, p, pt: (bi, 0, 0)),
                  pl.BlockSpec((1, n_chunk, KV_WIDTH), lambda bi, p, pt: (bi, 0, 0)),
                  pl.BlockSpec((1, WINDOW, 2 * KV_WIDTH), lambda bi, p, pt: (bi, 0, 0)),
                  pl.BlockSpec((1, dec_seq, 4 * KV_WIDTH), lambda bi, p, pt: (bi, 0, 0)),
                  pl.BlockSpec((1, dec_seq, 2 * KV_WIDTH), lambda bi, p, pt: (bi, 0, 0)),
                  pl.BlockSpec((1, dec_seq, NSA_HKV * LANES), lambda bi, p, pt: (bi, 0, 0)),
                  pl.BlockSpec((1, PAGE_SIZE, 2 * KV_WIDTH), lambda bi, p, pt: (pt[bi, p], 0, 1))],
        out_specs=pl.BlockSpec((1, dec_seq, N_HEADS * HEAD_DIM), lambda bi, p, pt: (bi, 0, 0)),
        scratch_shapes=[pltpu.VMEM((rows, slc_lanes), F32), pltpu.VMEM((rows, KV_WIDTH), F32),
                        pltpu.VMEM((rows, KV_WIDTH), F32), pltpu.VMEM((rows, 1), F32), pltpu.VMEM((rows, 1), F32),
                        pltpu.VMEM((rows, KV_WIDTH), F32),
                        pltpu.VMEM((PAGE_SIZE, KV_WIDTH), F32), pltpu.VMEM((PAGE_SIZE, KV_WIDTH), F32)],
    )
    return pl.pallas_call(
        functools.partial(_nsa_decode_kernel, n_pages=n_pages, dec_seq=dec_seq, past_len=past_len),
        grid_spec=grid_spec,
        out_shape=jax.ShapeDtypeStruct((b, dec_seq, N_HEADS * HEAD_DIM), F32),
        compiler_params=_params("parallel", "arbitrary"),
        name="nsa_decode",
    )(page_table, qbd, kc, vc, state_win, new_kv, new_win, gates, pool)


EVEN_PLAN = (
    (0, 512, "copy", 0, 0, 0),
    (512, 512, "rope", 0, 0, 512),
    (1024, 512, "copy", 0, 1, 0),
    (1536, 512, "rope", 1, 1, 512),
    (2048, 1024, "copy", 0, 1, 1024),
)
ODD_PLAN = (
    (0, 1024, "rope", 0, 0, 0),
    (1024, 256, "rope", 1, 1, 0),
    (1280, 256, "copy", 0, 1, 256),
    (1536, 256, "rope", 2, 1, 512),
    (1792, 256, "copy", 0, 1, 768),
    (2048, 256, "rope", 3, 2, 0),
    (2304, 256, "copy", 0, 2, 256),
    (2560, 512, "sigmoid", 0, 3, 0),
)
ODD_QKV = N_HEADS * HEAD_DIM + 6 * NSA_HKV * HEAD_DIM


def _gate_columns():
    idx = np.full((NSA_HKV * LANES,), ODD_QKV + 3 * N_HEADS, np.int32)
    for h in range(N_HEADS):
        for r in range(3):
            idx[(h // NSA_GROUP) * LANES + (h % NSA_GROUP) * 3 + r] = ODD_QKV + h * 3 + r
    return idx


def _rope_tables(pos):
    half = HEAD_DIM // 2
    inv = jnp.power(ROPE_THETA, -jnp.arange(half, dtype=F32) / half)
    ang = pos.astype(F32)[:, None] * inv[None, :]
    cos, sin = jnp.cos(ang), jnp.sin(ang)
    return (jnp.concatenate([cos] * (LANES // half), axis=1),
            jnp.concatenate([-sin, sin] * (LANES // HEAD_DIM), axis=1))


def _gain_rows(*gains):
    return jnp.stack([jnp.tile(g.astype(F32), LANES // HEAD_DIM) for g in gains])


def _block_diag_queries(q, heads_per_group, n_groups, group_major):
    b, dec, n_heads, _ = q.shape
    h = np.arange(n_heads)
    grp = h // heads_per_group
    onehot = jnp.asarray(np.eye(n_groups, dtype=np.float32)[grp])
    bd = q.transpose(0, 2, 1, 3)[:, :, :, None, :] * onehot[None, :, None, :, None]
    if group_major:
        bd = bd.reshape(b, n_groups, heads_per_group, dec, n_groups, HEAD_DIM).transpose(0, 2, 1, 3, 4, 5)
    return bd.reshape(b, n_heads * dec, n_groups * HEAD_DIM)


def _prepare_weights(w_in0, w_out0, w_in1, w_out1, cmp_w1, cmp_w2, w_gu, w_down):
    w_in1_ext = jnp.concatenate([w_in1[0], jnp.zeros((D_MODEL, 1), w_in1.dtype)], axis=1)
    cols = np.concatenate([np.arange(ODD_QKV, dtype=np.int32), _gate_columns()])
    return dict(
        w_in0=w_in0[0].astype(BF16),
        w_out0_sb=w_out0[0, :SB_WIDTH].astype(BF16),
        w_out0_mb=w_out0[0, SB_WIDTH:].astype(BF16),
        w_in1=w_in1_ext[:, cols].astype(BF16),
        w_out1=w_out1[0].astype(BF16),
        cmp_w1=cmp_w1[0].astype(BF16),
        cmp_w2=cmp_w2[0].astype(BF16),
        w_gu=w_gu.astype(BF16),
        w_down=w_down.astype(BF16),
    )


def _cmp_pe_rows(cmp_pe):
    return cmp_pe[0].reshape(2, CMP_BLOCK // CMP_STRIDE, CMP_HALF)


def _trunk(x, q0, caches, w, g_mix0, g_q0, g_k0, g_mix1, g_q1, g_k1, cmp_pe, g_ffn):
    b, t, _ = x.shape
    m = b * t
    tm = min(512, m)
    xf = x.reshape(m, D_MODEL)
    pos = q0 + jnp.arange(t)
    cos, sin = _rope_tables(pos)
    if m // tm * tm != m or t % tm != 0:
        cos, sin = jnp.tile(cos, (m // t, 1)), jnp.tile(sin, (m // t, 1))

    proj0 = norm_matmul(xf, g_mix0[0], w["w_in0"], tm, 1024)
    q_l0, kv_l0 = head_post(proj0, cos, sin, _gain_rows(g_q0[0], g_k0[0]), EVEN_PLAN, (1024, 2048), tm)
    if caches is None:
        o_sb = sb_prefill(q_l0.reshape(b, t, 1024), kv_l0.reshape(b, t, 2048)).reshape(m, SB_WIDTH)
        o_mb = moba_prefill(q_l0.reshape(b, t, 1024), kv_l0.reshape(b, t, 2048)).reshape(m, SB_WIDTH)
    else:
        pool0, pool1, state_win, page_table = caches
        q4 = q_l0.reshape(b, t, N_HEADS, HEAD_DIM)
        new_kv0 = kv_l0.reshape(b, t, 2048)
        o_sb, ksum = sb_decode(_block_diag_queries(q4[:, :, :H_SB], 1, H_SB, False), new_kv0, pool0, page_table)
        o_mb = moba_decode(_block_diag_queries(q4[:, :, H_SB:], 1, N_HEADS - H_SB, False), ksum, new_kv0, pool0,
                           page_table, q0)
        o_sb, o_mb = o_sb.reshape(m, SB_WIDTH), o_mb.reshape(m, SB_WIDTH)
    h1 = matmul_residual([o_sb, o_mb], [w["w_out0_sb"], w["w_out0_mb"]], xf, tm)
    h2 = ffn_residual(h1, g_ffn[0], w["w_gu"][0], w["w_down"][0], tm, D_FF // 2)

    proj1 = norm_matmul(h2, g_mix1[0], w["w_in1"], tm, 1024)
    gains1 = _gain_rows(g_q1[0], g_k1[0, 0], g_k1[0, 1], g_k1[0, 2])
    q_l1, kv_l1, win_l1, gates = head_post(proj1, cos, sin, gains1, ODD_PLAN, (1024, 1024, 512, 512), tm)
    pe_rows = _cmp_pe_rows(cmp_pe)
    if caches is None:
        ident = jnp.arange(m // PAGE_SIZE, dtype=jnp.int32).reshape(b, t // PAGE_SIZE)
        kc, vc = compress_pages(kv_l1.reshape(m // PAGE_SIZE, PAGE_SIZE, 1024), ident, pe_rows, w["cmp_w1"], w["cmp_w2"])
        o_nsa = nsa_prefill(q_l1.reshape(b, t, 1024), kv_l1.reshape(b, t, 1024), win_l1.reshape(b, t, 512),
                            kc, vc, gates.reshape(b, t, 512)).reshape(m, 1024)
        keep = min(WINDOW, t)
        win_state = win_l1.reshape(b, t, 512)[:, t - keep:]
    else:
        kc, vc = compress_pages(pool1, page_table, pe_rows, w["cmp_w1"], w["cmp_w2"])
        qbd1 = _block_diag_queries(q_l1.reshape(b, t, N_HEADS, HEAD_DIM), NSA_GROUP, NSA_HKV, True)
        new_win = win_l1.reshape(b, t, 512)
        o_nsa = nsa_decode(qbd1, kc, vc, state_win, kv_l1.reshape(b, t, 1024), new_win, gates.reshape(b, t, 512),
                           pool1, page_table, q0).reshape(m, 1024)
        win_state = jnp.concatenate([state_win, new_win], axis=1)[:, t:]
    h3 = matmul_residual([o_nsa], [w["w_out1"]], h2, tm)
    y = ffn_residual(h3, g_ffn[1], w["w_gu"][1], w["w_down"][1], tm, D_FF // 2)

    return (y.reshape(b, t, D_MODEL),
            kv_l0.reshape(1, b, t, 2, N_HEADS, HEAD_DIM),
            kv_l1.reshape(1, b, t, 4, NSA_HKV, HEAD_DIM),
            win_state.reshape(1, b, win_state.shape[1], 2, NSA_HKV, HEAD_DIM))


def kernel(x_prompt, x_sample, cache_kv0, cache_kv1, state_win, page_table, g_mix0, w_in0, g_q0, g_k0, w_out0,
           g_mix1, w_in1, g_q1, g_k1, cmp_pe, cmp_w1, cmp_w2, w_out1, g_ffn, w_gu, w_down):
    assert w_in0.shape[0] == 1 and w_in1.shape[0] == 1, "one even and one odd layer"
    w = _prepare_weights(w_in0, w_out0, w_in1, w_out1, cmp_w1, cmp_w2, w_gu, w_down)
    norms = (g_mix0, g_q0, g_k0, g_mix1, g_q1, g_k1, cmp_pe, g_ffn)
    y_p, kv0_p, kv1_p, win_p = _trunk(x_prompt, 0, None, w, *norms)
    n_pool, page = cache_kv0.shape[1], cache_kv0.shape[2]
    past_len = page_table.shape[1] * page
    caches = (cache_kv0[0].reshape(n_pool, page, 2 * N_HEADS * HEAD_DIM),
              cache_kv1[0].reshape(n_pool, page, 4 * NSA_HKV * HEAD_DIM),
              state_win[0].reshape(state_win.shape[1], state_win.shape[2], 2 * NSA_HKV * HEAD_DIM),
              page_table)
    y_s, kv0_s, kv1_s, win_s = _trunk(x_sample, past_len, caches, w, *norms)
    return (y_p, y_s, kv0_p, kv0_s, kv1_p, kv1_s, win_p, win_s)
```

```python
import functools

import numpy as np
import jax
import jax.numpy as jnp
from jax import lax
from jax.experimental import pallas as pl
from jax.experimental.pallas import tpu as pltpu

F32 = jnp.float32
BF16 = jnp.bfloat16

D_MODEL = 1024
HEAD_DIM = 64
N_HEADS = 16
H_SB = 8
PAGE_SIZE = 128
MOBA_BLOCK = 256
MOBA_TOPK = 3
NSA_HKV = 4
NSA_GROUP = 4
CMP_BLOCK = 32
CMP_STRIDE = 16
CMP_HIDDEN = 256
SLC_BLOCK = 64
SLC_TOPK = 16
WINDOW = 512
D_FF = 2816
ROPE_THETA = 10000.0
NORM_EPS = 1e-6
NEG_BIG = -1e30
FORCE_SCORE = 1e9
ATTN_SCALE = HEAD_DIM ** -0.5

LANES = 128
HEADS_PER_SLAB = LANES // HEAD_DIM
VMEM_LIMIT_BYTES = 56 * 1024 * 1024


def _params(*sem):
    return pltpu.CompilerParams(dimension_semantics=sem, vmem_limit_bytes=VMEM_LIMIT_BYTES)


def _iota(shape, dim):
    return lax.broadcasted_iota(jnp.int32, shape, dim)


def _dot(a, b):
    return jnp.dot(a, b, preferred_element_type=F32)


def _dot_nt(a, b):
    return lax.dot_general(a, b, (((1,), (1,)), ((), ())), preferred_element_type=F32)


def _split_bf16(x):
    hi = x.astype(BF16)
    lo = (x - hi.astype(F32)).astype(BF16)
    return hi, lo


def _dot_exact_rhs(x, rhs_bf16):
    hi, lo = _split_bf16(x)
    return _dot(hi, rhs_bf16) + _dot(lo, rhs_bf16)


def _log_sigmoid_pair(z):
    t = jnp.log1p(jnp.exp(-jnp.abs(z)))
    return jnp.minimum(z, 0.0) - t, jnp.minimum(-z, 0.0) - t


def _later_matrix(n):
    return jnp.where(_iota((n, n), 0) > _iota((n, n), 1), 1.0, 0.0).astype(BF16)


def _osm_init(rows, width):
    return (jnp.full((rows, 1), NEG_BIG, F32), jnp.zeros((rows, 1), F32), jnp.zeros((rows, width), F32))


def _osm_update(state, s, mask, weigh):
    m, l, acc = state
    sm = jnp.where(mask, s, NEG_BIG)
    m_new = jnp.maximum(m, jnp.max(sm, axis=1, keepdims=True))
    p = jnp.where(mask, jnp.exp(sm - m_new), 0.0)
    a = jnp.exp(m - m_new)
    return (m_new, a * l + jnp.sum(p, axis=1, keepdims=True), a * acc + weigh(p.astype(BF16)))


def _osm_final(state):
    _, l, acc = state
    return acc / jnp.maximum(l, 1e-30)


def _osm_reset(m_ref, l_ref, acc_ref):
    m_ref[...] = jnp.full_like(m_ref, NEG_BIG)
    l_ref[...] = jnp.zeros_like(l_ref)
    acc_ref[...] = jnp.zeros_like(acc_ref)


def _osm_update_biased(state, s_biased, weigh):
    m, l, acc = state
    m_new = jnp.maximum(m, jnp.max(s_biased, axis=1, keepdims=True))
    p = jnp.exp(s_biased - m_new)
    a = jnp.exp(m - m_new)
    return (m_new, a * l + jnp.sum(p, axis=1, keepdims=True), a * acc + weigh(p.astype(BF16)))


def _top_rank(score, n_real, axis):
    j = _iota(score.shape, axis)
    rank = jnp.zeros(score.shape, jnp.int32)
    for m in range(n_real):
        other = score[:, m:m + 1] if axis == 1 else score[m:m + 1, :]
        ahead = (other > score) | ((other == score) & (j > m))
        rank = rank + jnp.where(ahead, 1, 0)
    return rank


def _norm_matmul_kernel(x_ref, g_ref, w_ref, o_ref, xn_ref):
    @pl.when(pl.program_id(1) == 0)
    def _():
        x = x_ref[...]
        ms = jnp.mean(x * x, axis=-1, keepdims=True)
        xn_ref[...] = (x * lax.rsqrt(ms + NORM_EPS) * g_ref[...]).astype(BF16)

    o_ref[...] = _dot(xn_ref[...], w_ref[...])


def norm_matmul(x, g, w_bf16, tm, tn):
    m, k = x.shape
    n = w_bf16.shape[1]
    return pl.pallas_call(
        _norm_matmul_kernel,
        grid=(m // tm, n // tn),
        in_specs=[pl.BlockSpec((tm, k), lambda i, j: (i, 0)),
                  pl.BlockSpec((1, k), lambda i, j: (0, 0)),
                  pl.BlockSpec((k, tn), lambda i, j: (0, j))],
        out_specs=pl.BlockSpec((tm, tn), lambda i, j: (i, j)),
        out_shape=jax.ShapeDtypeStruct((m, n), F32),
        scratch_shapes=[pltpu.VMEM((tm, k), BF16)],
        compiler_params=_params("parallel", "arbitrary"),
        name="norm_matmul",
    )(x, g.reshape(1, k), w_bf16)


def _head_post_kernel(p_ref, cos_ref, sin_ref, gain_ref, *out_refs, plan):
    lane = _iota((1, LANES), 1)
    first_half = (lane & (HEAD_DIM - 1)) < HEAD_DIM // 2
    same_head = (_iota((LANES, LANES), 0) // HEAD_DIM) == (_iota((LANES, LANES), 1) // HEAD_DIM)
    head_mean = jnp.where(same_head, 1.0 / HEAD_DIM, 0.0).astype(BF16)
    cos = cos_ref[...]
    sin = sin_ref[...]
    for src, width, mode, gi, oi, dst in plan:
        if mode == "copy":
            out_refs[oi][:, dst:dst + width] = p_ref[:, src:src + width]
        elif mode == "sigmoid":
            x = p_ref[:, src:src + width]
            out_refs[oi][:, dst:dst + width] = 1.0 / (1.0 + jnp.exp(-x))
        else:
            for s in range(width // LANES):
                x = p_ref[:, src + s * LANES:src + (s + 1) * LANES]
                ms = _dot_exact_rhs(x * x, head_mean)
                y = x * lax.rsqrt(ms + NORM_EPS) * gain_ref[gi:gi + 1, :]
                other = jnp.where(first_half, pltpu.roll(y, LANES - HEAD_DIM // 2, 1),
                                  pltpu.roll(y, HEAD_DIM // 2, 1))
                out_refs[oi][:, dst + s * LANES:dst + (s + 1) * LANES] = y * cos + other * sin


def head_post(proj, cos, sin, gains, plan, out_widths, tm):
    m, n = proj.shape
    period = cos.shape[0] // tm
    return pl.pallas_call(
        functools.partial(_head_post_kernel, plan=plan),
        grid=(m // tm,),
        in_specs=[pl.BlockSpec((tm, n), lambda i: (i, 0)),
                  pl.BlockSpec((tm, LANES), lambda i: (i % period, 0)),
                  pl.BlockSpec((tm, LANES), lambda i: (i % period, 0)),
                  pl.BlockSpec(gains.shape, lambda i: (0, 0))],
        out_specs=[pl.BlockSpec((tm, w), lambda i: (i, 0)) for w in out_widths],
        out_shape=[jax.ShapeDtypeStruct((m, w), F32) for w in out_widths],
        compiler_params=_params("parallel"),
        name="head_post",
    )(proj, cos, sin, gains)


def _matmul_residual_kernel(*refs, n_in):
    a_refs, w_refs, r_ref, o_ref = refs[:n_in], refs[n_in:2 * n_in], refs[2 * n_in], refs[2 * n_in + 1]
    acc = r_ref[...]
    for a_ref, w_ref in zip(a_refs, w_refs):
        acc = acc + _dot(a_ref[...].astype(BF16), w_ref[...])
    o_ref[...] = acc


def matmul_residual(a_list, w_list, res, tm):
    m, n = res.shape
    n_in = len(a_list)
    in_specs = ([pl.BlockSpec((tm, a.shape[1]), lambda i: (i, 0)) for a in a_list]
                + [pl.BlockSpec(w.shape, lambda i: (0, 0)) for w in w_list]
                + [pl.BlockSpec((tm, n), lambda i: (i, 0))])
    return pl.pallas_call(
        functools.partial(_matmul_residual_kernel, n_in=n_in),
        grid=(m // tm,),
        in_specs=in_specs,
        out_specs=pl.BlockSpec((tm, n), lambda i: (i, 0)),
        out_shape=jax.ShapeDtypeStruct((m, n), F32),
        compiler_params=_params("parallel"),
        name="matmul_residual",
    )(*a_list, *w_list, res)


def _ffn_kernel(h_ref, g_ref, wg_ref, wu_ref, wd_ref, o_ref, xn_ref, acc_ref):
    j = pl.program_id(1)

    @pl.when(j == 0)
    def _():
        x = h_ref[...]
        ms = jnp.mean(x * x, axis=-1, keepdims=True)
        xn_ref[...] = (x * lax.rsqrt(ms + NORM_EPS) * g_ref[...]).astype(BF16)
        acc_ref[...] = x

    xn = xn_ref[...]
    gate = _dot(xn, wg_ref[...])
    up = _dot(xn, wu_ref[...])
    act = (gate / (1.0 + jnp.exp(-gate))) * up
    acc_ref[...] += _dot(act.astype(BF16), wd_ref[...])

    @pl.when(j == pl.num_programs(1) - 1)
    def _():
        o_ref[...] = acc_ref[...]


def ffn_residual(h, g, w_gu_bf16, w_down_bf16, tm, tf):
    m, k = h.shape
    n_chunks = D_FF // tf
    return pl.pallas_call(
        _ffn_kernel,
        grid=(m // tm, n_chunks),
        in_specs=[pl.BlockSpec((tm, k), lambda i, j: (i, 0)),
                  pl.BlockSpec((1, k), lambda i, j: (0, 0)),
                  pl.BlockSpec((k, tf), lambda i, j: (0, j)),
                  pl.BlockSpec((k, tf), lambda i, j: (0, j + n_chunks)),
                  pl.BlockSpec((tf, k), lambda i, j: (j, 0))],
        out_specs=pl.BlockSpec((tm, k), lambda i, j: (i, 0)),
        out_shape=jax.ShapeDtypeStruct((m, k), F32),
        scratch_shapes=[pltpu.VMEM((tm, k), BF16), pltpu.VMEM((tm, k), F32)],
        compiler_params=_params("parallel", "arbitrary"),
        name="ffn_residual",
    )(h, g.reshape(1, k), w_gu_bf16, w_gu_bf16, w_down_bf16)


ATTN_TQ = 256
ATTN_TK = 256
def _sb_prefill_kernel(q_ref, k_ref, v_ref, o_ref):
    qi = pl.program_id(2)
    tq, tk = ATTN_TQ, ATTN_TK
    q = q_ref[0] * ATTN_SCALE
    lane = _iota((1, LANES), 1)
    rowpos = qi * tq + _iota((tq, 1), 0)
    later = _later_matrix(tk)
    outs = []
    for h in range(HEADS_PER_SLAB):
        qh = jnp.where((lane // HEAD_DIM) == h, q, 0.0).astype(BF16)

        def body(i, carry, qh=qh):
            c, acc = carry
            kb = qi - i
            start = pl.multiple_of(kb * tk, tk)
            ks = k_ref[0, pl.ds(start, tk), :].astype(BF16)
            vs = v_ref[0, pl.ds(start, tk), :].astype(BF16)
            z = _dot_nt(qh, ks)
            past = (kb * tk + _iota((1, tk), 1)) < rowpos
            ls, lk = _log_sigmoid_pair(z)
            lk = jnp.where(past, lk, 0.0)
            gap = _dot_exact_rhs(lk, later) + c
            w = jnp.where(past, jnp.exp(ls + gap), 0.0)
            acc = acc + _dot(w.astype(BF16), vs)
            return c + jnp.sum(lk, axis=1, keepdims=True), acc

        _, acc = lax.fori_loop(0, qi + 1, body, (jnp.zeros((tq, 1), F32), jnp.zeros((tq, LANES), F32)))
        outs.append(acc)
    o_ref[0] = jnp.where((lane // HEAD_DIM) == 0, outs[0], outs[1])


def sb_prefill(q, kv):
    b, t, _ = q.shape
    n_slab = H_SB // HEADS_PER_SLAB
    v_off = N_HEADS // HEADS_PER_SLAB
    return pl.pallas_call(
        _sb_prefill_kernel,
        grid=(b, n_slab, t // ATTN_TQ),
        in_specs=[pl.BlockSpec((1, ATTN_TQ, LANES), lambda bi, s, qi: (bi, qi, s)),
                  pl.BlockSpec((1, t, LANES), lambda bi, s, qi: (bi, 0, s)),
                  pl.BlockSpec((1, t, LANES), lambda bi, s, qi: (bi, 0, v_off + s))],
        out_specs=pl.BlockSpec((1, ATTN_TQ, LANES), lambda bi, s, qi: (bi, qi, s)),
        out_shape=jax.ShapeDtypeStruct((b, t, H_SB * HEAD_DIM), F32),
        compiler_params=_params("parallel", "parallel", "arbitrary"),
        name="sb_prefill",
    )(q, kv, kv)


def _moba_prefill_kernel(q_ref, k_ref, v_ref, o_ref, kmean_ref, *, n_blk):
    qi = pl.program_id(2)
    tq, tk = ATTN_TQ, ATTN_TK
    t_total = n_blk * MOBA_BLOCK

    @pl.when(qi == 0)
    def _():
        kmean_ref[...] = jnp.zeros_like(kmean_ref)
        for n in range(n_blk):
            blk = k_ref[0, n * MOBA_BLOCK:(n + 1) * MOBA_BLOCK, :]
            kmean_ref[n:n + 1, :] = jnp.sum(blk, axis=0, keepdims=True) * (1.0 / MOBA_BLOCK)

    q = q_ref[0]
    lane = _iota((1, LANES), 1)
    blk_id = _iota((1, LANES), 1)
    k_hi, k_lo = _split_bf16(kmean_ref[...])
    outs = []
    for h in range(HEADS_PER_SLAB):
        qf = jnp.where((lane // HEAD_DIM) == h, q, 0.0)
        qh = (qf * ATTN_SCALE).astype(BF16)
        q_hi, q_lo = _split_bf16(qf)
        gate = _dot_nt(q_hi, k_hi) + _dot_nt(q_hi, k_lo) + _dot_nt(q_lo, k_hi)
        fully_past = blk_id < qi
        gate = jnp.where(fully_past, gate, -jnp.inf)
        sel = jnp.where(fully_past & (_top_rank(gate, n_blk, 1) < MOBA_TOPK), 1.0, 0.0)

        def body(i, st, qh=qh, sel=sel):
            kb = qi - i
            start = pl.multiple_of(kb * tk, tk)
            ks = k_ref[0, pl.ds(start, tk), :].astype(BF16)
            vs = v_ref[0, pl.ds(start, tk), :].astype(BF16)
            picked = jnp.sum(jnp.where(blk_id == kb, sel, 0.0), axis=1, keepdims=True) > 0.5
            limit = jnp.where(kb == qi, qi * tq + _iota((tq, 1), 0), jnp.where(picked, t_total, -1))
            bias = jnp.where((kb * tk + _iota((1, tk), 1)) <= limit, 0.0, NEG_BIG)
            return _osm_update_biased(st, _dot_nt(qh, ks) + bias, lambda p: _dot(p, vs))

        outs.append(_osm_final(lax.fori_loop(0, qi + 1, body, _osm_init(tq, LANES))))
    o_ref[0] = jnp.where((lane // HEAD_DIM) == 0, outs[0], outs[1])


def moba_prefill(q, kv):
    b, t, _ = q.shape
    n_slab = (N_HEADS - H_SB) // HEADS_PER_SLAB
    q_off = H_SB // HEADS_PER_SLAB
    v_off = N_HEADS // HEADS_PER_SLAB
    assert ATTN_TQ == MOBA_BLOCK and t % MOBA_BLOCK == 0 and t // MOBA_BLOCK <= LANES
    return pl.pallas_call(
        functools.partial(_moba_prefill_kernel, n_blk=t // MOBA_BLOCK),
        grid=(b, n_slab, t // ATTN_TQ),
        in_specs=[pl.BlockSpec((1, ATTN_TQ, LANES), lambda bi, s, qi: (bi, qi, q_off + s)),
                  pl.BlockSpec((1, t, LANES), lambda bi, s, qi: (bi, 0, q_off + s)),
                  pl.BlockSpec((1, t, LANES), lambda bi, s, qi: (bi, 0, v_off + q_off + s))],
        out_specs=pl.BlockSpec((1, ATTN_TQ, LANES), lambda bi, s, qi: (bi, qi, s)),
        out_shape=jax.ShapeDtypeStruct((b, t, (N_HEADS - H_SB) * HEAD_DIM), F32),
        scratch_shapes=[pltpu.VMEM((LANES, LANES), F32)],
        compiler_params=_params("parallel", "parallel", "arbitrary"),
        name="moba_prefill",
    )(q, kv, kv)


CHUNKS_PER_PAGE = PAGE_SIZE // CMP_STRIDE
CMP_HALF = CMP_STRIDE * HEAD_DIM
PAGES_PER_STEP = 4


def _page_specs(block, index_of_page):
    return [pl.BlockSpec(block, functools.partial(index_of_page, g=g)) for g in range(PAGES_PER_STEP)]


def _compress_kernel(pt_ref, *refs, n_steps):
    page_refs = refs[:PAGES_PER_STEP]
    pe_ref, w1_ref, w2_ref, kc_ref, vc_ref, x_ref = refs[PAGES_PER_STEP:]
    p = pl.program_id(1)
    lane = _iota((1, LANES), 1)
    low = lane < HEAD_DIM
    r = _iota((PAGE_SIZE, PAGE_SIZE), 0)
    token = _iota((PAGE_SIZE, PAGE_SIZE), 1)
    perm = jnp.where(token == CMP_STRIDE * (r % CHUNKS_PER_PAGE) + r // CHUNKS_PER_PAGE, 1.0, 0.0).astype(BF16)
    for g, page_ref in enumerate(page_refs):
        rows = pl.ds(pl.multiple_of((p * PAGES_PER_STEP + g) * CHUNKS_PER_PAGE, CHUNKS_PER_PAGE), CHUNKS_PER_PAGE)
        for kv in range(2):
            for s in range(NSA_HKV // HEADS_PER_SLAB):
                two_heads = page_ref[0, kv, HEADS_PER_SLAB * s:HEADS_PER_SLAB * (s + 1)].reshape(LANES, PAGE_SIZE)
                hi, lo = _split_bf16(two_heads)
                slab = _dot_nt(perm, hi) + _dot_nt(perm, lo)
                for pp in range(CMP_STRIDE // 2):
                    even = slab[2 * pp * CHUNKS_PER_PAGE:(2 * pp + 1) * CHUNKS_PER_PAGE]
                    odd = slab[(2 * pp + 1) * CHUNKS_PER_PAGE:(2 * pp + 2) * CHUNKS_PER_PAGE]
                    head0 = jnp.where(low, even, pltpu.roll(odd, HEAD_DIM, 1))
                    head1 = jnp.where(low, pltpu.roll(even, HEAD_DIM, 1), odd)
                    x_ref[kv * NSA_HKV + 2 * s, rows, pp * LANES:(pp + 1) * LANES] = head0
                    x_ref[kv * NSA_HKV + 2 * s + 1, rows, pp * LANES:(pp + 1) * LANES] = head1

    @pl.when(p == n_steps - 1)
    def _():
        n_chunk = n_steps * PAGES_PER_STEP * CHUNKS_PER_PAGE
        for kv, out_ref in ((0, kc_ref), (1, vc_ref)):
            pe_a = pe_ref[kv, 0:1, :]
            pe_b = pe_ref[kv, 1:2, :]
            w1a = w1_ref[kv, 0:CMP_HALF, :]
            w1b = w1_ref[kv, CMP_HALF:2 * CMP_HALF, :]
            w2 = w2_ref[kv]
            for h in range(NSA_HKV):
                x = x_ref[kv * NSA_HKV + h]
                first = _dot((x + pe_a).astype(BF16), w1a)
                second = _dot((x + pe_b).astype(BF16), w1b)
                hid = first + pltpu.roll(second, n_chunk - 1, 0)
                hid = hid / (1.0 + jnp.exp(-hid))
                out_ref[0, :, h * HEAD_DIM:(h + 1) * HEAD_DIM] = _dot(hid.astype(BF16), w2)


def compress_pages(pages, page_table, pe, w1_bf16, w2_bf16):
    b, n_pages = page_table.shape
    assert n_pages % PAGES_PER_STEP == 0
    n_chunk = n_pages * CHUNKS_PER_PAGE
    width = NSA_HKV * HEAD_DIM
    grid_spec = pltpu.PrefetchScalarGridSpec(
        num_scalar_prefetch=1,
        grid=(b, n_pages // PAGES_PER_STEP),
        in_specs=_page_specs((1, 2, NSA_HKV, HEAD_DIM, PAGE_SIZE),
                             lambda bi, p, pt, g: (pt[bi, p * PAGES_PER_STEP + g], 0, 0, 0, 0)) + [
                  pl.BlockSpec(pe.shape, lambda bi, p, pt: (0, 0, 0)),
                  pl.BlockSpec(w1_bf16.shape, lambda bi, p, pt: (0, 0, 0)),
                  pl.BlockSpec(w2_bf16.shape, lambda bi, p, pt: (0, 0, 0))],
        out_specs=[pl.BlockSpec((1, n_chunk, width), lambda bi, p, pt: (bi, 0, 0)),
                   pl.BlockSpec((1, n_chunk, width), lambda bi, p, pt: (bi, 0, 0))],
        scratch_shapes=[pltpu.VMEM((2 * NSA_HKV, n_chunk, CMP_HALF), F32)],
    )
    return pl.pallas_call(
        functools.partial(_compress_kernel, n_steps=n_pages // PAGES_PER_STEP),
        grid_spec=grid_spec,
        out_shape=[jax.ShapeDtypeStruct((b, n_chunk, width), F32)] * 2,
        compiler_params=_params("parallel", "arbitrary"),
        name="nsa_compress",
    )(page_table, *([pages] * PAGES_PER_STEP), pe, w1_bf16, w2_bf16)


NSA_TQ = 128
SLC_TK = 256
WIN_TK = 128


def _cover_matrix(n_cmp_rows, n_cols, cmp_axis):
    i = _iota((n_cmp_rows, n_cols) if cmp_axis == 0 else (n_cols, n_cmp_rows), cmp_axis)
    j = _iota((n_cmp_rows, n_cols) if cmp_axis == 0 else (n_cols, n_cmp_rows), 1 - cmp_axis)
    ratio = SLC_BLOCK // CMP_STRIDE
    reach = CMP_BLOCK // CMP_STRIDE - 1
    return jnp.where((i >= ratio * j - reach) & (i <= ratio * j + ratio - 1), 1.0, 0.0).astype(BF16)


def _slc_score(imp, cur, n_slc, axis):
    j = _iota(imp.shape, axis)
    forced = (j == 0) | (j == cur) | (j == cur - 1)
    score = jnp.where(forced, FORCE_SCORE, jnp.where(j > cur, -FORCE_SCORE, imp))
    return jnp.where(j < n_slc, score, -jnp.inf)


def _nsa_prefill_kernel(q_ref, slc_ref, win_ref, kc_ref, vc_ref, g_ref, o_ref, qs_ref, *, t):
    qi = pl.program_id(1)
    tq = NSA_TQ
    n_slc = t // SLC_BLOCK
    top = min(SLC_TOPK, n_slc)
    n_cmp_rows = t // CMP_STRIDE
    q = q_ref[0] * ATTN_SCALE
    gates = g_ref[0]
    lane = _iota((1, LANES), 1)
    rowpos = qi * tq + _iota((tq, 1), 0)
    rowpos4 = jnp.concatenate([rowpos] * NSA_GROUP, axis=0)
    pos_on_lanes = qi * tq + _iota((1, tq), 1)
    cover_t = _cover_matrix(n_cmp_rows, LANES, 1)
    outs = []
    for hk in range(NSA_HKV):
        half, slab = hk % HEADS_PER_SLAB, hk // HEADS_PER_SLAB
        in_head = (lane // HEAD_DIM) == half
        cols = slice(slab * LANES, (slab + 1) * LANES)
        vcols = slice(NSA_HKV * HEAD_DIM + slab * LANES, NSA_HKV * HEAD_DIM + (slab + 1) * LANES)
        for g in range(NSA_GROUP):
            h = hk * NSA_GROUP + g
            qg = q[:, h * HEAD_DIM:(h + 1) * HEAD_DIM]
            qs_ref[g * tq:(g + 1) * tq, :] = jnp.concatenate([qg, qg], axis=1).astype(BF16)
        qs = qs_ref[...]

        kc = jnp.where(in_head, kc_ref[0, :, cols], 0.0).astype(BF16)
        vc = vc_ref[0, :, cols].astype(BF16)
        s_c = _dot_nt(qs, kc)
        ok_c = (CMP_STRIDE * _iota((1, n_cmp_rows), 1) + CMP_BLOCK - 1) <= rowpos4
        sm = jnp.where(ok_c, s_c, NEG_BIG)
        pc = jnp.where(ok_c, jnp.exp(sm - jnp.max(sm, axis=1, keepdims=True)), 0.0)
        pc = pc / jnp.maximum(jnp.sum(pc, axis=1, keepdims=True), 1e-30)
        o_c = _dot(pc.astype(BF16), vc)

        psum = pc[0:tq] + pc[tq:2 * tq] + pc[2 * tq:3 * tq] + pc[3 * tq:4 * tq]
        p_hi, p_lo = _split_bf16(psum)
        imp_t = (_dot_nt(cover_t, p_hi) + _dot_nt(cover_t, p_lo))[0:n_slc]
        score_t = _slc_score(imp_t, pos_on_lanes // SLC_BLOCK, n_slc, 0)
        sel_t = jnp.where(_top_rank(score_t, n_slc, 0) < top, 1.0, 0.0)
        sel_t = jnp.concatenate([sel_t, jnp.zeros((LANES - n_slc, tq), F32)], axis=0)
        sel = sel_t.T.astype(BF16)

        n_slc_tiles = (qi * tq + tq + SLC_TK - 1) // SLC_TK

        def slc_body(i, st, qs=qs, sel=sel, in_head=in_head, cols=cols, vcols=vcols):
            kt = n_slc_tiles - 1 - i
            rows_k = pl.ds(pl.multiple_of(kt * SLC_TK, SLC_TK), SLC_TK)
            ks = jnp.where(in_head, slc_ref[0, rows_k, cols], 0.0).astype(BF16)
            vs = slc_ref[0, rows_k, vcols].astype(BF16)
            blk_of_key = kt * (SLC_TK // SLC_BLOCK) + _iota((LANES, SLC_TK), 1) // SLC_BLOCK
            expand = jnp.where(_iota((LANES, SLC_TK), 0) == blk_of_key, 1.0, 0.0).astype(BF16)
            kpos = kt * SLC_TK + _iota((1, SLC_TK), 1)
            bias = jnp.where((kpos <= rowpos) & (_dot(sel, expand) > 0.5), 0.0, NEG_BIG)
            return _osm_update_biased(st, _dot_nt(qs, ks) + jnp.concatenate([bias] * NSA_GROUP, axis=0),
                                      lambda p: _dot(p, vs))

        o_s = _osm_final(lax.fori_loop(0, n_slc_tiles, slc_body, _osm_init(NSA_GROUP * tq, LANES)))

        first_tile = jnp.maximum(qi * (tq // WIN_TK) - WINDOW // WIN_TK, 0)
        last_tile = qi * (tq // WIN_TK) + tq // WIN_TK - 1

        def win_body(i, st, qs=qs, in_head=in_head, cols=cols, vcols=vcols):
            kt = last_tile - i
            rows_k = pl.ds(pl.multiple_of(kt * WIN_TK, WIN_TK), WIN_TK)
            ks = jnp.where(in_head, win_ref[0, rows_k, cols], 0.0).astype(BF16)
            vs = win_ref[0, rows_k, vcols].astype(BF16)
            kpos = kt * WIN_TK + _iota((1, WIN_TK), 1)
            bias = jnp.where((kpos <= rowpos) & (rowpos - kpos < WINDOW), 0.0, NEG_BIG)
            return _osm_update_biased(st, _dot_nt(qs, ks) + jnp.concatenate([bias] * NSA_GROUP, axis=0),
                                      lambda p: _dot(p, vs))

        o_w = _osm_final(lax.fori_loop(0, last_tile - first_tile + 1, win_body, _osm_init(NSA_GROUP * tq, LANES)))

        for g in range(NSA_GROUP):
            rs = slice(g * tq, (g + 1) * tq)
            hs = slice(half * HEAD_DIM, (half + 1) * HEAD_DIM)
            gc = hk * LANES + g * 3
            outs.append(gates[:, gc:gc + 1] * o_c[rs, hs] + gates[:, gc + 1:gc + 2] * o_s[rs, hs]
                        + gates[:, gc + 2:gc + 3] * o_w[rs, hs])
    o_ref[0] = jnp.concatenate(outs, axis=1)


def nsa_prefill(q, kv, win, kc, vc, gates):
    b, t, _ = q.shape
    width = NSA_HKV * HEAD_DIM
    n_chunk = t // CMP_STRIDE
    n_slc = t // SLC_BLOCK
    assert t % SLC_TK == 0 and n_chunk % LANES == 0 and n_slc <= LANES and n_slc % 8 == 0 and NSA_TQ == WIN_TK
    rows = NSA_GROUP * NSA_TQ
    return pl.pallas_call(
        functools.partial(_nsa_prefill_kernel, t=t),
        grid=(b, t // NSA_TQ),
        in_specs=[pl.BlockSpec((1, NSA_TQ, N_HEADS * HEAD_DIM), lambda bi, qi: (bi, qi, 0)),
                  pl.BlockSpec((1, t, 2 * width), lambda bi, qi: (bi, 0, 1)),
                  pl.BlockSpec((1, t, 2 * width), lambda bi, qi: (bi, 0, 0)),
                  pl.BlockSpec((1, n_chunk, width), lambda bi, qi: (bi, 0, 0)),
                  pl.BlockSpec((1, n_chunk, width), lambda bi, qi: (bi, 0, 0)),
                  pl.BlockSpec((1, NSA_TQ, NSA_HKV * LANES), lambda bi, qi: (bi, qi, 0))],
        out_specs=pl.BlockSpec((1, NSA_TQ, N_HEADS * HEAD_DIM), lambda bi, qi: (bi, qi, 0)),
        out_shape=jax.ShapeDtypeStruct((b, t, N_HEADS * HEAD_DIM), F32),
        scratch_shapes=[pltpu.VMEM((rows, LANES), BF16)],
        compiler_params=_params("parallel", "arbitrary"),
        name="nsa_prefill",
    )(q, kv, win, kc, vc, gates)


SB_WIDTH = H_SB * HEAD_DIM
KV0_V_OFF = N_HEADS * HEAD_DIM
PAGES_PER_MOBA_BLOCK = MOBA_BLOCK // PAGE_SIZE


def _load_new_rows(dst_ref, new_ref, col, width, dec_seq):
    dst_ref[...] = jnp.zeros_like(dst_ref)
    dst_ref[0:dec_seq, :] = new_ref[0, :, col:col + width]


def _heads_by_dim(page_tile):
    return page_tile.reshape(page_tile.shape[0] * HEAD_DIM, page_tile.shape[2])


def _fold_heads(res, dec_seq, n_heads):
    lane_head = _iota((1, n_heads * HEAD_DIM), 1) // HEAD_DIM
    out = jnp.zeros((dec_seq, n_heads * HEAD_DIM), F32)
    for h in range(n_heads):
        out = out + jnp.where(lane_head == h, res[h * dec_seq:(h + 1) * dec_seq, :], 0.0)
    return out


def _sb_decode_kernel(pt_ref, qbd_ref, new_ref, *refs, n_steps, dec_seq):
    k_refs = refs[:PAGES_PER_STEP]
    v_refs = refs[PAGES_PER_STEP:2 * PAGES_PER_STEP]
    o_ref, kmean_ref, c_ref, acc_ref, kpad_ref, vpad_ref = refs[2 * PAGES_PER_STEP:]
    p = pl.program_id(1)
    qbd = (qbd_ref[0] * ATTN_SCALE).astype(BF16)
    rows = qbd.shape[0]
    later = _later_matrix(PAGE_SIZE)
    n_blk = n_steps * PAGES_PER_STEP // PAGES_PER_MOBA_BLOCK

    @pl.when(p == 0)
    def _():
        kmean_ref[...] = jnp.zeros_like(kmean_ref)
        _load_new_rows(kpad_ref, new_ref, 0, SB_WIDTH, dec_seq)
        _load_new_rows(vpad_ref, new_ref, KV0_V_OFF, SB_WIDTH, dec_seq)
        valid = _iota((1, PAGE_SIZE), 1) < lax.rem(_iota((rows, 1), 0), dec_seq)
        ls, lk = _log_sigmoid_pair(_dot_nt(qbd, kpad_ref[...].astype(BF16)))
        lk = jnp.where(valid, lk, 0.0)
        w = jnp.where(valid, jnp.exp(ls + _dot_exact_rhs(lk, later)), 0.0)
        acc_ref[...] = _dot(w.astype(BF16), vpad_ref[...].astype(BF16))
        c_ref[...] = jnp.sum(lk, axis=1, keepdims=True)

    zs = [_dot(qbd, _heads_by_dim(k_ref[0, 0, 0:H_SB]).astype(BF16)) for k_ref in k_refs]
    c = c_ref[...]
    acc = acc_ref[...]
    for z, v_ref in zip(zs, v_refs):
        ls, lk = _log_sigmoid_pair(z)
        w = jnp.exp(ls + _dot_exact_rhs(lk, later) + c)
        acc = acc + _dot_nt(w.astype(BF16), _heads_by_dim(v_ref[0, 0]).astype(BF16))
        c = c + jnp.sum(lk, axis=1, keepdims=True)
    c_ref[...] = c
    acc_ref[...] = acc

    lane = _iota((1, LANES), 1)
    kmean = kmean_ref[0]
    for j in range(PAGES_PER_STEP // PAGES_PER_MOBA_BLOCK):
        total = jnp.zeros(((N_HEADS - H_SB) * HEAD_DIM, 1), F32)
        for k_ref in k_refs[j * PAGES_PER_MOBA_BLOCK:(j + 1) * PAGES_PER_MOBA_BLOCK]:
            total = total + jnp.sum(_heads_by_dim(k_ref[0, 0, H_SB:N_HEADS]), axis=1, keepdims=True)
        blk = n_blk - 1 - (p * (PAGES_PER_STEP // PAGES_PER_MOBA_BLOCK) + j)
        kmean = jnp.where(lane == blk, total * (1.0 / MOBA_BLOCK), kmean)
    kmean_ref[0] = kmean

    @pl.when(p == n_steps - 1)
    def _():
        o_ref[0] = _fold_heads(acc_ref[...], dec_seq, H_SB)


def sb_decode(qbd, new_kv, pool, page_table):
    b, n_pages = page_table.shape
    dec_seq = new_kv.shape[1]
    rows = qbd.shape[1]
    assert n_pages % PAGES_PER_STEP == 0 and PAGES_PER_STEP % PAGES_PER_MOBA_BLOCK == 0
    assert n_pages // PAGES_PER_MOBA_BLOCK <= LANES

    def page_of(bi, p, pt, g):
        return pt[bi, n_pages - 1 - (p * PAGES_PER_STEP + g)]

    grid_spec = pltpu.PrefetchScalarGridSpec(
        num_scalar_prefetch=1,
        grid=(b, n_pages // PAGES_PER_STEP),
        in_specs=[pl.BlockSpec((1, rows, SB_WIDTH), lambda bi, p, pt: (bi, 0, 0)),
                  pl.BlockSpec((1, dec_seq, new_kv.shape[2]), lambda bi, p, pt: (bi, 0, 0))]
        + _page_specs((1, 1, N_HEADS, HEAD_DIM, PAGE_SIZE), lambda bi, p, pt, g: (page_of(bi, p, pt, g), 0, 0, 0, 0))
        + _page_specs((1, 1, H_SB, HEAD_DIM, PAGE_SIZE), lambda bi, p, pt, g: (page_of(bi, p, pt, g), 1, 0, 0, 0)),
        out_specs=[pl.BlockSpec((1, dec_seq, SB_WIDTH), lambda bi, p, pt: (bi, 0, 0)),
                   pl.BlockSpec((1, SB_WIDTH, LANES), lambda bi, p, pt: (bi, 0, 0))],
        scratch_shapes=[pltpu.VMEM((rows, 1), F32), pltpu.VMEM((rows, SB_WIDTH), F32),
                        pltpu.VMEM((PAGE_SIZE, SB_WIDTH), F32), pltpu.VMEM((PAGE_SIZE, SB_WIDTH), F32)],
    )
    return pl.pallas_call(
        functools.partial(_sb_decode_kernel, n_steps=n_pages // PAGES_PER_STEP, dec_seq=dec_seq),
        grid_spec=grid_spec,
        out_shape=[jax.ShapeDtypeStruct((b, dec_seq, SB_WIDTH), F32),
                   jax.ShapeDtypeStruct((b, SB_WIDTH, LANES), F32)],
        compiler_params=_params("parallel", "arbitrary"),
        name="sb_decode",
    )(page_table, qbd, new_kv, *([pool] * (2 * PAGES_PER_STEP)))


def _moba_decode_kernel(pt_ref, qbd_ref, kmean_ref, new_ref, *refs, n_steps, dec_seq, past_len):
    kv_refs = refs[:PAGES_PER_STEP]
    o_ref, sel_ref, m_ref, l_ref, acc_ref, kpad_ref, vpad_ref = refs[PAGES_PER_STEP:]
    p = pl.program_id(1)
    n_blk = n_steps * PAGES_PER_STEP // PAGES_PER_MOBA_BLOCK
    qf = qbd_ref[0]
    qb = (qf * ATTN_SCALE).astype(BF16)
    rows = qf.shape[0]
    t_of_row = lax.rem(_iota((rows, 1), 0), dec_seq)
    blk_id = _iota((1, LANES), 1)

    @pl.when(p == 0)
    def _():
        q_hi, q_lo = _split_bf16(qf)
        k_hi, k_lo = _split_bf16(kmean_ref[0])
        gate = _dot(q_hi, k_hi) + _dot(q_hi, k_lo) + _dot(q_lo, k_hi)
        fully_past = (blk_id < (past_len + t_of_row) // MOBA_BLOCK) & (blk_id < n_blk)
        gate = jnp.where(fully_past, gate, -jnp.inf)
        sel_ref[...] = jnp.where(fully_past & (_top_rank(gate, n_blk, 1) < MOBA_TOPK), 1.0, 0.0)
        _osm_reset(m_ref, l_ref, acc_ref)

    sel = sel_ref[...]
    s_parts, mask_parts = [], []
    for g, kv_ref in enumerate(kv_refs):
        s_parts.append(_dot(qb, _heads_by_dim(kv_ref[0, 0]).astype(BF16)))
        blk = (p * PAGES_PER_STEP + g) // PAGES_PER_MOBA_BLOCK
        picked = jnp.sum(jnp.where(blk_id == blk, sel, 0.0), axis=1, keepdims=True) > 0.5
        mask_parts.append(jnp.broadcast_to(picked, (rows, PAGE_SIZE)))

    def weigh_pages(pb):
        out = jnp.zeros((rows, SB_WIDTH), F32)
        for g, kv_ref in enumerate(kv_refs):
            out = out + _dot_nt(pb[:, g * PAGE_SIZE:(g + 1) * PAGE_SIZE], _heads_by_dim(kv_ref[0, 1]).astype(BF16))
        return out

    st = _osm_update((m_ref[...], l_ref[...], acc_ref[...]), jnp.concatenate(s_parts, axis=1),
                     jnp.concatenate(mask_parts, axis=1), weigh_pages)
    m_ref[...], l_ref[...], acc_ref[...] = st

    @pl.when(p == n_steps - 1)
    def _():
        _load_new_rows(kpad_ref, new_ref, SB_WIDTH, SB_WIDTH, dec_seq)
        _load_new_rows(vpad_ref, new_ref, KV0_V_OFF + SB_WIDTH, SB_WIDTH, dec_seq)
        key = _iota((1, PAGE_SIZE), 1)
        vb = vpad_ref[...].astype(BF16)
        fin = _osm_update((m_ref[...], l_ref[...], acc_ref[...]), _dot_nt(qb, kpad_ref[...].astype(BF16)),
                          (key <= t_of_row) & (key < dec_seq), lambda pb: _dot(pb, vb))
        o_ref[0] = _fold_heads(_osm_final(fin), dec_seq, N_HEADS - H_SB)


def moba_decode(qbd, kmean, new_kv, pool, page_table, past_len):
    b, n_pages = page_table.shape
    dec_seq = new_kv.shape[1]
    rows = qbd.shape[1]
    assert past_len % MOBA_BLOCK == 0 and n_pages * PAGE_SIZE == past_len and dec_seq <= PAGE_SIZE
    assert n_pages % PAGES_PER_STEP == 0
    grid_spec = pltpu.PrefetchScalarGridSpec(
        num_scalar_prefetch=1,
        grid=(b, n_pages // PAGES_PER_STEP),
        in_specs=[pl.BlockSpec((1, rows, SB_WIDTH), lambda bi, p, pt: (bi, 0, 0)),
                  pl.BlockSpec((1, SB_WIDTH, LANES), lambda bi, p, pt: (bi, 0, 0)),
                  pl.BlockSpec((1, dec_seq, new_kv.shape[2]), lambda bi, p, pt: (bi, 0, 0))]
        + _page_specs((1, 2, N_HEADS - H_SB, HEAD_DIM, PAGE_SIZE),
                      lambda bi, p, pt, g: (pt[bi, p * PAGES_PER_STEP + g], 0, 1, 0, 0)),
        out_specs=pl.BlockSpec((1, dec_seq, SB_WIDTH), lambda bi, p, pt: (bi, 0, 0)),
        scratch_shapes=[pltpu.VMEM((rows, LANES), F32), pltpu.VMEM((rows, 1), F32), pltpu.VMEM((rows, 1), F32),
                        pltpu.VMEM((rows, SB_WIDTH), F32),
                        pltpu.VMEM((PAGE_SIZE, SB_WIDTH), F32), pltpu.VMEM((PAGE_SIZE, SB_WIDTH), F32)],
    )
    return pl.pallas_call(
        functools.partial(_moba_decode_kernel, n_steps=n_pages // PAGES_PER_STEP, dec_seq=dec_seq, past_len=past_len),
        grid_spec=grid_spec,
        out_shape=jax.ShapeDtypeStruct((b, dec_seq, SB_WIDTH), F32),
        compiler_params=_params("parallel", "arbitrary"),
        name="moba_decode",
    )(page_table, qbd, kmean, new_kv, *([pool] * PAGES_PER_STEP))


KV_WIDTH = NSA_HKV * HEAD_DIM


def _nsa_decode_kernel(pt_ref, qbd_ref, kc_ref, vc_ref, state_ref, new_kv_ref, new_win_ref, g_ref, *refs,
                       n_steps, dec_seq, past_len):
    page_refs = refs[:PAGES_PER_STEP]
    o_ref, sel_ref, oc_ref, ow_ref, m_ref, l_ref, acc_ref, kpad_ref, vpad_ref = refs[PAGES_PER_STEP:]
    p = pl.program_id(1)
    qb = (qbd_ref[0] * ATTN_SCALE).astype(BF16)
    rows = qb.shape[0]
    grp_rows = NSA_HKV * dec_seq
    n_cmp_rows = kc_ref.shape[1]
    n_slc = past_len // SLC_BLOCK + 1
    slc_lanes = sel_ref.shape[1]
    t_of_row = lax.rem(_iota((rows, 1), 0), dec_seq)
    qpos = past_len + t_of_row
    key = _iota((1, PAGE_SIZE), 1)

    @pl.when(p == 0)
    def _():
        s_c = _dot_nt(qb, kc_ref[0].astype(BF16))
        ok_c = (CMP_STRIDE * _iota((1, n_cmp_rows), 1) + CMP_BLOCK - 1) <= qpos
        sm = jnp.where(ok_c, s_c, NEG_BIG)
        pc = jnp.where(ok_c, jnp.exp(sm - jnp.max(sm, axis=1, keepdims=True)), 0.0)
        pc = pc / jnp.maximum(jnp.sum(pc, axis=1, keepdims=True), 1e-30)
        oc_ref[...] = _dot(pc.astype(BF16), vc_ref[0].astype(BF16))
        psum = pc[0:grp_rows]
        for g in range(1, NSA_GROUP):
            psum = psum + pc[g * grp_rows:(g + 1) * grp_rows]
        imp = _dot_exact_rhs(psum, _cover_matrix(n_cmp_rows, slc_lanes, 0))
        score = _slc_score(imp, qpos[0:grp_rows] // SLC_BLOCK, n_slc, 1)
        sel = jnp.where(_top_rank(score, n_slc, 1) < min(SLC_TOPK, n_slc), 1.0, 0.0)
        sel_ref[...] = jnp.concatenate([sel] * NSA_GROUP, axis=0)
        vwt = _heads_by_dim(state_ref[0, 1]).astype(BF16)
        st = _osm_update(_osm_init(rows, KV_WIDTH), _dot(qb, _heads_by_dim(state_ref[0, 0]).astype(BF16)),
                         _iota((1, WINDOW), 1) > t_of_row, lambda pb: _dot_nt(pb, vwt))
        _load_new_rows(kpad_ref, new_win_ref, 0, KV_WIDTH, dec_seq)
        _load_new_rows(vpad_ref, new_win_ref, KV_WIDTH, KV_WIDTH, dec_seq)
        vb = vpad_ref[...].astype(BF16)
        st = _osm_update(st, _dot_nt(qb, kpad_ref[...].astype(BF16)), (key <= t_of_row) & (key < dec_seq),
                         lambda pb: _dot(pb, vb))
        ow_ref[...] = _osm_final(st)
        _osm_reset(m_ref, l_ref, acc_ref)

    keys_per_step = PAGES_PER_STEP * PAGE_SIZE
    blk_of_key = p * (keys_per_step // SLC_BLOCK) + _iota((slc_lanes, keys_per_step), 1) // SLC_BLOCK
    expand = jnp.where(_iota((slc_lanes, keys_per_step), 0) == blk_of_key, 1.0, 0.0).astype(BF16)
    s_all = jnp.concatenate([_dot(qb, _heads_by_dim(ref[0, 0]).astype(BF16)) for ref in page_refs], axis=1)

    def weigh_pages(pb):
        out = jnp.zeros((rows, KV_WIDTH), F32)
        for g, ref in enumerate(page_refs):
            out = out + _dot_nt(pb[:, g * PAGE_SIZE:(g + 1) * PAGE_SIZE], _heads_by_dim(ref[0, 1]).astype(BF16))
        return out

    st = _osm_update((m_ref[...], l_ref[...], acc_ref[...]), s_all,
                     _dot(sel_ref[...].astype(BF16), expand) > 0.5, weigh_pages)
    m_ref[...], l_ref[...], acc_ref[...] = st

    @pl.when(p == n_steps - 1)
    def _():
        _load_new_rows(kpad_ref, new_kv_ref, 2 * KV_WIDTH, KV_WIDTH, dec_seq)
        _load_new_rows(vpad_ref, new_kv_ref, 3 * KV_WIDTH, KV_WIDTH, dec_seq)
        cur = past_len // SLC_BLOCK
        picked = sel_ref[:, cur:cur + 1] > 0.5
        vb = vpad_ref[...].astype(BF16)
        fin = _osm_update((m_ref[...], l_ref[...], acc_ref[...]), _dot_nt(qb, kpad_ref[...].astype(BF16)),
                          picked & (key <= t_of_row) & (key < dec_seq), lambda pb: _dot(pb, vb))
        o_s = _osm_final(fin)
        o_c = oc_ref[...]
        o_w = ow_ref[...]
        gates = g_ref[0]
        for hk in range(NSA_HKV):
            for g in range(NSA_GROUP):
                rs = slice(g * grp_rows + hk * dec_seq, g * grp_rows + (hk + 1) * dec_seq)
                hs = slice(hk * HEAD_DIM, (hk + 1) * HEAD_DIM)
                gc = hk * LANES + g * 3
                h = hk * NSA_GROUP + g
                o_ref[0, :, h * HEAD_DIM:(h + 1) * HEAD_DIM] = (
                    gates[:, gc:gc + 1] * o_c[rs, hs] + gates[:, gc + 1:gc + 2] * o_s[rs, hs]
                    + gates[:, gc + 2:gc + 3] * o_w[rs, hs])


def nsa_decode(qbd, kc, vc, state_win, new_kv, new_win, gates, pool, page_table, past_len):
    b, n_pages = page_table.shape
    dec_seq = new_kv.shape[1]
    rows = qbd.shape[1]
    n_chunk = kc.shape[1]
    n_slc = past_len // SLC_BLOCK + 1
    slc_lanes = -(-n_slc // LANES) * LANES
    assert past_len % SLC_BLOCK == 0 and past_len >= WINDOW and state_win.shape[-1] == WINDOW and dec_seq <= SLC_BLOCK
    assert n_pages % PAGES_PER_STEP == 0
    grid_spec = pltpu.PrefetchScalarGridSpec(
        num_scalar_prefetch=1,
        grid=(b, n_pages // PAGES_PER_STEP),
        in_specs=[pl.BlockSpec((1, rows, KV_WIDTH), lambda bi, p, pt: (bi, 0, 0)),
                  pl.BlockSpec((1, n_chunk, KV_WIDTH), lambda bi, p, pt: (bi, 0, 0)),
                  pl.BlockSpec((1, n_chunk, KV_WIDTH), lambda bi, p, pt: (bi, 0, 0)),
                  pl.BlockSpec((1, 2, NSA_HKV, HEAD_DIM, WINDOW), lambda bi, p, pt: (bi, 0, 0, 0, 0)),
                  pl.BlockSpec((1, dec_seq, 4 * KV_WIDTH), lambda bi, p, pt: (bi, 0, 0)),
                  pl.BlockSpec((1, dec_seq, 2 * KV_WIDTH), lambda bi, p, pt: (bi, 0, 0)),
                  pl.BlockSpec((1, dec_seq, NSA_HKV * LANES), lambda bi, p, pt: (bi, 0, 0))]
        + _page_specs((1, 2, NSA_HKV, HEAD_DIM, PAGE_SIZE),
                      lambda bi, p, pt, g: (pt[bi, p * PAGES_PER_STEP + g], 1, 0, 0, 0)),
        out_specs=pl.BlockSpec((1, dec_seq, N_HEADS * HEAD_DIM), lambda bi, p, pt: (bi, 0, 0)),
        scratch_shapes=[pltpu.VMEM((rows, slc_lanes), F32), pltpu.VMEM((rows, KV_WIDTH), F32),
                        pltpu.VMEM((rows, KV_WIDTH), F32), pltpu.VMEM((rows, 1), F32), pltpu.VMEM((rows, 1), F32),
                        pltpu.VMEM((rows, KV_WIDTH), F32),
                        pltpu.VMEM((PAGE_SIZE, KV_WIDTH), F32), pltpu.VMEM((PAGE_SIZE, KV_WIDTH), F32)],
    )
    return pl.pallas_call(
        functools.partial(_nsa_decode_kernel, n_steps=n_pages // PAGES_PER_STEP, dec_seq=dec_seq, past_len=past_len),
        grid_spec=grid_spec,
        out_shape=jax.ShapeDtypeStruct((b, dec_seq, N_HEADS * HEAD_DIM), F32),
        compiler_params=_params("parallel", "arbitrary"),
        name="nsa_decode",
    )(page_table, qbd, kc, vc, state_win, new_kv, new_win, gates, *([pool] * PAGES_PER_STEP))


EVEN_PLAN = (
    (0, 512, "copy", 0, 0, 0),
    (512, 512, "rope", 0, 0, 512),
    (1024, 512, "copy", 0, 1, 0),
    (1536, 512, "rope", 1, 1, 512),
    (2048, 1024, "copy", 0, 1, 1024),
)
ODD_PLAN = (
    (0, 1024, "rope", 0, 0, 0),
    (1024, 256, "rope", 1, 1, 0),
    (1280, 256, "copy", 0, 1, 256),
    (1536, 256, "rope", 2, 1, 512),
    (1792, 256, "copy", 0, 1, 768),
    (2048, 256, "rope", 3, 2, 0),
    (2304, 256, "copy", 0, 2, 256),
    (2560, 512, "sigmoid", 0, 3, 0),
)
ODD_QKV = N_HEADS * HEAD_DIM + 6 * NSA_HKV * HEAD_DIM


def _gate_columns():
    idx = np.full((NSA_HKV * LANES,), ODD_QKV + 3 * N_HEADS, np.int32)
    for h in range(N_HEADS):
        for r in range(3):
            idx[(h // NSA_GROUP) * LANES + (h % NSA_GROUP) * 3 + r] = ODD_QKV + h * 3 + r
    return idx


def _rope_tables(pos):
    half = HEAD_DIM // 2
    inv = jnp.power(ROPE_THETA, -jnp.arange(half, dtype=F32) / half)
    ang = pos.astype(F32)[:, None] * inv[None, :]
    cos, sin = jnp.cos(ang), jnp.sin(ang)
    return (jnp.concatenate([cos] * (LANES // half), axis=1),
            jnp.concatenate([-sin, sin] * (LANES // HEAD_DIM), axis=1))


def _gain_rows(*gains):
    return jnp.stack([jnp.tile(g.astype(F32), LANES // HEAD_DIM) for g in gains])


def _block_diag_queries(q, heads_per_group, n_groups, group_major):
    b, dec, n_heads, _ = q.shape
    h = np.arange(n_heads)
    grp = h // heads_per_group
    onehot = jnp.asarray(np.eye(n_groups, dtype=np.float32)[grp])
    bd = q.transpose(0, 2, 1, 3)[:, :, :, None, :] * onehot[None, :, None, :, None]
    if group_major:
        bd = bd.reshape(b, n_groups, heads_per_group, dec, n_groups, HEAD_DIM).transpose(0, 2, 1, 3, 4, 5)
    return bd.reshape(b, n_heads * dec, n_groups * HEAD_DIM)


def _prepare_weights(w_in0, w_out0, w_in1, w_out1, cmp_w1, cmp_w2, w_gu, w_down):
    w_in1_ext = jnp.concatenate([w_in1[0], jnp.zeros((D_MODEL, 1), w_in1.dtype)], axis=1)
    cols = np.concatenate([np.arange(ODD_QKV, dtype=np.int32), _gate_columns()])
    return dict(
        w_in0=w_in0[0].astype(BF16),
        w_out0_sb=w_out0[0, :SB_WIDTH].astype(BF16),
        w_out0_mb=w_out0[0, SB_WIDTH:].astype(BF16),
        w_in1=w_in1_ext[:, cols].astype(BF16),
        w_out1=w_out1[0].astype(BF16),
        cmp_w1=cmp_w1[0].astype(BF16),
        cmp_w2=cmp_w2[0].astype(BF16),
        w_gu=w_gu.astype(BF16),
        w_down=w_down.astype(BF16),
    )


def _cmp_pe_rows(cmp_pe):
    return cmp_pe[0].reshape(2, CMP_BLOCK // CMP_STRIDE, CMP_HALF)


def _tokens_last(x):
    return x.transpose(0, 2, 3, 4, 1)


def _trunk(x, q0, caches, w, g_mix0, g_q0, g_k0, g_mix1, g_q1, g_k1, cmp_pe, g_ffn):
    b, t, _ = x.shape
    m = b * t
    tm = min(512, m)
    xf = x.reshape(m, D_MODEL)
    pos = q0 + jnp.arange(t)
    cos, sin = _rope_tables(pos)
    if m // tm * tm != m or t % tm != 0:
        cos, sin = jnp.tile(cos, (m // t, 1)), jnp.tile(sin, (m // t, 1))

    proj0 = norm_matmul(xf, g_mix0[0], w["w_in0"], tm, 1024)
    q_l0, kv_l0 = head_post(proj0, cos, sin, _gain_rows(g_q0[0], g_k0[0]), EVEN_PLAN, (1024, 2048), tm)
    if caches is None:
        o_sb = sb_prefill(q_l0.reshape(b, t, 1024), kv_l0.reshape(b, t, 2048)).reshape(m, SB_WIDTH)
        o_mb = moba_prefill(q_l0.reshape(b, t, 1024), kv_l0.reshape(b, t, 2048)).reshape(m, SB_WIDTH)
    else:
        pool0, pool1, state_win, page_table = caches
        q4 = q_l0.reshape(b, t, N_HEADS, HEAD_DIM)
        new_kv0 = kv_l0.reshape(b, t, 2048)
        o_sb, kmean = sb_decode(_block_diag_queries(q4[:, :, :H_SB], 1, H_SB, False), new_kv0, pool0, page_table)
        o_mb = moba_decode(_block_diag_queries(q4[:, :, H_SB:], 1, N_HEADS - H_SB, False), kmean, new_kv0, pool0,
                           page_table, q0)
        o_sb, o_mb = o_sb.reshape(m, SB_WIDTH), o_mb.reshape(m, SB_WIDTH)
    h1 = matmul_residual([o_sb, o_mb], [w["w_out0_sb"], w["w_out0_mb"]], xf, tm)
    h2 = ffn_residual(h1, g_ffn[0], w["w_gu"][0], w["w_down"][0], tm, D_FF // 2)

    proj1 = norm_matmul(h2, g_mix1[0], w["w_in1"], tm, 1024)
    gains1 = _gain_rows(g_q1[0], g_k1[0, 0], g_k1[0, 1], g_k1[0, 2])
    q_l1, kv_l1, win_l1, gates = head_post(proj1, cos, sin, gains1, ODD_PLAN, (1024, 1024, 512, 512), tm)
    pe_rows = _cmp_pe_rows(cmp_pe)
    new_win = win_l1.reshape(b, t, 2, NSA_HKV, HEAD_DIM)
    if caches is None:
        ident = jnp.arange(m // PAGE_SIZE, dtype=jnp.int32).reshape(b, t // PAGE_SIZE)
        pages = _tokens_last(kv_l1.reshape(m // PAGE_SIZE, PAGE_SIZE, 4, NSA_HKV, HEAD_DIM))
        kc, vc = compress_pages(pages, ident, pe_rows, w["cmp_w1"], w["cmp_w2"])
        o_nsa = nsa_prefill(q_l1.reshape(b, t, 1024), kv_l1.reshape(b, t, 1024), win_l1.reshape(b, t, 512),
                            kc, vc, gates.reshape(b, t, 512)).reshape(m, 1024)
        win_state = new_win[:, t - min(WINDOW, t):]
    else:
        kc, vc = compress_pages(pool1, page_table, pe_rows, w["cmp_w1"], w["cmp_w2"])
        qbd1 = _block_diag_queries(q_l1.reshape(b, t, N_HEADS, HEAD_DIM), NSA_GROUP, NSA_HKV, True)
        o_nsa = nsa_decode(qbd1, kc, vc, _tokens_last(state_win), kv_l1.reshape(b, t, 1024), win_l1.reshape(b, t, 512),
                           gates.reshape(b, t, 512), pool1, page_table, q0).reshape(m, 1024)
        win_state = jnp.concatenate([state_win[:, t:], new_win], axis=1)
    h3 = matmul_residual([o_nsa], [w["w_out1"]], h2, tm)
    y = ffn_residual(h3, g_ffn[1], w["w_gu"][1], w["w_down"][1], tm, D_FF // 2)

    return (y.reshape(b, t, D_MODEL),
            kv_l0.reshape(1, b, t, 2, N_HEADS, HEAD_DIM),
            kv_l1.reshape(1, b, t, 4, NSA_HKV, HEAD_DIM),
            win_state[None])


def kernel(x_prompt, x_sample, cache_kv0, cache_kv1, state_win, page_table, g_mix0, w_in0, g_q0, g_k0, w_out0,
           g_mix1, w_in1, g_q1, g_k1, cmp_pe, cmp_w1, cmp_w2, w_out1, g_ffn, w_gu, w_down):
    assert w_in0.shape[0] == 1 and w_in1.shape[0] == 1, "one even and one odd layer"
    w = _prepare_weights(w_in0, w_out0, w_in1, w_out1, cmp_w1, cmp_w2, w_gu, w_down)
    norms = (g_mix0, g_q0, g_k0, g_mix1, g_q1, g_k1, cmp_pe, g_ffn)
    y_p, kv0_p, kv1_p, win_p = _trunk(x_prompt, 0, None, w, *norms)
    past_len = page_table.shape[1] * cache_kv0.shape[2]
    caches = (_tokens_last(cache_kv0[0]), _tokens_last(cache_kv1[0]), state_win[0], page_table)
    y_s, kv0_s, kv1_s, win_s = _trunk(x_sample, past_len, caches, w, *norms)
    return (y_p, y_s, kv0_p, kv0_s, kv1_p, kv1_s, win_p, win_s)
```

```python
import functools

import numpy as np
import jax
import jax.numpy as jnp
from jax import lax
from jax.experimental import pallas as pl
from jax.experimental.pallas import tpu as pltpu

F32 = jnp.float32
BF16 = jnp.bfloat16

D_MODEL = 1024
HEAD_DIM = 64
N_HEADS = 16
H_SB = 8
PAGE_SIZE = 128
MOBA_BLOCK = 256
MOBA_TOPK = 3
NSA_HKV = 4
NSA_GROUP = 4
CMP_BLOCK = 32
CMP_STRIDE = 16
CMP_HIDDEN = 256
SLC_BLOCK = 64
SLC_TOPK = 16
WINDOW = 512
D_FF = 2816
ROPE_THETA = 10000.0
NORM_EPS = 1e-6
NEG_BIG = -1e30
FORCE_SCORE = 1e9
ATTN_SCALE = HEAD_DIM ** -0.5

LANES = 128
HEADS_PER_SLAB = LANES // HEAD_DIM
VMEM_LIMIT_BYTES = 56 * 1024 * 1024


def _params(*sem):
    return pltpu.CompilerParams(dimension_semantics=sem, vmem_limit_bytes=VMEM_LIMIT_BYTES)


def _iota(shape, dim):
    return lax.broadcasted_iota(jnp.int32, shape, dim)


def _dot(a, b):
    return jnp.dot(a, b, preferred_element_type=F32)


def _dot_nt(a, b):
    return lax.dot_general(a, b, (((1,), (1,)), ((), ())), preferred_element_type=F32)


def _split_bf16(x):
    hi = x.astype(BF16)
    lo = (x - hi.astype(F32)).astype(BF16)
    return hi, lo


def _dot_exact_rhs(x, rhs_bf16):
    hi, lo = _split_bf16(x)
    return _dot(hi, rhs_bf16) + _dot(lo, rhs_bf16)


def _log_sigmoid_pair(z):
    t = jnp.log1p(jnp.exp(-jnp.abs(z)))
    return jnp.minimum(z, 0.0) - t, jnp.minimum(-z, 0.0) - t


def _later_matrix(n):
    return jnp.where(_iota((n, n), 0) > _iota((n, n), 1), 1.0, 0.0).astype(BF16)


def _osm_init(rows, width):
    return (jnp.full((rows, 1), NEG_BIG, F32), jnp.zeros((rows, 1), F32), jnp.zeros((rows, width), F32))


def _osm_update(state, s, mask, weigh):
    m, l, acc = state
    sm = jnp.where(mask, s, NEG_BIG)
    m_new = jnp.maximum(m, jnp.max(sm, axis=1, keepdims=True))
    p = jnp.where(mask, jnp.exp(sm - m_new), 0.0)
    a = jnp.exp(m - m_new)
    return (m_new, a * l + jnp.sum(p, axis=1, keepdims=True), a * acc + weigh(p.astype(BF16)))


def _osm_final(state):
    _, l, acc = state
    return acc / jnp.maximum(l, 1e-30)


def _osm_reset(m_ref, l_ref, acc_ref):
    m_ref[...] = jnp.full_like(m_ref, NEG_BIG)
    l_ref[...] = jnp.zeros_like(l_ref)
    acc_ref[...] = jnp.zeros_like(acc_ref)


def _osm_update_biased(state, s_biased, weigh):
    m, l, acc = state
    m_new = jnp.maximum(m, jnp.max(s_biased, axis=1, keepdims=True))
    p = jnp.exp(s_biased - m_new)
    a = jnp.exp(m - m_new)
    return (m_new, a * l + jnp.sum(p, axis=1, keepdims=True), a * acc + weigh(p.astype(BF16)))


def _top_rank(score, n_real, axis):
    j = _iota(score.shape, axis)
    rank = jnp.zeros(score.shape, jnp.int32)
    for m in range(n_real):
        other = score[:, m:m + 1] if axis == 1 else score[m:m + 1, :]
        ahead = (other > score) | ((other == score) & (j > m))
        rank = rank + jnp.where(ahead, 1, 0)
    return rank


def _norm_matmul_kernel(x_ref, g_ref, w_ref, o_ref, xn_ref):
    @pl.when(pl.program_id(1) == 0)
    def _():
        x = x_ref[...]
        ms = jnp.mean(x * x, axis=-1, keepdims=True)
        xn_ref[...] = (x * lax.rsqrt(ms + NORM_EPS) * g_ref[...]).astype(BF16)

    o_ref[...] = _dot(xn_ref[...], w_ref[...])


def norm_matmul(x, g, w_bf16, tm, tn):
    m, k = x.shape
    n = w_bf16.shape[1]
    return pl.pallas_call(
        _norm_matmul_kernel,
        grid=(m // tm, n // tn),
        in_specs=[pl.BlockSpec((tm, k), lambda i, j: (i, 0)),
                  pl.BlockSpec((1, k), lambda i, j: (0, 0)),
                  pl.BlockSpec((k, tn), lambda i, j: (0, j))],
        out_specs=pl.BlockSpec((tm, tn), lambda i, j: (i, j)),
        out_shape=jax.ShapeDtypeStruct((m, n), F32),
        scratch_shapes=[pltpu.VMEM((tm, k), BF16)],
        compiler_params=_params("parallel", "arbitrary"),
        name="norm_matmul",
    )(x, g.reshape(1, k), w_bf16)


def _head_post_kernel(p_ref, cos_ref, sin_ref, gain_ref, *out_refs, plan):
    lane = _iota((1, LANES), 1)
    first_half = (lane & (HEAD_DIM - 1)) < HEAD_DIM // 2
    same_head = (_iota((LANES, LANES), 0) // HEAD_DIM) == (_iota((LANES, LANES), 1) // HEAD_DIM)
    head_mean = jnp.where(same_head, 1.0 / HEAD_DIM, 0.0).astype(BF16)
    cos = cos_ref[...]
    sin = sin_ref[...]
    for src, width, mode, gi, oi, dst in plan:
        if mode == "copy":
            out_refs[oi][:, dst:dst + width] = p_ref[:, src:src + width]
        elif mode == "sigmoid":
            x = p_ref[:, src:src + width]
            out_refs[oi][:, dst:dst + width] = 1.0 / (1.0 + jnp.exp(-x))
        else:
            for s in range(width // LANES):
                x = p_ref[:, src + s * LANES:src + (s + 1) * LANES]
                ms = _dot_exact_rhs(x * x, head_mean)
                y = x * lax.rsqrt(ms + NORM_EPS) * gain_ref[gi:gi + 1, :]
                other = jnp.where(first_half, pltpu.roll(y, LANES - HEAD_DIM // 2, 1),
                                  pltpu.roll(y, HEAD_DIM // 2, 1))
                out_refs[oi][:, dst + s * LANES:dst + (s + 1) * LANES] = y * cos + other * sin


def head_post(proj, cos, sin, gains, plan, out_widths, tm):
    m, n = proj.shape
    period = cos.shape[0] // tm
    return pl.pallas_call(
        functools.partial(_head_post_kernel, plan=plan),
        grid=(m // tm,),
        in_specs=[pl.BlockSpec((tm, n), lambda i: (i, 0)),
                  pl.BlockSpec((tm, LANES), lambda i: (i % period, 0)),
                  pl.BlockSpec((tm, LANES), lambda i: (i % period, 0)),
                  pl.BlockSpec(gains.shape, lambda i: (0, 0))],
        out_specs=[pl.BlockSpec((tm, w), lambda i: (i, 0)) for w in out_widths],
        out_shape=[jax.ShapeDtypeStruct((m, w), F32) for w in out_widths],
        compiler_params=_params("parallel"),
        name="head_post",
    )(proj, cos, sin, gains)


def _matmul_residual_kernel(*refs, n_in):
    a_refs, w_refs, r_ref, o_ref = refs[:n_in], refs[n_in:2 * n_in], refs[2 * n_in], refs[2 * n_in + 1]
    acc = r_ref[...]
    for a_ref, w_ref in zip(a_refs, w_refs):
        acc = acc + _dot(a_ref[...].astype(BF16), w_ref[...])
    o_ref[...] = acc


def matmul_residual(a_list, w_list, res, tm):
    m, n = res.shape
    n_in = len(a_list)
    in_specs = ([pl.BlockSpec((tm, a.shape[1]), lambda i: (i, 0)) for a in a_list]
                + [pl.BlockSpec(w.shape, lambda i: (0, 0)) for w in w_list]
                + [pl.BlockSpec((tm, n), lambda i: (i, 0))])
    return pl.pallas_call(
        functools.partial(_matmul_residual_kernel, n_in=n_in),
        grid=(m // tm,),
        in_specs=in_specs,
        out_specs=pl.BlockSpec((tm, n), lambda i: (i, 0)),
        out_shape=jax.ShapeDtypeStruct((m, n), F32),
        compiler_params=_params("parallel"),
        name="matmul_residual",
    )(*a_list, *w_list, res)


def _ffn_kernel(h_ref, g_ref, wg_ref, wu_ref, wd_ref, o_ref, xn_ref, acc_ref):
    j = pl.program_id(1)

    @pl.when(j == 0)
    def _():
        x = h_ref[...]
        ms = jnp.mean(x * x, axis=-1, keepdims=True)
        xn_ref[...] = (x * lax.rsqrt(ms + NORM_EPS) * g_ref[...]).astype(BF16)
        acc_ref[...] = x

    xn = xn_ref[...]
    gate = _dot(xn, wg_ref[...])
    up = _dot(xn, wu_ref[...])
    act = (gate / (1.0 + jnp.exp(-gate))) * up
    acc_ref[...] += _dot(act.astype(BF16), wd_ref[...])

    @pl.when(j == pl.num_programs(1) - 1)
    def _():
        o_ref[...] = acc_ref[...]


def ffn_residual(h, g, w_gu_bf16, w_down_bf16, tm, tf):
    m, k = h.shape
    n_chunks = D_FF // tf
    return pl.pallas_call(
        _ffn_kernel,
        grid=(m // tm, n_chunks),
        in_specs=[pl.BlockSpec((tm, k), lambda i, j: (i, 0)),
                  pl.BlockSpec((1, k), lambda i, j: (0, 0)),
                  pl.BlockSpec((k, tf), lambda i, j: (0, j)),
                  pl.BlockSpec((k, tf), lambda i, j: (0, j + n_chunks)),
                  pl.BlockSpec((tf, k), lambda i, j: (j, 0))],
        out_specs=pl.BlockSpec((tm, k), lambda i, j: (i, 0)),
        out_shape=jax.ShapeDtypeStruct((m, k), F32),
        scratch_shapes=[pltpu.VMEM((tm, k), BF16), pltpu.VMEM((tm, k), F32)],
        compiler_params=_params("parallel", "arbitrary"),
        name="ffn_residual",
    )(h, g.reshape(1, k), w_gu_bf16, w_gu_bf16, w_down_bf16)


ATTN_TQ = 256
ATTN_TK = 256


def _stage_keys_values(k_ref, k_cols, v_ref, v_cols, kb_ref, vt_ref, n_tiles, tk):
    kb_ref[...] = k_ref[0, :, k_cols].astype(BF16)
    for c in range(n_tiles):
        vt_ref[c] = v_ref[0, c * tk:(c + 1) * tk, v_cols].T.astype(BF16)


def _split_heads_t(qt, scale):
    row_head = _iota((LANES, 1), 0) // HEAD_DIM
    return [jnp.where(row_head == h, qt * scale, 0.0) for h in range(HEADS_PER_SLAB)]


def _osm_t_init(width, cols):
    return (jnp.full((1, cols), NEG_BIG, F32), jnp.zeros((1, cols), F32), jnp.zeros((width, cols), F32))


def _osm_t_update(state, s_biased_t, weigh):
    m, l, acc = state
    m_new = jnp.maximum(m, jnp.max(s_biased_t, axis=0, keepdims=True))
    p = jnp.exp(s_biased_t - m_new)
    a = jnp.exp(m - m_new)
    return (m_new, a * l + jnp.sum(p, axis=0, keepdims=True), a * acc + weigh(p.astype(BF16)))


def _osm_t_update2(state, s_a, s_b, weigh_a, weigh_b):
    m, l, acc = state
    m_new = jnp.maximum(m, jnp.maximum(jnp.max(s_a, axis=0, keepdims=True), jnp.max(s_b, axis=0, keepdims=True)))
    p_a = jnp.exp(s_a - m_new)
    p_b = jnp.exp(s_b - m_new)
    a = jnp.exp(m - m_new)
    l_new = a * l + jnp.sum(p_a, axis=0, keepdims=True) + jnp.sum(p_b, axis=0, keepdims=True)
    return (m_new, l_new, a * acc + weigh_a(p_a.astype(BF16)) + weigh_b(p_b.astype(BF16)))


def _osm_t_final(state):
    _, l, acc = state
    return acc / jnp.maximum(l, 1e-30)


def _weigh_heads_t(vt_ref, tile, w, tq):
    return jnp.concatenate([_dot(vt_ref[tile, h * HEAD_DIM:(h + 1) * HEAD_DIM, :], w[:, h * tq:(h + 1) * tq])
                            for h in range(HEADS_PER_SLAB)], axis=1)


def _unstack_heads_t(x, tq):
    return jnp.concatenate([x[:, h * tq:(h + 1) * tq] for h in range(HEADS_PER_SLAB)], axis=0)


def _sb_prefill_kernel(q_ref, k_ref, v_ref, o_ref, kb_ref, vt_ref, *, n_tiles):
    qi = pl.program_id(2)
    tq, tk = ATTN_TQ, ATTN_TK

    @pl.when(qi == 0)
    def _():
        _stage_keys_values(k_ref, slice(None), v_ref, slice(None), kb_ref, vt_ref, n_tiles, tk)

    qpos = jnp.concatenate([qi * tq + _iota((1, tq), 1)] * HEADS_PER_SLAB, axis=1)
    sooner = jnp.where(_iota((tk, tk), 0) < _iota((tk, tk), 1), 1.0, 0.0).astype(BF16)
    qs = jnp.concatenate(_split_heads_t(q_ref[0].T, ATTN_SCALE), axis=1).astype(BF16)

    def body(i, carry):
        c, acc = carry
        kb = qi - i
        z = _dot(kb_ref[pl.ds(pl.multiple_of(kb * tk, tk), tk), :], qs)
        past = (kb * tk + _iota((tk, 1), 0)) < qpos
        ls, lk = _log_sigmoid_pair(z)
        lk = jnp.where(past, lk, 0.0)
        hi, lo = _split_bf16(lk)
        gap = _dot(sooner, hi) + _dot(sooner, lo) + c
        w = jnp.where(past, jnp.exp(ls + gap), 0.0).astype(BF16)
        acc = acc + _weigh_heads_t(vt_ref, kb, w, tq)
        return c + jnp.sum(lk, axis=0, keepdims=True), acc

    _, acc = lax.fori_loop(0, qi + 1, body, (jnp.zeros((1, HEADS_PER_SLAB * tq), F32),
                                             jnp.zeros((HEAD_DIM, HEADS_PER_SLAB * tq), F32)))
    o_ref[0] = _unstack_heads_t(acc, tq).T


def sb_prefill(q, kv):
    b, t, _ = q.shape
    n_slab = H_SB // HEADS_PER_SLAB
    v_off = N_HEADS // HEADS_PER_SLAB
    assert t % ATTN_TK == 0 and ATTN_TQ == ATTN_TK
    return pl.pallas_call(
        functools.partial(_sb_prefill_kernel, n_tiles=t // ATTN_TK),
        grid=(b, n_slab, t // ATTN_TQ),
        in_specs=[pl.BlockSpec((1, ATTN_TQ, LANES), lambda bi, s, qi: (bi, qi, s)),
                  pl.BlockSpec((1, t, LANES), lambda bi, s, qi: (bi, 0, s)),
                  pl.BlockSpec((1, t, LANES), lambda bi, s, qi: (bi, 0, v_off + s))],
        out_specs=pl.BlockSpec((1, ATTN_TQ, LANES), lambda bi, s, qi: (bi, qi, s)),
        out_shape=jax.ShapeDtypeStruct((b, t, H_SB * HEAD_DIM), F32),
        scratch_shapes=[pltpu.VMEM((t, LANES), BF16), pltpu.VMEM((t // ATTN_TK, LANES, ATTN_TK), BF16)],
        compiler_params=_params("parallel", "parallel", "arbitrary"),
        name="sb_prefill",
    )(q, kv, kv)


def _moba_prefill_kernel(q_ref, k_ref, v_ref, o_ref, kmean_ref, kb_ref, vt_ref, *, n_blk):
    qi = pl.program_id(2)
    tq, tk = ATTN_TQ, ATTN_TK
    t_total = n_blk * MOBA_BLOCK
    blk_rows = -(-n_blk // 8) * 8

    @pl.when(qi == 0)
    def _():
        kmean_ref[...] = jnp.zeros_like(kmean_ref)
        for n in range(n_blk):
            blk = k_ref[0, n * MOBA_BLOCK:(n + 1) * MOBA_BLOCK, :]
            kmean_ref[n:n + 1, :] = jnp.sum(blk, axis=0, keepdims=True) * (1.0 / MOBA_BLOCK)
        _stage_keys_values(k_ref, slice(None), v_ref, slice(None), kb_ref, vt_ref, n_blk, tk)

    cols = HEADS_PER_SLAB * tq
    qpos = jnp.concatenate([qi * tq + _iota((1, tq), 1)] * HEADS_PER_SLAB, axis=1)
    blk_id = _iota((blk_rows, cols), 0)
    k_hi, k_lo = _split_bf16(kmean_ref[0:blk_rows, :])
    qf = jnp.concatenate(_split_heads_t(q_ref[0].T, 1.0), axis=1)
    qs = (qf * ATTN_SCALE).astype(BF16)
    q_hi, q_lo = _split_bf16(qf)
    gate = _dot(k_hi, q_hi) + _dot(k_lo, q_hi) + _dot(k_hi, q_lo)
    fully_past = (blk_id < qi) & (blk_id < n_blk)
    gate = jnp.where(fully_past, gate, -jnp.inf)
    sel = jnp.where(fully_past & (_top_rank(gate, n_blk, 0) < MOBA_TOPK), 1.0, 0.0)

    def scores(kb, limit):
        bias = jnp.where((kb * tk + _iota((tk, 1), 0)) <= limit, 0.0, NEG_BIG)
        return _dot(kb_ref[pl.ds(pl.multiple_of(kb * tk, tk), tk), :], qs) + bias

    def past_limit(kb, valid):
        picked = jnp.sum(jnp.where(blk_id == kb, sel, 0.0), axis=0, keepdims=True) > 0.5
        return jnp.where(picked, jnp.where(valid, t_total, -1), -1)

    def body(i, st):
        ka = qi - 2 * i
        kb = ka - 1
        kb_safe = jnp.maximum(kb, 0)
        s_a = scores(ka, jnp.where(ka == qi, qpos, past_limit(ka, True)))
        s_b = scores(kb_safe, past_limit(kb_safe, kb >= 0))
        return _osm_t_update2(st, s_a, s_b, lambda p: _weigh_heads_t(vt_ref, ka, p, tq),
                              lambda p: _weigh_heads_t(vt_ref, kb_safe, p, tq))

    out = _osm_t_final(lax.fori_loop(0, (qi + 2) // 2, body, _osm_t_init(HEAD_DIM, cols)))
    o_ref[0] = _unstack_heads_t(out, tq).T


def moba_prefill(q, kv):
    b, t, _ = q.shape
    n_slab = (N_HEADS - H_SB) // HEADS_PER_SLAB
    q_off = H_SB // HEADS_PER_SLAB
    v_off = N_HEADS // HEADS_PER_SLAB
    assert ATTN_TQ == MOBA_BLOCK and t % MOBA_BLOCK == 0 and t // MOBA_BLOCK <= LANES
    return pl.pallas_call(
        functools.partial(_moba_prefill_kernel, n_blk=t // MOBA_BLOCK),
        grid=(b, n_slab, t // ATTN_TQ),
        in_specs=[pl.BlockSpec((1, ATTN_TQ, LANES), lambda bi, s, qi: (bi, qi, q_off + s)),
                  pl.BlockSpec((1, t, LANES), lambda bi, s, qi: (bi, 0, q_off + s)),
                  pl.BlockSpec((1, t, LANES), lambda bi, s, qi: (bi, 0, v_off + q_off + s))],
        out_specs=pl.BlockSpec((1, ATTN_TQ, LANES), lambda bi, s, qi: (bi, qi, s)),
        out_shape=jax.ShapeDtypeStruct((b, t, (N_HEADS - H_SB) * HEAD_DIM), F32),
        scratch_shapes=[pltpu.VMEM((LANES, LANES), F32), pltpu.VMEM((t, LANES), BF16),
                        pltpu.VMEM((t // ATTN_TK, LANES, ATTN_TK), BF16)],
        compiler_params=_params("parallel", "parallel", "arbitrary"),
        name="moba_prefill",
    )(q, kv, kv)


CHUNKS_PER_PAGE = PAGE_SIZE // CMP_STRIDE
CMP_HALF = CMP_STRIDE * HEAD_DIM
PAGES_PER_STEP = 8


def _page_specs(block, index_of_page):
    return [pl.BlockSpec(block, functools.partial(index_of_page, g=g)) for g in range(PAGES_PER_STEP)]


def _compress_kernel(pt_ref, *refs, n_steps):
    page_refs = refs[:PAGES_PER_STEP]
    pe_ref, w1_ref, w2_ref, kc_ref, vc_ref, x_ref = refs[PAGES_PER_STEP:]
    p = pl.program_id(1)
    lane = _iota((1, LANES), 1)
    low = lane < HEAD_DIM
    r = _iota((PAGE_SIZE, PAGE_SIZE), 0)
    token = _iota((PAGE_SIZE, PAGE_SIZE), 1)
    perm = jnp.where(token == CMP_STRIDE * (r % CHUNKS_PER_PAGE) + r // CHUNKS_PER_PAGE, 1.0, 0.0).astype(BF16)
    for g, page_ref in enumerate(page_refs):
        rows = pl.ds(pl.multiple_of((p * PAGES_PER_STEP + g) * CHUNKS_PER_PAGE, CHUNKS_PER_PAGE), CHUNKS_PER_PAGE)
        for kv in range(2):
            for s in range(NSA_HKV // HEADS_PER_SLAB):
                two_heads = page_ref[0, kv, HEADS_PER_SLAB * s:HEADS_PER_SLAB * (s + 1)].reshape(LANES, PAGE_SIZE)
                hi, lo = _split_bf16(two_heads)
                slab = _dot_nt(perm, hi) + _dot_nt(perm, lo)
                for pp in range(CMP_STRIDE // 2):
                    even = slab[2 * pp * CHUNKS_PER_PAGE:(2 * pp + 1) * CHUNKS_PER_PAGE]
                    odd = slab[(2 * pp + 1) * CHUNKS_PER_PAGE:(2 * pp + 2) * CHUNKS_PER_PAGE]
                    head0 = jnp.where(low, even, pltpu.roll(odd, HEAD_DIM, 1))
                    head1 = jnp.where(low, pltpu.roll(even, HEAD_DIM, 1), odd)
                    x_ref[kv * NSA_HKV + 2 * s, rows, pp * LANES:(pp + 1) * LANES] = head0
                    x_ref[kv * NSA_HKV + 2 * s + 1, rows, pp * LANES:(pp + 1) * LANES] = head1

    @pl.when(p == n_steps - 1)
    def _():
        n_chunk = n_steps * PAGES_PER_STEP * CHUNKS_PER_PAGE
        for kv, out_ref in ((0, kc_ref), (1, vc_ref)):
            pe_a = pe_ref[kv, 0:1, :]
            pe_b = pe_ref[kv, 1:2, :]
            w1a = w1_ref[kv, 0:CMP_HALF, :]
            w1b = w1_ref[kv, CMP_HALF:2 * CMP_HALF, :]
            w2 = w2_ref[kv]
            for h in range(NSA_HKV):
                x = x_ref[kv * NSA_HKV + h]
                first = _dot((x + pe_a).astype(BF16), w1a)
                second = _dot((x + pe_b).astype(BF16), w1b)
                hid = first + pltpu.roll(second, n_chunk - 1, 0)
                hid = hid / (1.0 + jnp.exp(-hid))
                out_ref[0, :, h * HEAD_DIM:(h + 1) * HEAD_DIM] = _dot(hid.astype(BF16), w2)


def compress_pages(pages, page_table, pe, w1_bf16, w2_bf16):
    b, n_pages = page_table.shape
    assert n_pages % PAGES_PER_STEP == 0
    n_chunk = n_pages * CHUNKS_PER_PAGE
    width = NSA_HKV * HEAD_DIM
    grid_spec = pltpu.PrefetchScalarGridSpec(
        num_scalar_prefetch=1,
        grid=(b, n_pages // PAGES_PER_STEP),
        in_specs=_page_specs((1, 2, NSA_HKV, HEAD_DIM, PAGE_SIZE),
                             lambda bi, p, pt, g: (pt[bi, p * PAGES_PER_STEP + g], 0, 0, 0, 0)) + [
                  pl.BlockSpec(pe.shape, lambda bi, p, pt: (0, 0, 0)),
                  pl.BlockSpec(w1_bf16.shape, lambda bi, p, pt: (0, 0, 0)),
                  pl.BlockSpec(w2_bf16.shape, lambda bi, p, pt: (0, 0, 0))],
        out_specs=[pl.BlockSpec((1, n_chunk, width), lambda bi, p, pt: (bi, 0, 0)),
                   pl.BlockSpec((1, n_chunk, width), lambda bi, p, pt: (bi, 0, 0))],
        scratch_shapes=[pltpu.VMEM((2 * NSA_HKV, n_chunk, CMP_HALF), F32)],
    )
    return pl.pallas_call(
        functools.partial(_compress_kernel, n_steps=n_pages // PAGES_PER_STEP),
        grid_spec=grid_spec,
        out_shape=[jax.ShapeDtypeStruct((b, n_chunk, width), F32)] * 2,
        compiler_params=_params("parallel", "arbitrary"),
        name="nsa_compress",
    )(page_table, *([pages] * PAGES_PER_STEP), pe, w1_bf16, w2_bf16)


NSA_TQ = 128
SLC_TK = 256
WIN_TK = 128


def _cover_matrix(n_cmp_rows, n_cols, cmp_axis):
    i = _iota((n_cmp_rows, n_cols) if cmp_axis == 0 else (n_cols, n_cmp_rows), cmp_axis)
    j = _iota((n_cmp_rows, n_cols) if cmp_axis == 0 else (n_cols, n_cmp_rows), 1 - cmp_axis)
    ratio = SLC_BLOCK // CMP_STRIDE
    reach = CMP_BLOCK // CMP_STRIDE - 1
    return jnp.where((i >= ratio * j - reach) & (i <= ratio * j + ratio - 1), 1.0, 0.0).astype(BF16)


def _slc_score(imp, cur, n_slc, axis):
    j = _iota(imp.shape, axis)
    forced = (j == 0) | (j == cur) | (j == cur - 1)
    score = jnp.where(forced, FORCE_SCORE, jnp.where(j > cur, -FORCE_SCORE, imp))
    return jnp.where(j < n_slc, score, -jnp.inf)


def _nsa_prefill_kernel(q_ref, slc_ref, win_ref, kc_ref, vc_ref, g_ref, o_ref, ks_ref, vst_ref, kw_ref, vwt_ref, *, t):
    qi = pl.program_id(1)
    tq = NSA_TQ
    n_slc = t // SLC_BLOCK
    top = min(SLC_TOPK, n_slc)
    n_cmp_rows = t // CMP_STRIDE
    n_slab = NSA_HKV // HEADS_PER_SLAB

    @pl.when(qi == 0)
    def _():
        for s in range(n_slab):
            v_cols = slice(KV_WIDTH + s * LANES, KV_WIDTH + (s + 1) * LANES)
            k_cols = slice(s * LANES, (s + 1) * LANES)
            _stage_keys_values(slc_ref, k_cols, slc_ref, v_cols, ks_ref.at[s], vst_ref.at[s], t // SLC_TK, SLC_TK)
            _stage_keys_values(win_ref, k_cols, win_ref, v_cols, kw_ref.at[s], vwt_ref.at[s], t // WIN_TK, WIN_TK)

    qt = q_ref[0].T * ATTN_SCALE
    gates_t = g_ref[0].T
    qpos = qi * tq + _iota((1, tq), 1)
    qpos4 = jnp.concatenate([qpos] * NSA_GROUP, axis=1)
    cover_t = _cover_matrix(n_cmp_rows, LANES, 1)
    zeros_head = jnp.zeros((HEAD_DIM, tq), F32)
    outs = []
    for hk in range(NSA_HKV):
        half, slab = hk % HEADS_PER_SLAB, hk // HEADS_PER_SLAB
        cols = slice(slab * LANES, (slab + 1) * LANES)
        head_rows = slice(half * HEAD_DIM, (half + 1) * HEAD_DIM)
        parts = []
        for g in range(NSA_GROUP):
            h = hk * NSA_GROUP + g
            qg = qt[h * HEAD_DIM:(h + 1) * HEAD_DIM]
            parts.append(jnp.concatenate([qg, zeros_head] if half == 0 else [zeros_head, qg], axis=0))
        qs = jnp.concatenate(parts, axis=1).astype(BF16)

        s_c = _dot(kc_ref[0, :, cols].astype(BF16), qs)
        ok_c = (CMP_STRIDE * _iota((n_cmp_rows, 1), 0) + CMP_BLOCK - 1) <= qpos4
        sm = jnp.where(ok_c, s_c, NEG_BIG)
        pc = jnp.where(ok_c, jnp.exp(sm - jnp.max(sm, axis=0, keepdims=True)), 0.0)
        pc = pc / jnp.maximum(jnp.sum(pc, axis=0, keepdims=True), 1e-30)
        o_c = _dot(vc_ref[0, :, cols].T[head_rows].astype(BF16), pc.astype(BF16))

        psum = pc[:, 0:tq] + pc[:, tq:2 * tq] + pc[:, 2 * tq:3 * tq] + pc[:, 3 * tq:4 * tq]
        p_hi, p_lo = _split_bf16(psum)
        imp_t = (_dot(cover_t, p_hi) + _dot(cover_t, p_lo))[0:n_slc]
        score_t = _slc_score(imp_t, qpos // SLC_BLOCK, n_slc, 0)
        sel_t = jnp.where(_top_rank(score_t, n_slc, 0) < top, 1.0, 0.0)
        sel = jnp.concatenate([sel_t, jnp.zeros((LANES - n_slc, tq), F32)], axis=0).astype(BF16)

        n_slc_tiles = (qi * tq + tq + SLC_TK - 1) // SLC_TK

        def slc_scores(kt, valid, qs=qs, sel=sel, slab=slab):
            blk_of_key = kt * (SLC_TK // SLC_BLOCK) + _iota((SLC_TK, LANES), 0) // SLC_BLOCK
            expand = jnp.where(_iota((SLC_TK, LANES), 1) == blk_of_key, 1.0, 0.0).astype(BF16)
            kpos = kt * SLC_TK + _iota((SLC_TK, 1), 0)
            ok = (kpos <= qpos) & (_dot(expand, sel) > 0.5)
            bias = jnp.where(ok, jnp.where(valid, 0.0, NEG_BIG), NEG_BIG)
            s = _dot(ks_ref[slab, pl.ds(pl.multiple_of(kt * SLC_TK, SLC_TK), SLC_TK), :], qs)
            return s + jnp.concatenate([bias] * NSA_GROUP, axis=1)

        def slc_body(i, st, slc_scores=slc_scores, slab=slab, head_rows=head_rows):
            ka = n_slc_tiles - 1 - 2 * i
            kb = jnp.maximum(ka - 1, 0)
            return _osm_t_update2(st, slc_scores(ka, True), slc_scores(kb, ka >= 1),
                                  lambda p: _dot(vst_ref[slab, ka, head_rows, :], p),
                                  lambda p: _dot(vst_ref[slab, kb, head_rows, :], p))

        o_s = _osm_t_final(lax.fori_loop(0, (n_slc_tiles + 1) // 2, slc_body, _osm_t_init(HEAD_DIM, NSA_GROUP * tq)))

        first_tile = jnp.maximum(qi * (tq // WIN_TK) - WINDOW // WIN_TK, 0)
        last_tile = qi * (tq // WIN_TK) + tq // WIN_TK - 1

        def win_scores(kt, valid, qs=qs, slab=slab):
            kpos = kt * WIN_TK + _iota((WIN_TK, 1), 0)
            ok = (kpos <= qpos) & (qpos - kpos < WINDOW)
            bias = jnp.where(ok, jnp.where(valid, 0.0, NEG_BIG), NEG_BIG)
            s = _dot(kw_ref[slab, pl.ds(pl.multiple_of(kt * WIN_TK, WIN_TK), WIN_TK), :], qs)
            return s + jnp.concatenate([bias] * NSA_GROUP, axis=1)

        def win_body(i, st, win_scores=win_scores, slab=slab, head_rows=head_rows):
            ka = last_tile - 2 * i
            kb = jnp.maximum(ka - 1, first_tile)
            return _osm_t_update2(st, win_scores(ka, True), win_scores(kb, ka - 1 >= first_tile),
                                  lambda p: _dot(vwt_ref[slab, ka, head_rows, :], p),
                                  lambda p: _dot(vwt_ref[slab, kb, head_rows, :], p))

        o_w = _osm_t_final(lax.fori_loop(0, (last_tile - first_tile + 2) // 2, win_body,
                                         _osm_t_init(HEAD_DIM, NSA_GROUP * tq)))

        for g in range(NSA_GROUP):
            cs = slice(g * tq, (g + 1) * tq)
            gc = hk * LANES + g * 3
            outs.append(gates_t[gc:gc + 1] * o_c[:, cs] + gates_t[gc + 1:gc + 2] * o_s[:, cs]
                        + gates_t[gc + 2:gc + 3] * o_w[:, cs])
    o_ref[0] = jnp.concatenate(outs, axis=0).T


def nsa_prefill(q, kv, win, kc, vc, gates):
    b, t, _ = q.shape
    width = NSA_HKV * HEAD_DIM
    n_chunk = t // CMP_STRIDE
    n_slc = t // SLC_BLOCK
    assert t % SLC_TK == 0 and n_chunk % LANES == 0 and n_slc <= LANES and n_slc % 8 == 0 and NSA_TQ == WIN_TK
    n_slab = NSA_HKV // HEADS_PER_SLAB
    return pl.pallas_call(
        functools.partial(_nsa_prefill_kernel, t=t),
        grid=(b, t // NSA_TQ),
        in_specs=[pl.BlockSpec((1, NSA_TQ, N_HEADS * HEAD_DIM), lambda bi, qi: (bi, qi, 0)),
                  pl.BlockSpec((1, t, 2 * width), lambda bi, qi: (bi, 0, 1)),
                  pl.BlockSpec((1, t, 2 * width), lambda bi, qi: (bi, 0, 0)),
                  pl.BlockSpec((1, n_chunk, width), lambda bi, qi: (bi, 0, 0)),
                  pl.BlockSpec((1, n_chunk, width), lambda bi, qi: (bi, 0, 0)),
                  pl.BlockSpec((1, NSA_TQ, NSA_HKV * LANES), lambda bi, qi: (bi, qi, 0))],
        out_specs=pl.BlockSpec((1, NSA_TQ, N_HEADS * HEAD_DIM), lambda bi, qi: (bi, qi, 0)),
        out_shape=jax.ShapeDtypeStruct((b, t, N_HEADS * HEAD_DIM), F32),
        scratch_shapes=[pltpu.VMEM((n_slab, t, LANES), BF16), pltpu.VMEM((n_slab, t // SLC_TK, LANES, SLC_TK), BF16),
                        pltpu.VMEM((n_slab, t, LANES), BF16), pltpu.VMEM((n_slab, t // WIN_TK, LANES, WIN_TK), BF16)],
        compiler_params=_params("parallel", "arbitrary"),
        name="nsa_prefill",
    )(q, kv, win, kc, vc, gates)


SB_WIDTH = H_SB * HEAD_DIM
KV0_V_OFF = N_HEADS * HEAD_DIM
PAGES_PER_MOBA_BLOCK = MOBA_BLOCK // PAGE_SIZE


def _load_new_rows(dst_ref, new_ref, col, width, dec_seq):
    dst_ref[...] = jnp.zeros_like(dst_ref)
    dst_ref[0:dec_seq, :] = new_ref[0, :, col:col + width]


def _heads_by_dim(page_tile):
    return page_tile.reshape(page_tile.shape[0] * HEAD_DIM, page_tile.shape[2])


def _fold_heads(res, dec_seq, n_heads):
    lane_head = _iota((1, n_heads * HEAD_DIM), 1) // HEAD_DIM
    out = jnp.zeros((dec_seq, n_heads * HEAD_DIM), F32)
    for h in range(n_heads):
        out = out + jnp.where(lane_head == h, res[h * dec_seq:(h + 1) * dec_seq, :], 0.0)
    return out


def _sb_decode_kernel(pt_ref, qbd_ref, new_ref, *refs, n_steps, dec_seq):
    k_refs = refs[:PAGES_PER_STEP]
    v_refs = refs[PAGES_PER_STEP:2 * PAGES_PER_STEP]
    o_ref, kmean_ref, c_ref, acc_ref, kpad_ref, vpad_ref = refs[2 * PAGES_PER_STEP:]
    p = pl.program_id(1)
    qbd = (qbd_ref[0] * ATTN_SCALE).astype(BF16)
    rows = qbd.shape[0]
    later = _later_matrix(PAGE_SIZE)
    n_blk = n_steps * PAGES_PER_STEP // PAGES_PER_MOBA_BLOCK

    @pl.when(p == 0)
    def _():
        kmean_ref[...] = jnp.zeros_like(kmean_ref)
        _load_new_rows(kpad_ref, new_ref, 0, SB_WIDTH, dec_seq)
        _load_new_rows(vpad_ref, new_ref, KV0_V_OFF, SB_WIDTH, dec_seq)
        valid = _iota((1, PAGE_SIZE), 1) < lax.rem(_iota((rows, 1), 0), dec_seq)
        ls, lk = _log_sigmoid_pair(_dot_nt(qbd, kpad_ref[...].astype(BF16)))
        lk = jnp.where(valid, lk, 0.0)
        w = jnp.where(valid, jnp.exp(ls + _dot_exact_rhs(lk, later)), 0.0)
        acc_ref[...] = _dot(w.astype(BF16), vpad_ref[...].astype(BF16))
        c_ref[...] = jnp.sum(lk, axis=1, keepdims=True)

    zs = [_dot(qbd, _heads_by_dim(k_ref[0, 0, 0:H_SB]).astype(BF16)) for k_ref in k_refs]
    c = c_ref[...]
    acc = acc_ref[...]
    for z, v_ref in zip(zs, v_refs):
        ls, lk = _log_sigmoid_pair(z)
        w = jnp.exp(ls + _dot_exact_rhs(lk, later) + c)
        acc = acc + _dot_nt(w.astype(BF16), _heads_by_dim(v_ref[0, 0]).astype(BF16))
        c = c + jnp.sum(lk, axis=1, keepdims=True)
    c_ref[...] = c
    acc_ref[...] = acc

    lane = _iota((1, LANES), 1)
    kmean = kmean_ref[0]
    for j in range(PAGES_PER_STEP // PAGES_PER_MOBA_BLOCK):
        total = jnp.zeros(((N_HEADS - H_SB) * HEAD_DIM, 1), F32)
        for k_ref in k_refs[j * PAGES_PER_MOBA_BLOCK:(j + 1) * PAGES_PER_MOBA_BLOCK]:
            total = total + jnp.sum(_heads_by_dim(k_ref[0, 0, H_SB:N_HEADS]), axis=1, keepdims=True)
        blk = n_blk - 1 - (p * (PAGES_PER_STEP // PAGES_PER_MOBA_BLOCK) + j)
        kmean = jnp.where(lane == blk, total * (1.0 / MOBA_BLOCK), kmean)
    kmean_ref[0] = kmean

    @pl.when(p == n_steps - 1)
    def _():
        o_ref[0] = _fold_heads(acc_ref[...], dec_seq, H_SB)


def sb_decode(qbd, new_kv, pool, page_table):
    b, n_pages = page_table.shape
    dec_seq = new_kv.shape[1]
    rows = qbd.shape[1]
    assert n_pages % PAGES_PER_STEP == 0 and PAGES_PER_STEP % PAGES_PER_MOBA_BLOCK == 0
    assert n_pages // PAGES_PER_MOBA_BLOCK <= LANES

    def page_of(bi, p, pt, g):
        return pt[bi, n_pages - 1 - (p * PAGES_PER_STEP + g)]

    grid_spec = pltpu.PrefetchScalarGridSpec(
        num_scalar_prefetch=1,
        grid=(b, n_pages // PAGES_PER_STEP),
        in_specs=[pl.BlockSpec((1, rows, SB_WIDTH), lambda bi, p, pt: (bi, 0, 0)),
                  pl.BlockSpec((1, dec_seq, new_kv.shape[2]), lambda bi, p, pt: (bi, 0, 0))]
        + _page_specs((1, 1, N_HEADS, HEAD_DIM, PAGE_SIZE), lambda bi, p, pt, g: (page_of(bi, p, pt, g), 0, 0, 0, 0))
        + _page_specs((1, 1, H_SB, HEAD_DIM, PAGE_SIZE), lambda bi, p, pt, g: (page_of(bi, p, pt, g), 1, 0, 0, 0)),
        out_specs=[pl.BlockSpec((1, dec_seq, SB_WIDTH), lambda bi, p, pt: (bi, 0, 0)),
                   pl.BlockSpec((1, SB_WIDTH, LANES), lambda bi, p, pt: (bi, 0, 0))],
        scratch_shapes=[pltpu.VMEM((rows, 1), F32), pltpu.VMEM((rows, SB_WIDTH), F32),
                        pltpu.VMEM((PAGE_SIZE, SB_WIDTH), F32), pltpu.VMEM((PAGE_SIZE, SB_WIDTH), F32)],
    )
    return pl.pallas_call(
        functools.partial(_sb_decode_kernel, n_steps=n_pages // PAGES_PER_STEP, dec_seq=dec_seq),
        grid_spec=grid_spec,
        out_shape=[jax.ShapeDtypeStruct((b, dec_seq, SB_WIDTH), F32),
                   jax.ShapeDtypeStruct((b, SB_WIDTH, LANES), F32)],
        compiler_params=_params("parallel", "arbitrary"),
        name="sb_decode",
    )(page_table, qbd, new_kv, *([pool] * (2 * PAGES_PER_STEP)))


def _moba_decode_kernel(pt_ref, qbd_ref, kmean_ref, new_ref, *refs, n_steps, dec_seq, past_len):
    kv_refs = refs[:PAGES_PER_STEP]
    o_ref, sel_ref, m_ref, l_ref, acc_ref, kpad_ref, vpad_ref = refs[PAGES_PER_STEP:]
    p = pl.program_id(1)
    n_blk = n_steps * PAGES_PER_STEP // PAGES_PER_MOBA_BLOCK
    qf = qbd_ref[0]
    qb = (qf * ATTN_SCALE).astype(BF16)
    rows = qf.shape[0]
    t_of_row = lax.rem(_iota((rows, 1), 0), dec_seq)
    blk_id = _iota((1, LANES), 1)

    @pl.when(p == 0)
    def _():
        q_hi, q_lo = _split_bf16(qf)
        k_hi, k_lo = _split_bf16(kmean_ref[0])
        gate = _dot(q_hi, k_hi) + _dot(q_hi, k_lo) + _dot(q_lo, k_hi)
        fully_past = (blk_id < (past_len + t_of_row) // MOBA_BLOCK) & (blk_id < n_blk)
        gate = jnp.where(fully_past, gate, -jnp.inf)
        sel_ref[...] = jnp.where(fully_past & (_top_rank(gate, n_blk, 1) < MOBA_TOPK), 1.0, 0.0)
        _osm_reset(m_ref, l_ref, acc_ref)

    sel = sel_ref[...]
    s_parts, mask_parts = [], []
    for g, kv_ref in enumerate(kv_refs):
        s_parts.append(_dot(qb, _heads_by_dim(kv_ref[0, 0]).astype(BF16)))
        blk = (p * PAGES_PER_STEP + g) // PAGES_PER_MOBA_BLOCK
        picked = jnp.sum(jnp.where(blk_id == blk, sel, 0.0), axis=1, keepdims=True) > 0.5
        mask_parts.append(jnp.broadcast_to(picked, (rows, PAGE_SIZE)))

    def weigh_pages(pb):
        out = jnp.zeros((rows, SB_WIDTH), F32)
        for g, kv_ref in enumerate(kv_refs):
            out = out + _dot_nt(pb[:, g * PAGE_SIZE:(g + 1) * PAGE_SIZE], _heads_by_dim(kv_ref[0, 1]).astype(BF16))
        return out

    st = _osm_update((m_ref[...], l_ref[...], acc_ref[...]), jnp.concatenate(s_parts, axis=1),
                     jnp.concatenate(mask_parts, axis=1), weigh_pages)
    m_ref[...], l_ref[...], acc_ref[...] = st

    @pl.when(p == n_steps - 1)
    def _():
        _load_new_rows(kpad_ref, new_ref, SB_WIDTH, SB_WIDTH, dec_seq)
        _load_new_rows(vpad_ref, new_ref, KV0_V_OFF + SB_WIDTH, SB_WIDTH, dec_seq)
        key = _iota((1, PAGE_SIZE), 1)
        vb = vpad_ref[...].astype(BF16)
        fin = _osm_update((m_ref[...], l_ref[...], acc_ref[...]), _dot_nt(qb, kpad_ref[...].astype(BF16)),
                          (key <= t_of_row) & (key < dec_seq), lambda pb: _dot(pb, vb))
        o_ref[0] = _fold_heads(_osm_final(fin), dec_seq, N_HEADS - H_SB)


def moba_decode(qbd, kmean, new_kv, pool, page_table, past_len):
    b, n_pages = page_table.shape
    dec_seq = new_kv.shape[1]
    rows = qbd.shape[1]
    assert past_len % MOBA_BLOCK == 0 and n_pages * PAGE_SIZE == past_len and dec_seq <= PAGE_SIZE
    assert n_pages % PAGES_PER_STEP == 0
    grid_spec = pltpu.PrefetchScalarGridSpec(
        num_scalar_prefetch=1,
        grid=(b, n_pages // PAGES_PER_STEP),
        in_specs=[pl.BlockSpec((1, rows, SB_WIDTH), lambda bi, p, pt: (bi, 0, 0)),
                  pl.BlockSpec((1, SB_WIDTH, LANES), lambda bi, p, pt: (bi, 0, 0)),
                  pl.BlockSpec((1, dec_seq, new_kv.shape[2]), lambda bi, p, pt: (bi, 0, 0))]
        + _page_specs((1, 2, N_HEADS - H_SB, HEAD_DIM, PAGE_SIZE),
                      lambda bi, p, pt, g: (pt[bi, p * PAGES_PER_STEP + g], 0, 1, 0, 0)),
        out_specs=pl.BlockSpec((1, dec_seq, SB_WIDTH), lambda bi, p, pt: (bi, 0, 0)),
        scratch_shapes=[pltpu.VMEM((rows, LANES), F32), pltpu.VMEM((rows, 1), F32), pltpu.VMEM((rows, 1), F32),
                        pltpu.VMEM((rows, SB_WIDTH), F32),
                        pltpu.VMEM((PAGE_SIZE, SB_WIDTH), F32), pltpu.VMEM((PAGE_SIZE, SB_WIDTH), F32)],
    )
    return pl.pallas_call(
        functools.partial(_moba_decode_kernel, n_steps=n_pages // PAGES_PER_STEP, dec_seq=dec_seq, past_len=past_len),
        grid_spec=grid_spec,
        out_shape=jax.ShapeDtypeStruct((b, dec_seq, SB_WIDTH), F32),
        compiler_params=_params("parallel", "arbitrary"),
        name="moba_decode",
    )(page_table, qbd, kmean, new_kv, *([pool] * PAGES_PER_STEP))


KV_WIDTH = NSA_HKV * HEAD_DIM


def _nsa_decode_kernel(pt_ref, qbd_ref, kc_ref, vc_ref, state_ref, new_kv_ref, new_win_ref, g_ref, *refs,
                       n_steps, dec_seq, past_len):
    page_refs = refs[:PAGES_PER_STEP]
    o_ref, sel_ref, oc_ref, ow_ref, m_ref, l_ref, acc_ref, kpad_ref, vpad_ref = refs[PAGES_PER_STEP:]
    p = pl.program_id(1)
    qb = (qbd_ref[0] * ATTN_SCALE).astype(BF16)
    rows = qb.shape[0]
    grp_rows = NSA_HKV * dec_seq
    n_cmp_rows = kc_ref.shape[1]
    n_slc = past_len // SLC_BLOCK + 1
    slc_lanes = sel_ref.shape[1]
    t_of_row = lax.rem(_iota((rows, 1), 0), dec_seq)
    qpos = past_len + t_of_row
    key = _iota((1, PAGE_SIZE), 1)

    @pl.when(p == 0)
    def _():
        s_c = _dot_nt(qb, kc_ref[0].astype(BF16))
        ok_c = (CMP_STRIDE * _iota((1, n_cmp_rows), 1) + CMP_BLOCK - 1) <= qpos
        sm = jnp.where(ok_c, s_c, NEG_BIG)
        pc = jnp.where(ok_c, jnp.exp(sm - jnp.max(sm, axis=1, keepdims=True)), 0.0)
        pc = pc / jnp.maximum(jnp.sum(pc, axis=1, keepdims=True), 1e-30)
        oc_ref[...] = _dot(pc.astype(BF16), vc_ref[0].astype(BF16))
        psum = pc[0:grp_rows]
        for g in range(1, NSA_GROUP):
            psum = psum + pc[g * grp_rows:(g + 1) * grp_rows]
        imp = _dot_exact_rhs(psum, _cover_matrix(n_cmp_rows, slc_lanes, 0))
        score = _slc_score(imp, qpos[0:grp_rows] // SLC_BLOCK, n_slc, 1)
        sel = jnp.where(_top_rank(score, n_slc, 1) < min(SLC_TOPK, n_slc), 1.0, 0.0)
        sel_ref[...] = jnp.concatenate([sel] * NSA_GROUP, axis=0)
        vwt = _heads_by_dim(state_ref[0, 1]).astype(BF16)
        st = _osm_update(_osm_init(rows, KV_WIDTH), _dot(qb, _heads_by_dim(state_ref[0, 0]).astype(BF16)),
                         _iota((1, WINDOW), 1) > t_of_row, lambda pb: _dot_nt(pb, vwt))
        _load_new_rows(kpad_ref, new_win_ref, 0, KV_WIDTH, dec_seq)
        _load_new_rows(vpad_ref, new_win_ref, KV_WIDTH, KV_WIDTH, dec_seq)
        vb = vpad_ref[...].astype(BF16)
        st = _osm_update(st, _dot_nt(qb, kpad_ref[...].astype(BF16)), (key <= t_of_row) & (key < dec_seq),
                         lambda pb: _dot(pb, vb))
        ow_ref[...] = _osm_final(st)
        _osm_reset(m_ref, l_ref, acc_ref)

    keys_per_step = PAGES_PER_STEP * PAGE_SIZE
    blk_of_key = p * (keys_per_step // SLC_BLOCK) + _iota((slc_lanes, keys_per_step), 1) // SLC_BLOCK
    expand = jnp.where(_iota((slc_lanes, keys_per_step), 0) == blk_of_key, 1.0, 0.0).astype(BF16)
    s_all = jnp.concatenate([_dot(qb, _heads_by_dim(ref[0, 0]).astype(BF16)) for ref in page_refs], axis=1)

    def weigh_pages(pb):
        out = jnp.zeros((rows, KV_WIDTH), F32)
        for g, ref in enumerate(page_refs):
            out = out + _dot_nt(pb[:, g * PAGE_SIZE:(g + 1) * PAGE_SIZE], _heads_by_dim(ref[0, 1]).astype(BF16))
        return out

    st = _osm_update((m_ref[...], l_ref[...], acc_ref[...]), s_all,
                     _dot(sel_ref[...].astype(BF16), expand) > 0.5, weigh_pages)
    m_ref[...], l_ref[...], acc_ref[...] = st

    @pl.when(p == n_steps - 1)
    def _():
        _load_new_rows(kpad_ref, new_kv_ref, 2 * KV_WIDTH, KV_WIDTH, dec_seq)
        _load_new_rows(vpad_ref, new_kv_ref, 3 * KV_WIDTH, KV_WIDTH, dec_seq)
        cur = past_len // SLC_BLOCK
        picked = sel_ref[:, cur:cur + 1] > 0.5
        vb = vpad_ref[...].astype(BF16)
        fin = _osm_update((m_ref[...], l_ref[...], acc_ref[...]), _dot_nt(qb, kpad_ref[...].astype(BF16)),
                          picked & (key <= t_of_row) & (key < dec_seq), lambda pb: _dot(pb, vb))
        o_s = _osm_final(fin)
        o_c = oc_ref[...]
        o_w = ow_ref[...]
        gates = g_ref[0]
        for hk in range(NSA_HKV):
            for g in range(NSA_GROUP):
                rs = slice(g * grp_rows + hk * dec_seq, g * grp_rows + (hk + 1) * dec_seq)
                hs = slice(hk * HEAD_DIM, (hk + 1) * HEAD_DIM)
                gc = hk * LANES + g * 3
                h = hk * NSA_GROUP + g
                o_ref[0, :, h * HEAD_DIM:(h + 1) * HEAD_DIM] = (
                    gates[:, gc:gc + 1] * o_c[rs, hs] + gates[:, gc + 1:gc + 2] * o_s[rs, hs]
                    + gates[:, gc + 2:gc + 3] * o_w[rs, hs])


def nsa_decode(qbd, kc, vc, state_win, new_kv, new_win, gates, pool, page_table, past_len):
    b, n_pages = page_table.shape
    dec_seq = new_kv.shape[1]
    rows = qbd.shape[1]
    n_chunk = kc.shape[1]
    n_slc = past_len // SLC_BLOCK + 1
    slc_lanes = -(-n_slc // LANES) * LANES
    assert past_len % SLC_BLOCK == 0 and past_len >= WINDOW and state_win.shape[-1] == WINDOW and dec_seq <= SLC_BLOCK
    assert n_pages % PAGES_PER_STEP == 0
    grid_spec = pltpu.PrefetchScalarGridSpec(
        num_scalar_prefetch=1,
        grid=(b, n_pages // PAGES_PER_STEP),
        in_specs=[pl.BlockSpec((1, rows, KV_WIDTH), lambda bi, p, pt: (bi, 0, 0)),
                  pl.BlockSpec((1, n_chunk, KV_WIDTH), lambda bi, p, pt: (bi, 0, 0)),
                  pl.BlockSpec((1, n_chunk, KV_WIDTH), lambda bi, p, pt: (bi, 0, 0)),
                  pl.BlockSpec((1, 2, NSA_HKV, HEAD_DIM, WINDOW), lambda bi, p, pt: (bi, 0, 0, 0, 0)),
                  pl.BlockSpec((1, dec_seq, 4 * KV_WIDTH), lambda bi, p, pt: (bi, 0, 0)),
                  pl.BlockSpec((1, dec_seq, 2 * KV_WIDTH), lambda bi, p, pt: (bi, 0, 0)),
                  pl.BlockSpec((1, dec_seq, NSA_HKV * LANES), lambda bi, p, pt: (bi, 0, 0))]
        + _page_specs((1, 2, NSA_HKV, HEAD_DIM, PAGE_SIZE),
                      lambda bi, p, pt, g: (pt[bi, p * PAGES_PER_STEP + g], 1, 0, 0, 0)),
        out_specs=pl.BlockSpec((1, dec_seq, N_HEADS * HEAD_DIM), lambda bi, p, pt: (bi, 0, 0)),
        scratch_shapes=[pltpu.VMEM((rows, slc_lanes), F32), pltpu.VMEM((rows, KV_WIDTH), F32),
                        pltpu.VMEM((rows, KV_WIDTH), F32), pltpu.VMEM((rows, 1), F32), pltpu.VMEM((rows, 1), F32),
                        pltpu.VMEM((rows, KV_WIDTH), F32),
                        pltpu.VMEM((PAGE_SIZE, KV_WIDTH), F32), pltpu.VMEM((PAGE_SIZE, KV_WIDTH), F32)],
    )
    return pl.pallas_call(
        functools.partial(_nsa_decode_kernel, n_steps=n_pages // PAGES_PER_STEP, dec_seq=dec_seq, past_len=past_len),
        grid_spec=grid_spec,
        out_shape=jax.ShapeDtypeStruct((b, dec_seq, N_HEADS * HEAD_DIM), F32),
        compiler_params=_params("parallel", "arbitrary"),
        name="nsa_decode",
    )(page_table, qbd, kc, vc, state_win, new_kv, new_win, gates, *([pool] * PAGES_PER_STEP))


EVEN_PLAN = (
    (0, 512, "copy", 0, 0, 0),
    (512, 512, "rope", 0, 0, 512),
    (1024, 512, "copy", 0, 1, 0),
    (1536, 512, "rope", 1, 1, 512),
    (2048, 1024, "copy", 0, 1, 1024),
)
ODD_PLAN = (
    (0, 1024, "rope", 0, 0, 0),
    (1024, 256, "rope", 1, 1, 0),
    (1280, 256, "copy", 0, 1, 256),
    (1536, 256, "rope", 2, 1, 512),
    (1792, 256, "copy", 0, 1, 768),
    (2048, 256, "rope", 3, 2, 0),
    (2304, 256, "copy", 0, 2, 256),
    (2560, 512, "sigmoid", 0, 3, 0),
)
ODD_QKV = N_HEADS * HEAD_DIM + 6 * NSA_HKV * HEAD_DIM


def _gate_columns():
    idx = np.full((NSA_HKV * LANES,), ODD_QKV + 3 * N_HEADS, np.int32)
    for h in range(N_HEADS):
        for r in range(3):
            idx[(h // NSA_GROUP) * LANES + (h % NSA_GROUP) * 3 + r] = ODD_QKV + h * 3 + r
    return idx


def _rope_tables(pos):
    half = HEAD_DIM // 2
    inv = jnp.power(ROPE_THETA, -jnp.arange(half, dtype=F32) / half)
    ang = pos.astype(F32)[:, None] * inv[None, :]
    cos, sin = jnp.cos(ang), jnp.sin(ang)
    return (jnp.concatenate([cos] * (LANES // half), axis=1),
            jnp.concatenate([-sin, sin] * (LANES // HEAD_DIM), axis=1))


def _gain_rows(*gains):
    return jnp.stack([jnp.tile(g.astype(F32), LANES // HEAD_DIM) for g in gains])


def _block_diag_queries(q, heads_per_group, n_groups, group_major):
    b, dec, n_heads, _ = q.shape
    h = np.arange(n_heads)
    grp = h // heads_per_group
    onehot = jnp.asarray(np.eye(n_groups, dtype=np.float32)[grp])
    bd = q.transpose(0, 2, 1, 3)[:, :, :, None, :] * onehot[None, :, None, :, None]
    if group_major:
        bd = bd.reshape(b, n_groups, heads_per_group, dec, n_groups, HEAD_DIM).transpose(0, 2, 1, 3, 4, 5)
    return bd.reshape(b, n_heads * dec, n_groups * HEAD_DIM)


def _prepare_weights(w_in0, w_out0, w_in1, w_out1, cmp_w1, cmp_w2, w_gu, w_down):
    w_in1_ext = jnp.concatenate([w_in1[0], jnp.zeros((D_MODEL, 1), w_in1.dtype)], axis=1)
    cols = np.concatenate([np.arange(ODD_QKV, dtype=np.int32), _gate_columns()])
    return dict(
        w_in0=w_in0[0].astype(BF16),
        w_out0_sb=w_out0[0, :SB_WIDTH].astype(BF16),
        w_out0_mb=w_out0[0, SB_WIDTH:].astype(BF16),
        w_in1=w_in1_ext[:, cols].astype(BF16),
        w_out1=w_out1[0].astype(BF16),
        cmp_w1=cmp_w1[0].astype(BF16),
        cmp_w2=cmp_w2[0].astype(BF16),
        w_gu=w_gu.astype(BF16),
        w_down=w_down.astype(BF16),
    )


def _cmp_pe_rows(cmp_pe):
    return cmp_pe[0].reshape(2, CMP_BLOCK // CMP_STRIDE, CMP_HALF)


def _tokens_last(x):
    return x.transpose(0, 2, 3, 4, 1)


def _trunk(x, q0, caches, w, g_mix0, g_q0, g_k0, g_mix1, g_q1, g_k1, cmp_pe, g_ffn):
    b, t, _ = x.shape
    m = b * t
    tm = min(512, m)
    xf = x.reshape(m, D_MODEL)
    pos = q0 + jnp.arange(t)
    cos, sin = _rope_tables(pos)
    if m // tm * tm != m or t % tm != 0:
        cos, sin = jnp.tile(cos, (m // t, 1)), jnp.tile(sin, (m // t, 1))

    proj0 = norm_matmul(xf, g_mix0[0], w["w_in0"], tm, 1024)
    q_l0, kv_l0 = head_post(proj0, cos, sin, _gain_rows(g_q0[0], g_k0[0]), EVEN_PLAN, (1024, 2048), tm)
    if caches is None:
        o_sb = sb_prefill(q_l0.reshape(b, t, 1024), kv_l0.reshape(b, t, 2048)).reshape(m, SB_WIDTH)
        o_mb = moba_prefill(q_l0.reshape(b, t, 1024), kv_l0.reshape(b, t, 2048)).reshape(m, SB_WIDTH)
    else:
        pool0, pool1, state_win, page_table = caches
        q4 = q_l0.reshape(b, t, N_HEADS, HEAD_DIM)
        new_kv0 = kv_l0.reshape(b, t, 2048)
        o_sb, kmean = sb_decode(_block_diag_queries(q4[:, :, :H_SB], 1, H_SB, False), new_kv0, pool0, page_table)
        o_mb = moba_decode(_block_diag_queries(q4[:, :, H_SB:], 1, N_HEADS - H_SB, False), kmean, new_kv0, pool0,
                           page_table, q0)
        o_sb, o_mb = o_sb.reshape(m, SB_WIDTH), o_mb.reshape(m, SB_WIDTH)
    h1 = matmul_residual([o_sb, o_mb], [w["w_out0_sb"], w["w_out0_mb"]], xf, tm)
    h2 = ffn_residual(h1, g_ffn[0], w["w_gu"][0], w["w_down"][0], tm, D_FF // 2)

    proj1 = norm_matmul(h2, g_mix1[0], w["w_in1"], tm, 1024)
    gains1 = _gain_rows(g_q1[0], g_k1[0, 0], g_k1[0, 1], g_k1[0, 2])
    q_l1, kv_l1, win_l1, gates = head_post(proj1, cos, sin, gains1, ODD_PLAN, (1024, 1024, 512, 512), tm)
    pe_rows = _cmp_pe_rows(cmp_pe)
    new_win = win_l1.reshape(b, t, 2, NSA_HKV, HEAD_DIM)
    if caches is None:
        ident = jnp.arange(m // PAGE_SIZE, dtype=jnp.int32).reshape(b, t // PAGE_SIZE)
        pages = _tokens_last(kv_l1.reshape(m // PAGE_SIZE, PAGE_SIZE, 4, NSA_HKV, HEAD_DIM))
        kc, vc = compress_pages(pages, ident, pe_rows, w["cmp_w1"], w["cmp_w2"])
        o_nsa = nsa_prefill(q_l1.reshape(b, t, 1024), kv_l1.reshape(b, t, 1024), win_l1.reshape(b, t, 512),
                            kc, vc, gates.reshape(b, t, 512)).reshape(m, 1024)
        win_state = new_win[:, t - min(WINDOW, t):]
    else:
        kc, vc = compress_pages(pool1, page_table, pe_rows, w["cmp_w1"], w["cmp_w2"])
        qbd1 = _block_diag_queries(q_l1.reshape(b, t, N_HEADS, HEAD_DIM), NSA_GROUP, NSA_HKV, True)
        o_nsa = nsa_decode(qbd1, kc, vc, _tokens_last(state_win), kv_l1.reshape(b, t, 1024), win_l1.reshape(b, t, 512),
                           gates.reshape(b, t, 512), pool1, page_table, q0).reshape(m, 1024)
        win_state = jnp.concatenate([state_win[:, t:], new_win], axis=1)
    h3 = matmul_residual([o_nsa], [w["w_out1"]], h2, tm)
    y = ffn_residual(h3, g_ffn[1], w["w_gu"][1], w["w_down"][1], tm, D_FF // 2)

    return (y.reshape(b, t, D_MODEL),
            kv_l0.reshape(1, b, t, 2, N_HEADS, HEAD_DIM),
            kv_l1.reshape(1, b, t, 4, NSA_HKV, HEAD_DIM),
            win_state[None])


def kernel(x_prompt, x_sample, cache_kv0, cache_kv1, state_win, page_table, g_mix0, w_in0, g_q0, g_k0, w_out0,
           g_mix1, w_in1, g_q1, g_k1, cmp_pe, cmp_w1, cmp_w2, w_out1, g_ffn, w_gu, w_down):
    assert w_in0.shape[0] == 1 and w_in1.shape[0] == 1, "one even and one odd layer"
    w = _prepare_weights(w_in0, w_out0, w_in1, w_out1, cmp_w1, cmp_w2, w_gu, w_down)
    norms = (g_mix0, g_q0, g_k0, g_mix1, g_q1, g_k1, cmp_pe, g_ffn)
    y_p, kv0_p, kv1_p, win_p = _trunk(x_prompt, 0, None, w, *norms)
    past_len = page_table.shape[1] * cache_kv0.shape[2]
    caches = (_tokens_last(cache_kv0[0]), _tokens_last(cache_kv1[0]), state_win[0], page_table)
    y_s, kv0_s, kv1_s, win_s = _trunk(x_sample, past_len, caches, w, *norms)
    return (y_p, y_s, kv0_p, kv0_s, kv1_p, kv1_s, win_p, win_s)
```

```python
import functools

import numpy as np
import jax
import jax.numpy as jnp
from jax import lax
from jax.experimental import pallas as pl
from jax.experimental.pallas import tpu as pltpu

F32 = jnp.float32
BF16 = jnp.bfloat16

D_MODEL = 1024
HEAD_DIM = 64
N_HEADS = 16
H_SB = 8
PAGE_SIZE = 128
MOBA_BLOCK = 256
MOBA_TOPK = 3
NSA_HKV = 4
NSA_GROUP = 4
CMP_BLOCK = 32
CMP_STRIDE = 16
CMP_HIDDEN = 256
SLC_BLOCK = 64
SLC_TOPK = 16
WINDOW = 512
D_FF = 2816
ROPE_THETA = 10000.0
NORM_EPS = 1e-6
NEG_BIG = -1e30
FORCE_SCORE = 1e9
ATTN_SCALE = HEAD_DIM ** -0.5

LANES = 128
HEADS_PER_SLAB = LANES // HEAD_DIM
VMEM_LIMIT_BYTES = 56 * 1024 * 1024


def _params(*sem):
    return pltpu.CompilerParams(dimension_semantics=sem, vmem_limit_bytes=VMEM_LIMIT_BYTES)


def _iota(shape, dim):
    return lax.broadcasted_iota(jnp.int32, shape, dim)


def _dot(a, b):
    return jnp.dot(a, b, preferred_element_type=F32)


def _dot_nt(a, b):
    return lax.dot_general(a, b, (((1,), (1,)), ((), ())), preferred_element_type=F32)


def _split_bf16(x):
    hi = x.astype(BF16)
    lo = (x - hi.astype(F32)).astype(BF16)
    return hi, lo


def _dot_exact_rhs(x, rhs_bf16):
    hi, lo = _split_bf16(x)
    return _dot(hi, rhs_bf16) + _dot(lo, rhs_bf16)


def _log_sigmoid_pair(z):
    t = jnp.log(1.0 + jnp.exp(-jnp.abs(z)))
    return jnp.minimum(z, 0.0) - t, jnp.minimum(-z, 0.0) - t


def _later_matrix(n):
    return jnp.where(_iota((n, n), 0) > _iota((n, n), 1), 1.0, 0.0).astype(BF16)


def _osm_init(rows, width):
    return (jnp.full((rows, 1), NEG_BIG, F32), jnp.zeros((rows, 1), F32), jnp.zeros((rows, width), F32))


def _osm_update(state, s, mask, weigh):
    m, l, acc = state
    sm = jnp.where(mask, s, NEG_BIG)
    m_new = jnp.maximum(m, jnp.max(sm, axis=1, keepdims=True))
    p = jnp.where(mask, jnp.exp(sm - m_new), 0.0)
    a = jnp.exp(m - m_new)
    return (m_new, a * l + jnp.sum(p, axis=1, keepdims=True), a * acc + weigh(p.astype(BF16)))


def _osm_final(state):
    _, l, acc = state
    return acc / jnp.maximum(l, 1e-30)


def _osm_reset(m_ref, l_ref, acc_ref):
    m_ref[...] = jnp.full_like(m_ref, NEG_BIG)
    l_ref[...] = jnp.zeros_like(l_ref)
    acc_ref[...] = jnp.zeros_like(acc_ref)


def _osm_update_biased(state, s_biased, weigh):
    m, l, acc = state
    m_new = jnp.maximum(m, jnp.max(s_biased, axis=1, keepdims=True))
    p = jnp.exp(s_biased - m_new)
    a = jnp.exp(m - m_new)
    return (m_new, a * l + jnp.sum(p, axis=1, keepdims=True), a * acc + weigh(p.astype(BF16)))


def _top_rank(score, n_real, axis):
    j = _iota(score.shape, axis)
    rank = jnp.zeros(score.shape, jnp.int32)
    for m in range(n_real):
        other = score[:, m:m + 1] if axis == 1 else score[m:m + 1, :]
        ahead = (other > score) | ((other == score) & (j > m))
        rank = rank + jnp.where(ahead, 1, 0)
    return rank


def _norm_proj_post_kernel(x_ref, g_ref, w_ref, cos_ref, sin_ref, gain_ref, *refs, plan):
    out_refs, p_ref = refs[:-1], refs[-1]
    x = x_ref[...]
    ms = jnp.mean(x * x, axis=-1, keepdims=True)
    p_ref[...] = _dot((x * lax.rsqrt(ms + NORM_EPS) * g_ref[...]).astype(BF16), w_ref[...])
    lane = _iota((1, LANES), 1)
    first_half = (lane & (HEAD_DIM - 1)) < HEAD_DIM // 2
    same_head = (_iota((LANES, LANES), 0) // HEAD_DIM) == (_iota((LANES, LANES), 1) // HEAD_DIM)
    head_mean = jnp.where(same_head, 1.0 / HEAD_DIM, 0.0).astype(BF16)
    cos = cos_ref[...]
    sin = sin_ref[...]
    for src, width, mode, gi, oi, dst in plan:
        if mode == "copy":
            out_refs[oi][:, dst:dst + width] = p_ref[:, src:src + width]
        elif mode == "sigmoid":
            x = p_ref[:, src:src + width]
            out_refs[oi][:, dst:dst + width] = 1.0 / (1.0 + jnp.exp(-x))
        else:
            for s in range(width // LANES):
                x = p_ref[:, src + s * LANES:src + (s + 1) * LANES]
                ms = _dot_exact_rhs(x * x, head_mean)
                y = x * lax.rsqrt(ms + NORM_EPS) * gain_ref[gi:gi + 1, :]
                other = jnp.where(first_half, pltpu.roll(y, LANES - HEAD_DIM // 2, 1),
                                  pltpu.roll(y, HEAD_DIM // 2, 1))
                out_refs[oi][:, dst + s * LANES:dst + (s + 1) * LANES] = y * cos + other * sin


def norm_proj_post(x, g, w_bf16, cos, sin, gains, plan, out_widths, tm):
    m, k = x.shape
    n = w_bf16.shape[1]
    period = cos.shape[0] // tm
    return pl.pallas_call(
        functools.partial(_norm_proj_post_kernel, plan=plan),
        grid=(m // tm,),
        in_specs=[pl.BlockSpec((tm, k), lambda i: (i, 0)),
                  pl.BlockSpec((1, k), lambda i: (0, 0)),
                  pl.BlockSpec((k, n), lambda i: (0, 0)),
                  pl.BlockSpec((tm, LANES), lambda i: (i % period, 0)),
                  pl.BlockSpec((tm, LANES), lambda i: (i % period, 0)),
                  pl.BlockSpec(gains.shape, lambda i: (0, 0))],
        out_specs=[pl.BlockSpec((tm, w), lambda i: (i, 0)) for w in out_widths],
        out_shape=[jax.ShapeDtypeStruct((m, w), F32) for w in out_widths],
        scratch_shapes=[pltpu.VMEM((tm, n), F32)],
        compiler_params=_params("parallel"),
        name="norm_proj_post",
    )(x, g.reshape(1, k), w_bf16, cos, sin, gains)


def _matmul_residual_kernel(*refs, n_in):
    a_refs, w_refs, r_ref, o_ref = refs[:n_in], refs[n_in:2 * n_in], refs[2 * n_in], refs[2 * n_in + 1]
    acc = r_ref[...]
    for a_ref, w_ref in zip(a_refs, w_refs):
        acc = acc + _dot(a_ref[...].astype(BF16), w_ref[...])
    o_ref[...] = acc


def matmul_residual(a_list, w_list, res, tm):
    m, n = res.shape
    n_in = len(a_list)
    in_specs = ([pl.BlockSpec((tm, a.shape[1]), lambda i: (i, 0)) for a in a_list]
                + [pl.BlockSpec(w.shape, lambda i: (0, 0)) for w in w_list]
                + [pl.BlockSpec((tm, n), lambda i: (i, 0))])
    return pl.pallas_call(
        functools.partial(_matmul_residual_kernel, n_in=n_in),
        grid=(m // tm,),
        in_specs=in_specs,
        out_specs=pl.BlockSpec((tm, n), lambda i: (i, 0)),
        out_shape=jax.ShapeDtypeStruct((m, n), F32),
        compiler_params=_params("parallel"),
        name="matmul_residual",
    )(*a_list, *w_list, res)


def _ffn_kernel(h_ref, g_ref, wg_ref, wu_ref, wd_ref, o_ref, xn_ref, acc_ref):
    j = pl.program_id(1)

    @pl.when(j == 0)
    def _():
        x = h_ref[...]
        ms = jnp.mean(x * x, axis=-1, keepdims=True)
        xn_ref[...] = (x * lax.rsqrt(ms + NORM_EPS) * g_ref[...]).astype(BF16)
        acc_ref[...] = x

    xn = xn_ref[...]
    gate = _dot(xn, wg_ref[...])
    up = _dot(xn, wu_ref[...])
    act = (gate / (1.0 + jnp.exp(-gate))) * up
    acc_ref[...] += _dot(act.astype(BF16), wd_ref[...])

    @pl.when(j == pl.num_programs(1) - 1)
    def _():
        o_ref[...] = acc_ref[...]


def ffn_residual(h, g, w_gu_bf16, w_down_bf16, tm, tf):
    m, k = h.shape
    n_chunks = D_FF // tf
    return pl.pallas_call(
        _ffn_kernel,
        grid=(m // tm, n_chunks),
        in_specs=[pl.BlockSpec((tm, k), lambda i, j: (i, 0)),
                  pl.BlockSpec((1, k), lambda i, j: (0, 0)),
                  pl.BlockSpec((k, tf), lambda i, j: (0, j)),
                  pl.BlockSpec((k, tf), lambda i, j: (0, j + n_chunks)),
                  pl.BlockSpec((tf, k), lambda i, j: (j, 0))],
        out_specs=pl.BlockSpec((tm, k), lambda i, j: (i, 0)),
        out_shape=jax.ShapeDtypeStruct((m, k), F32),
        scratch_shapes=[pltpu.VMEM((tm, k), BF16), pltpu.VMEM((tm, k), F32)],
        compiler_params=_params("parallel", "arbitrary"),
        name="ffn_residual",
    )(h, g.reshape(1, k), w_gu_bf16, w_gu_bf16, w_down_bf16)


ATTN_TQ = 256
ATTN_TK = 256


def _stage_keys_values(k_ref, k_cols, v_ref, v_cols, kb_ref, vt_ref, n_tiles, tk):
    kb_ref[...] = k_ref[0, :, k_cols].astype(BF16)
    for c in range(n_tiles):
        vt_ref[c] = v_ref[0, c * tk:(c + 1) * tk, v_cols].T.astype(BF16)


def _split_heads_t(qt, scale):
    row_head = _iota((LANES, 1), 0) // HEAD_DIM
    return [jnp.where(row_head == h, qt * scale, 0.0) for h in range(HEADS_PER_SLAB)]


def _osm_t_init(width, cols):
    return (jnp.full((1, cols), NEG_BIG, F32), jnp.zeros((1, cols), F32), jnp.zeros((width, cols), F32))


def _osm_t_update(state, s_biased_t, weigh):
    m, l, acc = state
    m_new = jnp.maximum(m, jnp.max(s_biased_t, axis=0, keepdims=True))
    p = jnp.exp(s_biased_t - m_new)
    a = jnp.exp(m - m_new)
    return (m_new, a * l + jnp.sum(p, axis=0, keepdims=True), a * acc + weigh(p.astype(BF16)))


def _osm_t_update2(state, s_a, s_b, weigh_a, weigh_b):
    m, l, acc = state
    m_new = jnp.maximum(m, jnp.maximum(jnp.max(s_a, axis=0, keepdims=True), jnp.max(s_b, axis=0, keepdims=True)))
    p_a = jnp.exp(s_a - m_new)
    p_b = jnp.exp(s_b - m_new)
    a = jnp.exp(m - m_new)
    l_new = a * l + jnp.sum(p_a, axis=0, keepdims=True) + jnp.sum(p_b, axis=0, keepdims=True)
    return (m_new, l_new, a * acc + weigh_a(p_a.astype(BF16)) + weigh_b(p_b.astype(BF16)))


def _osm_t_final(state):
    _, l, acc = state
    return acc / jnp.maximum(l, 1e-30)


def _weigh_heads_t(vt_ref, tile, w, tq):
    return jnp.concatenate([_dot(vt_ref[tile, h * HEAD_DIM:(h + 1) * HEAD_DIM, :], w[:, h * tq:(h + 1) * tq])
                            for h in range(HEADS_PER_SLAB)], axis=1)


def _unstack_heads_t(x, tq):
    return jnp.concatenate([x[:, h * tq:(h + 1) * tq] for h in range(HEADS_PER_SLAB)], axis=0)


def _sb_prefill_kernel(q_ref, k_ref, v_ref, o_ref, kb_ref, vt_ref, *, n_tiles):
    qi = pl.program_id(2)
    tq, tk = ATTN_TQ, ATTN_TK

    @pl.when(qi == 0)
    def _():
        _stage_keys_values(k_ref, slice(None), v_ref, slice(None), kb_ref, vt_ref, n_tiles, tk)

    qpos = jnp.concatenate([qi * tq + _iota((1, tq), 1)] * HEADS_PER_SLAB, axis=1)
    sooner = jnp.where(_iota((tk, tk), 0) < _iota((tk, tk), 1), 1.0, 0.0).astype(BF16)
    qs = jnp.concatenate(_split_heads_t(q_ref[0].T, ATTN_SCALE), axis=1).astype(BF16)

    def tile(kb, carry, on_diagonal):
        c, acc = carry
        z = _dot(kb_ref[pl.ds(pl.multiple_of(kb * tk, tk), tk), :], qs)
        ls, lk = _log_sigmoid_pair(z)
        if on_diagonal:
            past = (kb * tk + _iota((tk, 1), 0)) < qpos
            lk = jnp.where(past, lk, 0.0)
        hi, lo = _split_bf16(lk)
        gap = _dot(sooner, hi) + _dot(sooner, lo) + c
        w = jnp.exp(ls + gap)
        if on_diagonal:
            w = jnp.where(past, w, 0.0)
        acc = acc + _weigh_heads_t(vt_ref, kb, w.astype(BF16), tq)
        return c + jnp.sum(lk, axis=0, keepdims=True), acc

    carry = tile(qi, (jnp.zeros((1, HEADS_PER_SLAB * tq), F32), jnp.zeros((HEAD_DIM, HEADS_PER_SLAB * tq), F32)), True)
    _, acc = lax.fori_loop(0, qi, lambda i, carry: tile(qi - 1 - i, carry, False), carry)
    o_ref[0] = _unstack_heads_t(acc, tq).T


def sb_prefill(q, kv):
    b, t, _ = q.shape
    n_slab = H_SB // HEADS_PER_SLAB
    v_off = N_HEADS // HEADS_PER_SLAB
    assert t % ATTN_TK == 0 and ATTN_TQ == ATTN_TK
    return pl.pallas_call(
        functools.partial(_sb_prefill_kernel, n_tiles=t // ATTN_TK),
        grid=(b, n_slab, t // ATTN_TQ),
        in_specs=[pl.BlockSpec((1, ATTN_TQ, LANES), lambda bi, s, qi: (bi, qi, s)),
                  pl.BlockSpec((1, t, LANES), lambda bi, s, qi: (bi, 0, s)),
                  pl.BlockSpec((1, t, LANES), lambda bi, s, qi: (bi, 0, v_off + s))],
        out_specs=pl.BlockSpec((1, ATTN_TQ, LANES), lambda bi, s, qi: (bi, qi, s)),
        out_shape=jax.ShapeDtypeStruct((b, t, H_SB * HEAD_DIM), F32),
        scratch_shapes=[pltpu.VMEM((t, LANES), BF16), pltpu.VMEM((t // ATTN_TK, LANES, ATTN_TK), BF16)],
        compiler_params=_params("parallel", "parallel", "arbitrary"),
        name="sb_prefill",
    )(q, kv, kv)


def _moba_prefill_kernel(q_ref, k_ref, v_ref, o_ref, kmean_ref, kb_ref, vt_ref, *, n_blk):
    qi = pl.program_id(2)
    tq, tk = ATTN_TQ, ATTN_TK
    t_total = n_blk * MOBA_BLOCK
    blk_rows = -(-n_blk // 8) * 8

    @pl.when(qi == 0)
    def _():
        kmean_ref[...] = jnp.zeros_like(kmean_ref)
        for n in range(n_blk):
            blk = k_ref[0, n * MOBA_BLOCK:(n + 1) * MOBA_BLOCK, :]
            kmean_ref[n:n + 1, :] = jnp.sum(blk, axis=0, keepdims=True) * (1.0 / MOBA_BLOCK)
        _stage_keys_values(k_ref, slice(None), v_ref, slice(None), kb_ref, vt_ref, n_blk, tk)

    cols = HEADS_PER_SLAB * tq
    qpos = jnp.concatenate([qi * tq + _iota((1, tq), 1)] * HEADS_PER_SLAB, axis=1)
    blk_id = _iota((blk_rows, cols), 0)
    k_hi, k_lo = _split_bf16(kmean_ref[0:blk_rows, :])
    qf = jnp.concatenate(_split_heads_t(q_ref[0].T, 1.0), axis=1)
    qs = (qf * ATTN_SCALE).astype(BF16)
    q_hi, q_lo = _split_bf16(qf)
    gate = _dot(k_hi, q_hi) + _dot(k_lo, q_hi) + _dot(k_hi, q_lo)
    fully_past = (blk_id < qi) & (blk_id < n_blk)
    gate = jnp.where(fully_past, gate, -jnp.inf)
    sel = jnp.where(fully_past & (_top_rank(gate, n_blk, 0) < MOBA_TOPK), 1.0, 0.0)

    def scores(kb, limit):
        bias = jnp.where((kb * tk + _iota((tk, 1), 0)) <= limit, 0.0, NEG_BIG)
        return _dot(kb_ref[pl.ds(pl.multiple_of(kb * tk, tk), tk), :], qs) + bias

    def past_limit(kb, valid):
        picked = jnp.sum(jnp.where(blk_id == kb, sel, 0.0), axis=0, keepdims=True) > 0.5
        return jnp.where(picked, jnp.where(valid, t_total, -1), -1)

    def body(i, st):
        ka = qi - 2 * i
        kb = ka - 1
        kb_safe = jnp.maximum(kb, 0)
        s_a = scores(ka, jnp.where(ka == qi, qpos, past_limit(ka, True)))
        s_b = scores(kb_safe, past_limit(kb_safe, kb >= 0))
        return _osm_t_update2(st, s_a, s_b, lambda p: _weigh_heads_t(vt_ref, ka, p, tq),
                              lambda p: _weigh_heads_t(vt_ref, kb_safe, p, tq))

    out = _osm_t_final(lax.fori_loop(0, (qi + 2) // 2, body, _osm_t_init(HEAD_DIM, cols)))
    o_ref[0] = _unstack_heads_t(out, tq).T


def moba_prefill(q, kv):
    b, t, _ = q.shape
    n_slab = (N_HEADS - H_SB) // HEADS_PER_SLAB
    q_off = H_SB // HEADS_PER_SLAB
    v_off = N_HEADS // HEADS_PER_SLAB
    assert ATTN_TQ == MOBA_BLOCK and t % MOBA_BLOCK == 0 and t // MOBA_BLOCK <= LANES
    return pl.pallas_call(
        functools.partial(_moba_prefill_kernel, n_blk=t // MOBA_BLOCK),
        grid=(b, n_slab, t // ATTN_TQ),
        in_specs=[pl.BlockSpec((1, ATTN_TQ, LANES), lambda bi, s, qi: (bi, qi, q_off + s)),
                  pl.BlockSpec((1, t, LANES), lambda bi, s, qi: (bi, 0, q_off + s)),
                  pl.BlockSpec((1, t, LANES), lambda bi, s, qi: (bi, 0, v_off + q_off + s))],
        out_specs=pl.BlockSpec((1, ATTN_TQ, LANES), lambda bi, s, qi: (bi, qi, s)),
        out_shape=jax.ShapeDtypeStruct((b, t, (N_HEADS - H_SB) * HEAD_DIM), F32),
        scratch_shapes=[pltpu.VMEM((LANES, LANES), F32), pltpu.VMEM((t, LANES), BF16),
                        pltpu.VMEM((t // ATTN_TK, LANES, ATTN_TK), BF16)],
        compiler_params=_params("parallel", "parallel", "arbitrary"),
        name="moba_prefill",
    )(q, kv, kv)


CHUNKS_PER_PAGE = PAGE_SIZE // CMP_STRIDE
CMP_HALF = CMP_STRIDE * HEAD_DIM
PAGES_PER_STEP = 8


def _page_specs(block, index_of_page):
    return [pl.BlockSpec(block, functools.partial(index_of_page, g=g)) for g in range(PAGES_PER_STEP)]


def _compress_kernel(pt_ref, *refs, n_steps):
    page_refs = refs[:PAGES_PER_STEP]
    pe_ref, w1_ref, w2_ref, kc_ref, vc_ref, x_ref = refs[PAGES_PER_STEP:]
    p = pl.program_id(1)
    lane = _iota((1, LANES), 1)
    low = lane < HEAD_DIM
    r = _iota((PAGE_SIZE, PAGE_SIZE), 0)
    token = _iota((PAGE_SIZE, PAGE_SIZE), 1)
    perm = jnp.where(token == CMP_STRIDE * (r % CHUNKS_PER_PAGE) + r // CHUNKS_PER_PAGE, 1.0, 0.0).astype(BF16)
    for g, page_ref in enumerate(page_refs):
        rows = pl.ds(pl.multiple_of((p * PAGES_PER_STEP + g) * CHUNKS_PER_PAGE, CHUNKS_PER_PAGE), CHUNKS_PER_PAGE)
        for kv in range(2):
            for s in range(NSA_HKV // HEADS_PER_SLAB):
                two_heads = page_ref[0, kv, HEADS_PER_SLAB * s:HEADS_PER_SLAB * (s + 1)].reshape(LANES, PAGE_SIZE)
                hi, lo = _split_bf16(two_heads)
                slab = _dot_nt(perm, hi) + _dot_nt(perm, lo)
                for pp in range(CMP_STRIDE // 2):
                    even = slab[2 * pp * CHUNKS_PER_PAGE:(2 * pp + 1) * CHUNKS_PER_PAGE]
                    odd = slab[(2 * pp + 1) * CHUNKS_PER_PAGE:(2 * pp + 2) * CHUNKS_PER_PAGE]
                    head0 = jnp.where(low, even, pltpu.roll(odd, HEAD_DIM, 1))
                    head1 = jnp.where(low, pltpu.roll(even, HEAD_DIM, 1), odd)
                    x_ref[kv * NSA_HKV + 2 * s, rows, pp * LANES:(pp + 1) * LANES] = head0
                    x_ref[kv * NSA_HKV + 2 * s + 1, rows, pp * LANES:(pp + 1) * LANES] = head1

    @pl.when(p == n_steps - 1)
    def _():
        n_chunk = n_steps * PAGES_PER_STEP * CHUNKS_PER_PAGE
        for kv, out_ref in ((0, kc_ref), (1, vc_ref)):
            pe_a = pe_ref[kv, 0:1, :]
            pe_b = pe_ref[kv, 1:2, :]
            w1a = w1_ref[kv, 0:CMP_HALF, :]
            w1b = w1_ref[kv, CMP_HALF:2 * CMP_HALF, :]
            w2 = w2_ref[kv]
            for h in range(NSA_HKV):
                x = x_ref[kv * NSA_HKV + h]
                first = _dot((x + pe_a).astype(BF16), w1a)
                second = _dot((x + pe_b).astype(BF16), w1b)
                hid = first + pltpu.roll(second, n_chunk - 1, 0)
                hid = hid / (1.0 + jnp.exp(-hid))
                out_ref[0, :, h * HEAD_DIM:(h + 1) * HEAD_DIM] = _dot(hid.astype(BF16), w2)


def compress_pages(pages, page_table, pe, w1_bf16, w2_bf16):
    if page_table is None:
        b, n_pages = pages.shape[0], pages.shape[-1] // PAGE_SIZE
        page_table = jnp.zeros((1, 1), jnp.int32)

        def index_of_page(bi, p, pt, g):
            return (bi, 0, 0, 0, p * PAGES_PER_STEP + g)
    else:
        b, n_pages = page_table.shape

        def index_of_page(bi, p, pt, g):
            return (pt[bi, p * PAGES_PER_STEP + g], 0, 0, 0, 0)
    assert n_pages % PAGES_PER_STEP == 0
    n_chunk = n_pages * CHUNKS_PER_PAGE
    width = NSA_HKV * HEAD_DIM
    grid_spec = pltpu.PrefetchScalarGridSpec(
        num_scalar_prefetch=1,
        grid=(b, n_pages // PAGES_PER_STEP),
        in_specs=_page_specs((1, 2, NSA_HKV, HEAD_DIM, PAGE_SIZE), index_of_page) + [
                  pl.BlockSpec(pe.shape, lambda bi, p, pt: (0, 0, 0)),
                  pl.BlockSpec(w1_bf16.shape, lambda bi, p, pt: (0, 0, 0)),
                  pl.BlockSpec(w2_bf16.shape, lambda bi, p, pt: (0, 0, 0))],
        out_specs=[pl.BlockSpec((1, n_chunk, width), lambda bi, p, pt: (bi, 0, 0)),
                   pl.BlockSpec((1, n_chunk, width), lambda bi, p, pt: (bi, 0, 0))],
        scratch_shapes=[pltpu.VMEM((2 * NSA_HKV, n_chunk, CMP_HALF), F32)],
    )
    return pl.pallas_call(
        functools.partial(_compress_kernel, n_steps=n_pages // PAGES_PER_STEP),
        grid_spec=grid_spec,
        out_shape=[jax.ShapeDtypeStruct((b, n_chunk, width), F32)] * 2,
        compiler_params=_params("parallel", "arbitrary"),
        name="nsa_compress",
    )(page_table, *([pages] * PAGES_PER_STEP), pe, w1_bf16, w2_bf16)


NSA_TQ = 256
SLC_TK = 256
WIN_TK = 128


def _cover_matrix(n_cmp_rows, n_cols, cmp_axis):
    i = _iota((n_cmp_rows, n_cols) if cmp_axis == 0 else (n_cols, n_cmp_rows), cmp_axis)
    j = _iota((n_cmp_rows, n_cols) if cmp_axis == 0 else (n_cols, n_cmp_rows), 1 - cmp_axis)
    ratio = SLC_BLOCK // CMP_STRIDE
    reach = CMP_BLOCK // CMP_STRIDE - 1
    return jnp.where((i >= ratio * j - reach) & (i <= ratio * j + ratio - 1), 1.0, 0.0).astype(BF16)


def _slc_score(imp, cur, n_slc, axis):
    j = _iota(imp.shape, axis)
    forced = (j == 0) | (j == cur) | (j == cur - 1)
    score = jnp.where(forced, FORCE_SCORE, jnp.where(j > cur, -FORCE_SCORE, imp))
    return jnp.where(j < n_slc, score, -jnp.inf)


def _nsa_prefill_kernel(q_ref, slc_ref, win_ref, kc_ref, vc_ref, g_ref, o_ref, ks_ref, vst_ref, kw_ref, vwt_ref, *, t):
    qi = pl.program_id(1)
    tq = NSA_TQ
    n_slc = t // SLC_BLOCK
    top = min(SLC_TOPK, n_slc)
    n_cmp_rows = t // CMP_STRIDE
    n_slab = NSA_HKV // HEADS_PER_SLAB

    @pl.when(qi == 0)
    def _():
        for s in range(n_slab):
            v_cols = slice(KV_WIDTH + s * LANES, KV_WIDTH + (s + 1) * LANES)
            k_cols = slice(s * LANES, (s + 1) * LANES)
            _stage_keys_values(slc_ref, k_cols, slc_ref, v_cols, ks_ref.at[s], vst_ref.at[s], t // SLC_TK, SLC_TK)
            _stage_keys_values(win_ref, k_cols, win_ref, v_cols, kw_ref.at[s], vwt_ref.at[s], t // WIN_TK, WIN_TK)

    qt = q_ref[0].T * ATTN_SCALE
    gates_t = g_ref[0].T
    qpos = qi * tq + _iota((1, tq), 1)
    qpos4 = jnp.concatenate([qpos] * NSA_GROUP, axis=1)
    cover_t = _cover_matrix(n_cmp_rows, LANES, 1)
    zeros_head = jnp.zeros((HEAD_DIM, tq), F32)
    outs = []
    for hk in range(NSA_HKV):
        half, slab = hk % HEADS_PER_SLAB, hk // HEADS_PER_SLAB
        cols = slice(slab * LANES, (slab + 1) * LANES)
        head_rows = slice(half * HEAD_DIM, (half + 1) * HEAD_DIM)
        parts = []
        for g in range(NSA_GROUP):
            h = hk * NSA_GROUP + g
            qg = qt[h * HEAD_DIM:(h + 1) * HEAD_DIM]
            parts.append(jnp.concatenate([qg, zeros_head] if half == 0 else [zeros_head, qg], axis=0))
        qs = jnp.concatenate(parts, axis=1).astype(BF16)

        s_c = _dot(kc_ref[0, :, cols].astype(BF16), qs)
        ok_c = (CMP_STRIDE * _iota((n_cmp_rows, 1), 0) + CMP_BLOCK - 1) <= qpos4
        sm = jnp.where(ok_c, s_c, NEG_BIG)
        pc = jnp.where(ok_c, jnp.exp(sm - jnp.max(sm, axis=0, keepdims=True)), 0.0)
        pc = pc / jnp.maximum(jnp.sum(pc, axis=0, keepdims=True), 1e-30)
        o_c = _dot(vc_ref[0, :, cols].T[head_rows].astype(BF16), pc.astype(BF16))

        psum = pc[:, 0:tq] + pc[:, tq:2 * tq] + pc[:, 2 * tq:3 * tq] + pc[:, 3 * tq:4 * tq]
        p_hi, p_lo = _split_bf16(psum)
        imp_t = (_dot(cover_t, p_hi) + _dot(cover_t, p_lo))[0:n_slc]
        score_t = _slc_score(imp_t, qpos // SLC_BLOCK, n_slc, 0)
        sel_t = jnp.where(_top_rank(score_t, n_slc, 0) < top, 1.0, 0.0)
        sel = jnp.concatenate([sel_t, jnp.zeros((LANES - n_slc, tq), F32)], axis=0).astype(BF16)

        n_slc_tiles = (qi * tq + tq + SLC_TK - 1) // SLC_TK

        def slc_scores(kt, valid, on_diagonal, qs=qs, sel=sel, slab=slab):
            blk_of_key = kt * (SLC_TK // SLC_BLOCK) + _iota((SLC_TK, LANES), 0) // SLC_BLOCK
            expand = jnp.where(_iota((SLC_TK, LANES), 1) == blk_of_key, jnp.where(valid, 1.0, 0.0), 0.0).astype(BF16)
            bias = _dot(expand, sel) * (-NEG_BIG) + NEG_BIG
            if on_diagonal:
                bias = jnp.where(kt * SLC_TK + _iota((SLC_TK, 1), 0) <= qpos, bias, NEG_BIG)
            s = _dot(ks_ref[slab, pl.ds(pl.multiple_of(kt * SLC_TK, SLC_TK), SLC_TK), :], qs)
            return s + jnp.concatenate([bias] * NSA_GROUP, axis=1)

        def slc_pair(st, ka, on_diagonal, slc_scores=slc_scores, slab=slab, head_rows=head_rows):
            kb = jnp.maximum(ka - 1, 0)
            return _osm_t_update2(st, slc_scores(ka, True, on_diagonal), slc_scores(kb, ka >= 1, False),
                                  lambda p: _dot(vst_ref[slab, ka, head_rows, :], p),
                                  lambda p: _dot(vst_ref[slab, kb, head_rows, :], p))

        st = slc_pair(_osm_t_init(HEAD_DIM, NSA_GROUP * tq), n_slc_tiles - 1, True)
        o_s = _osm_t_final(lax.fori_loop(1, (n_slc_tiles + 1) // 2,
                                         lambda i, st, slc_pair=slc_pair: slc_pair(st, n_slc_tiles - 1 - 2 * i, False), st))

        first_tile = jnp.maximum(qi * (tq // WIN_TK) - WINDOW // WIN_TK, 0)
        last_tile = qi * (tq // WIN_TK) + tq // WIN_TK - 1

        def win_scores(kt, valid, qs=qs, slab=slab):
            dist = lax.bitcast_convert_type(qpos - (kt * WIN_TK + _iota((WIN_TK, 1), 0)), jnp.uint32)
            bias = jnp.where(dist < WINDOW, jnp.where(valid, 0.0, NEG_BIG), NEG_BIG)
            s = _dot(kw_ref[slab, pl.ds(pl.multiple_of(kt * WIN_TK, WIN_TK), WIN_TK), :], qs)
            return s + jnp.concatenate([bias] * NSA_GROUP, axis=1)

        def win_body(i, st, win_scores=win_scores, slab=slab, head_rows=head_rows):
            ka = last_tile - 2 * i
            kb = jnp.maximum(ka - 1, first_tile)
            return _osm_t_update2(st, win_scores(ka, True), win_scores(kb, ka - 1 >= first_tile),
                                  lambda p: _dot(vwt_ref[slab, ka, head_rows, :], p),
                                  lambda p: _dot(vwt_ref[slab, kb, head_rows, :], p))

        o_w = _osm_t_final(lax.fori_loop(0, (last_tile - first_tile + 2) // 2, win_body,
                                         _osm_t_init(HEAD_DIM, NSA_GROUP * tq)))

        for g in range(NSA_GROUP):
            cs = slice(g * tq, (g + 1) * tq)
            gc = hk * LANES + g * 3
            outs.append(gates_t[gc:gc + 1] * o_c[:, cs] + gates_t[gc + 1:gc + 2] * o_s[:, cs]
                        + gates_t[gc + 2:gc + 3] * o_w[:, cs])
    o_ref[0] = jnp.concatenate(outs, axis=0).T


def nsa_prefill(q, kv, win, kc, vc, gates):
    b, t, _ = q.shape
    width = NSA_HKV * HEAD_DIM
    n_chunk = t // CMP_STRIDE
    n_slc = t // SLC_BLOCK
    assert t % SLC_TK == 0 and n_chunk % LANES == 0 and n_slc <= LANES and n_slc % 8 == 0
    assert NSA_TQ % WIN_TK == 0 and NSA_TQ % SLC_TK == 0
    n_slab = NSA_HKV // HEADS_PER_SLAB
    return pl.pallas_call(
        functools.partial(_nsa_prefill_kernel, t=t),
        grid=(b, t // NSA_TQ),
        in_specs=[pl.BlockSpec((1, NSA_TQ, N_HEADS * HEAD_DIM), lambda bi, qi: (bi, qi, 0)),
                  pl.BlockSpec((1, t, 2 * width), lambda bi, qi: (bi, 0, 1)),
                  pl.BlockSpec((1, t, 2 * width), lambda bi, qi: (bi, 0, 0)),
                  pl.BlockSpec((1, n_chunk, width), lambda bi, qi: (bi, 0, 0)),
                  pl.BlockSpec((1, n_chunk, width), lambda bi, qi: (bi, 0, 0)),
                  pl.BlockSpec((1, NSA_TQ, NSA_HKV * LANES), lambda bi, qi: (bi, qi, 0))],
        out_specs=pl.BlockSpec((1, NSA_TQ, N_HEADS * HEAD_DIM), lambda bi, qi: (bi, qi, 0)),
        out_shape=jax.ShapeDtypeStruct((b, t, N_HEADS * HEAD_DIM), F32),
        scratch_shapes=[pltpu.VMEM((n_slab, t, LANES), BF16), pltpu.VMEM((n_slab, t // SLC_TK, LANES, SLC_TK), BF16),
                        pltpu.VMEM((n_slab, t, LANES), BF16), pltpu.VMEM((n_slab, t // WIN_TK, LANES, WIN_TK), BF16)],
        compiler_params=_params("parallel", "arbitrary"),
        name="nsa_prefill",
    )(q, kv, win, kc, vc, gates)


SB_WIDTH = H_SB * HEAD_DIM
KV0_V_OFF = N_HEADS * HEAD_DIM
PAGES_PER_MOBA_BLOCK = MOBA_BLOCK // PAGE_SIZE


def _load_new_rows(dst_ref, new_ref, col, width, dec_seq):
    dst_ref[...] = jnp.zeros_like(dst_ref)
    dst_ref[0:dec_seq, :] = new_ref[0, :, col:col + width]


def _heads_by_dim(page_tile):
    return page_tile.reshape(page_tile.shape[0] * HEAD_DIM, page_tile.shape[2])


def _fold_heads(res, dec_seq, n_heads):
    lane_head = _iota((1, n_heads * HEAD_DIM), 1) // HEAD_DIM
    out = jnp.zeros((dec_seq, n_heads * HEAD_DIM), F32)
    for h in range(n_heads):
        out = out + jnp.where(lane_head == h, res[h * dec_seq:(h + 1) * dec_seq, :], 0.0)
    return out


def _sb_decode_kernel(pt_ref, qbd_ref, new_ref, *refs, n_steps, dec_seq):
    k_refs = refs[:PAGES_PER_STEP]
    v_refs = refs[PAGES_PER_STEP:2 * PAGES_PER_STEP]
    o_ref, kmean_ref, c_ref, acc_ref, kpad_ref, vpad_ref = refs[2 * PAGES_PER_STEP:]
    p = pl.program_id(1)
    qbd = (qbd_ref[0] * ATTN_SCALE).astype(BF16)
    rows = qbd.shape[0]
    later = _later_matrix(PAGE_SIZE)
    n_blk = n_steps * PAGES_PER_STEP // PAGES_PER_MOBA_BLOCK

    @pl.when(p == 0)
    def _():
        kmean_ref[...] = jnp.zeros_like(kmean_ref)
        _load_new_rows(kpad_ref, new_ref, 0, SB_WIDTH, dec_seq)
        _load_new_rows(vpad_ref, new_ref, KV0_V_OFF, SB_WIDTH, dec_seq)
        valid = _iota((1, PAGE_SIZE), 1) < lax.rem(_iota((rows, 1), 0), dec_seq)
        ls, lk = _log_sigmoid_pair(_dot_nt(qbd, kpad_ref[...].astype(BF16)))
        lk = jnp.where(valid, lk, 0.0)
        w = jnp.where(valid, jnp.exp(ls + _dot_exact_rhs(lk, later)), 0.0)
        acc_ref[...] = _dot(w.astype(BF16), vpad_ref[...].astype(BF16))
        c_ref[...] = jnp.sum(lk, axis=1, keepdims=True)

    zs = [_dot(qbd, _heads_by_dim(k_ref[0, 0, 0:H_SB]).astype(BF16)) for k_ref in k_refs]
    c = c_ref[...]
    acc = acc_ref[...]
    for z, v_ref in zip(zs, v_refs):
        ls, lk = _log_sigmoid_pair(z)
        w = jnp.exp(ls + _dot_exact_rhs(lk, later) + c)
        acc = acc + _dot_nt(w.astype(BF16), _heads_by_dim(v_ref[0, 0]).astype(BF16))
        c = c + jnp.sum(lk, axis=1, keepdims=True)
    c_ref[...] = c
    acc_ref[...] = acc

    lane = _iota((1, LANES), 1)
    kmean = kmean_ref[0]
    for j in range(PAGES_PER_STEP // PAGES_PER_MOBA_BLOCK):
        total = jnp.zeros(((N_HEADS - H_SB) * HEAD_DIM, 1), F32)
        for k_ref in k_refs[j * PAGES_PER_MOBA_BLOCK:(j + 1) * PAGES_PER_MOBA_BLOCK]:
            total = total + jnp.sum(_heads_by_dim(k_ref[0, 0, H_SB:N_HEADS]), axis=1, keepdims=True)
        blk = n_blk - 1 - (p * (PAGES_PER_STEP // PAGES_PER_MOBA_BLOCK) + j)
        kmean = jnp.where(lane == blk, total * (1.0 / MOBA_BLOCK), kmean)
    kmean_ref[0] = kmean

    @pl.when(p == n_steps - 1)
    def _():
        o_ref[0] = _fold_heads(acc_ref[...], dec_seq, H_SB)


def sb_decode(qbd, new_kv, pool, page_table):
    b, n_pages = page_table.shape
    dec_seq = new_kv.shape[1]
    rows = qbd.shape[1]
    assert n_pages % PAGES_PER_STEP == 0 and PAGES_PER_STEP % PAGES_PER_MOBA_BLOCK == 0
    assert n_pages // PAGES_PER_MOBA_BLOCK <= LANES

    def page_of(bi, p, pt, g):
        return pt[bi, n_pages - 1 - (p * PAGES_PER_STEP + g)]

    grid_spec = pltpu.PrefetchScalarGridSpec(
        num_scalar_prefetch=1,
        grid=(b, n_pages // PAGES_PER_STEP),
        in_specs=[pl.BlockSpec((1, rows, SB_WIDTH), lambda bi, p, pt: (bi, 0, 0)),
                  pl.BlockSpec((1, dec_seq, new_kv.shape[2]), lambda bi, p, pt: (bi, 0, 0))]
        + _page_specs((1, 1, N_HEADS, HEAD_DIM, PAGE_SIZE), lambda bi, p, pt, g: (page_of(bi, p, pt, g), 0, 0, 0, 0))
        + _page_specs((1, 1, H_SB, HEAD_DIM, PAGE_SIZE), lambda bi, p, pt, g: (page_of(bi, p, pt, g), 1, 0, 0, 0)),
        out_specs=[pl.BlockSpec((1, dec_seq, SB_WIDTH), lambda bi, p, pt: (bi, 0, 0)),
                   pl.BlockSpec((1, SB_WIDTH, LANES), lambda bi, p, pt: (bi, 0, 0))],
        scratch_shapes=[pltpu.VMEM((rows, 1), F32), pltpu.VMEM((rows, SB_WIDTH), F32),
                        pltpu.VMEM((PAGE_SIZE, SB_WIDTH), F32), pltpu.VMEM((PAGE_SIZE, SB_WIDTH), F32)],
    )
    return pl.pallas_call(
        functools.partial(_sb_decode_kernel, n_steps=n_pages // PAGES_PER_STEP, dec_seq=dec_seq),
        grid_spec=grid_spec,
        out_shape=[jax.ShapeDtypeStruct((b, dec_seq, SB_WIDTH), F32),
                   jax.ShapeDtypeStruct((b, SB_WIDTH, LANES), F32)],
        compiler_params=_params("parallel", "arbitrary"),
        name="sb_decode",
    )(page_table, qbd, new_kv, *([pool] * (2 * PAGES_PER_STEP)))


def _moba_decode_kernel(pt_ref, qbd_ref, kmean_ref, new_ref, *refs, n_steps, dec_seq, past_len):
    kv_refs = refs[:PAGES_PER_STEP]
    o_ref, sel_ref, m_ref, l_ref, acc_ref, kpad_ref, vpad_ref = refs[PAGES_PER_STEP:]
    p = pl.program_id(1)
    n_blk = n_steps * PAGES_PER_STEP // PAGES_PER_MOBA_BLOCK
    qf = qbd_ref[0]
    qb = (qf * ATTN_SCALE).astype(BF16)
    rows = qf.shape[0]
    t_of_row = lax.rem(_iota((rows, 1), 0), dec_seq)
    blk_id = _iota((1, LANES), 1)

    @pl.when(p == 0)
    def _():
        q_hi, q_lo = _split_bf16(qf)
        k_hi, k_lo = _split_bf16(kmean_ref[0])
        gate = _dot(q_hi, k_hi) + _dot(q_hi, k_lo) + _dot(q_lo, k_hi)
        fully_past = (blk_id < (past_len + t_of_row) // MOBA_BLOCK) & (blk_id < n_blk)
        gate = jnp.where(fully_past, gate, -jnp.inf)
        sel_ref[...] = jnp.where(fully_past & (_top_rank(gate, n_blk, 1) < MOBA_TOPK), 1.0, 0.0)
        _osm_reset(m_ref, l_ref, acc_ref)

    sel = sel_ref[...]
    s_parts, mask_parts = [], []
    for g, kv_ref in enumerate(kv_refs):
        s_parts.append(_dot(qb, _heads_by_dim(kv_ref[0, 0]).astype(BF16)))
        blk = (p * PAGES_PER_STEP + g) // PAGES_PER_MOBA_BLOCK
        picked = jnp.sum(jnp.where(blk_id == blk, sel, 0.0), axis=1, keepdims=True) > 0.5
        mask_parts.append(jnp.broadcast_to(picked, (rows, PAGE_SIZE)))

    def weigh_pages(pb):
        out = jnp.zeros((rows, SB_WIDTH), F32)
        for g, kv_ref in enumerate(kv_refs):
            out = out + _dot_nt(pb[:, g * PAGE_SIZE:(g + 1) * PAGE_SIZE], _heads_by_dim(kv_ref[0, 1]).astype(BF16))
        return out

    st = _osm_update((m_ref[...], l_ref[...], acc_ref[...]), jnp.concatenate(s_parts, axis=1),
                     jnp.concatenate(mask_parts, axis=1), weigh_pages)
    m_ref[...], l_ref[...], acc_ref[...] = st

    @pl.when(p == n_steps - 1)
    def _():
        _load_new_rows(kpad_ref, new_ref, SB_WIDTH, SB_WIDTH, dec_seq)
        _load_new_rows(vpad_ref, new_ref, KV0_V_OFF + SB_WIDTH, SB_WIDTH, dec_seq)
        key = _iota((1, PAGE_SIZE), 1)
        vb = vpad_ref[...].astype(BF16)
        fin = _osm_update((m_ref[...], l_ref[...], acc_ref[...]), _dot_nt(qb, kpad_ref[...].astype(BF16)),
                          (key <= t_of_row) & (key < dec_seq), lambda pb: _dot(pb, vb))
        o_ref[0] = _fold_heads(_osm_final(fin), dec_seq, N_HEADS - H_SB)


def moba_decode(qbd, kmean, new_kv, pool, page_table, past_len):
    b, n_pages = page_table.shape
    dec_seq = new_kv.shape[1]
    rows = qbd.shape[1]
    assert past_len % MOBA_BLOCK == 0 and n_pages * PAGE_SIZE == past_len and dec_seq <= PAGE_SIZE
    assert n_pages % PAGES_PER_STEP == 0
    grid_spec = pltpu.PrefetchScalarGridSpec(
        num_scalar_prefetch=1,
        grid=(b, n_pages // PAGES_PER_STEP),
        in_specs=[pl.BlockSpec((1, rows, SB_WIDTH), lambda bi, p, pt: (bi, 0, 0)),
                  pl.BlockSpec((1, SB_WIDTH, LANES), lambda bi, p, pt: (bi, 0, 0)),
                  pl.BlockSpec((1, dec_seq, new_kv.shape[2]), lambda bi, p, pt: (bi, 0, 0))]
        + _page_specs((1, 2, N_HEADS - H_SB, HEAD_DIM, PAGE_SIZE),
                      lambda bi, p, pt, g: (pt[bi, p * PAGES_PER_STEP + g], 0, 1, 0, 0)),
        out_specs=pl.BlockSpec((1, dec_seq, SB_WIDTH), lambda bi, p, pt: (bi, 0, 0)),
        scratch_shapes=[pltpu.VMEM((rows, LANES), F32), pltpu.VMEM((rows, 1), F32), pltpu.VMEM((rows, 1), F32),
                        pltpu.VMEM((rows, SB_WIDTH), F32),
                        pltpu.VMEM((PAGE_SIZE, SB_WIDTH), F32), pltpu.VMEM((PAGE_SIZE, SB_WIDTH), F32)],
    )
    return pl.pallas_call(
        functools.partial(_moba_decode_kernel, n_steps=n_pages // PAGES_PER_STEP, dec_seq=dec_seq, past_len=past_len),
        grid_spec=grid_spec,
        out_shape=jax.ShapeDtypeStruct((b, dec_seq, SB_WIDTH), F32),
        compiler_params=_params("parallel", "arbitrary"),
        name="moba_decode",
    )(page_table, qbd, kmean, new_kv, *([pool] * PAGES_PER_STEP))


KV_WIDTH = NSA_HKV * HEAD_DIM


def _nsa_decode_kernel(pt_ref, qbd_ref, kc_ref, vc_ref, state_ref, new_kv_ref, new_win_ref, g_ref, *refs,
                       n_steps, dec_seq, past_len):
    page_refs = refs[:PAGES_PER_STEP]
    o_ref, sel_ref, oc_ref, ow_ref, m_ref, l_ref, acc_ref, kpad_ref, vpad_ref = refs[PAGES_PER_STEP:]
    p = pl.program_id(1)
    qb = (qbd_ref[0] * ATTN_SCALE).astype(BF16)
    rows = qb.shape[0]
    grp_rows = NSA_HKV * dec_seq
    n_cmp_rows = kc_ref.shape[1]
    n_slc = past_len // SLC_BLOCK + 1
    slc_lanes = sel_ref.shape[1]
    t_of_row = lax.rem(_iota((rows, 1), 0), dec_seq)
    qpos = past_len + t_of_row
    key = _iota((1, PAGE_SIZE), 1)

    @pl.when(p == 0)
    def _():
        s_c = _dot_nt(qb, kc_ref[0].astype(BF16))
        ok_c = (CMP_STRIDE * _iota((1, n_cmp_rows), 1) + CMP_BLOCK - 1) <= qpos
        sm = jnp.where(ok_c, s_c, NEG_BIG)
        pc = jnp.where(ok_c, jnp.exp(sm - jnp.max(sm, axis=1, keepdims=True)), 0.0)
        pc = pc / jnp.maximum(jnp.sum(pc, axis=1, keepdims=True), 1e-30)
        oc_ref[...] = _dot(pc.astype(BF16), vc_ref[0].astype(BF16))
        psum = pc[0:grp_rows]
        for g in range(1, NSA_GROUP):
            psum = psum + pc[g * grp_rows:(g + 1) * grp_rows]
        imp = _dot_exact_rhs(psum, _cover_matrix(n_cmp_rows, slc_lanes, 0))
        score = _slc_score(imp, qpos[0:grp_rows] // SLC_BLOCK, n_slc, 1)
        sel = jnp.where(_top_rank(score, n_slc, 1) < min(SLC_TOPK, n_slc), 1.0, 0.0)
        sel_ref[...] = jnp.concatenate([sel] * NSA_GROUP, axis=0)
        vwt = _heads_by_dim(state_ref[0, 1]).astype(BF16)
        st = _osm_update(_osm_init(rows, KV_WIDTH), _dot(qb, _heads_by_dim(state_ref[0, 0]).astype(BF16)),
                         _iota((1, WINDOW), 1) > t_of_row, lambda pb: _dot_nt(pb, vwt))
        _load_new_rows(kpad_ref, new_win_ref, 0, KV_WIDTH, dec_seq)
        _load_new_rows(vpad_ref, new_win_ref, KV_WIDTH, KV_WIDTH, dec_seq)
        vb = vpad_ref[...].astype(BF16)
        st = _osm_update(st, _dot_nt(qb, kpad_ref[...].astype(BF16)), (key <= t_of_row) & (key < dec_seq),
                         lambda pb: _dot(pb, vb))
        ow_ref[...] = _osm_final(st)
        _osm_reset(m_ref, l_ref, acc_ref)

    keys_per_step = PAGES_PER_STEP * PAGE_SIZE
    blk_of_key = p * (keys_per_step // SLC_BLOCK) + _iota((slc_lanes, keys_per_step), 1) // SLC_BLOCK
    expand = jnp.where(_iota((slc_lanes, keys_per_step), 0) == blk_of_key, 1.0, 0.0).astype(BF16)
    s_all = jnp.concatenate([_dot(qb, _heads_by_dim(ref[0, 0]).astype(BF16)) for ref in page_refs], axis=1)

    def weigh_pages(pb):
        out = jnp.zeros((rows, KV_WIDTH), F32)
        for g, ref in enumerate(page_refs):
            out = out + _dot_nt(pb[:, g * PAGE_SIZE:(g + 1) * PAGE_SIZE], _heads_by_dim(ref[0, 1]).astype(BF16))
        return out

    st = _osm_update((m_ref[...], l_ref[...], acc_ref[...]), s_all,
                     _dot(sel_ref[...].astype(BF16), expand) > 0.5, weigh_pages)
    m_ref[...], l_ref[...], acc_ref[...] = st

    @pl.when(p == n_steps - 1)
    def _():
        _load_new_rows(kpad_ref, new_kv_ref, 2 * KV_WIDTH, KV_WIDTH, dec_seq)
        _load_new_rows(vpad_ref, new_kv_ref, 3 * KV_WIDTH, KV_WIDTH, dec_seq)
        cur = past_len // SLC_BLOCK
        picked = sel_ref[:, cur:cur + 1] > 0.5
        vb = vpad_ref[...].astype(BF16)
        fin = _osm_update((m_ref[...], l_ref[...], acc_ref[...]), _dot_nt(qb, kpad_ref[...].astype(BF16)),
                          picked & (key <= t_of_row) & (key < dec_seq), lambda pb: _dot(pb, vb))
        o_s = _osm_final(fin)
        o_c = oc_ref[...]
        o_w = ow_ref[...]
        gates = g_ref[0]
        for hk in range(NSA_HKV):
            for g in range(NSA_GROUP):
                rs = slice(g * grp_rows + hk * dec_seq, g * grp_rows + (hk + 1) * dec_seq)
                hs = slice(hk * HEAD_DIM, (hk + 1) * HEAD_DIM)
                gc = hk * LANES + g * 3
                h = hk * NSA_GROUP + g
                o_ref[0, :, h * HEAD_DIM:(h + 1) * HEAD_DIM] = (
                    gates[:, gc:gc + 1] * o_c[rs, hs] + gates[:, gc + 1:gc + 2] * o_s[rs, hs]
                    + gates[:, gc + 2:gc + 3] * o_w[rs, hs])


def nsa_decode(qbd, kc, vc, state_win, new_kv, new_win, gates, pool, page_table, past_len):
    b, n_pages = page_table.shape
    dec_seq = new_kv.shape[1]
    rows = qbd.shape[1]
    n_chunk = kc.shape[1]
    n_slc = past_len // SLC_BLOCK + 1
    slc_lanes = -(-n_slc // LANES) * LANES
    assert past_len % SLC_BLOCK == 0 and past_len >= WINDOW and state_win.shape[-1] == WINDOW and dec_seq <= SLC_BLOCK
    assert n_pages % PAGES_PER_STEP == 0
    grid_spec = pltpu.PrefetchScalarGridSpec(
        num_scalar_prefetch=1,
        grid=(b, n_pages // PAGES_PER_STEP),
        in_specs=[pl.BlockSpec((1, rows, KV_WIDTH), lambda bi, p, pt: (bi, 0, 0)),
                  pl.BlockSpec((1, n_chunk, KV_WIDTH), lambda bi, p, pt: (bi, 0, 0)),
                  pl.BlockSpec((1, n_chunk, KV_WIDTH), lambda bi, p, pt: (bi, 0, 0)),
                  pl.BlockSpec((1, 2, NSA_HKV, HEAD_DIM, WINDOW), lambda bi, p, pt: (bi, 0, 0, 0, 0)),
                  pl.BlockSpec((1, dec_seq, 4 * KV_WIDTH), lambda bi, p, pt: (bi, 0, 0)),
                  pl.BlockSpec((1, dec_seq, 2 * KV_WIDTH), lambda bi, p, pt: (bi, 0, 0)),
                  pl.BlockSpec((1, dec_seq, NSA_HKV * LANES), lambda bi, p, pt: (bi, 0, 0))]
        + _page_specs((1, 2, NSA_HKV, HEAD_DIM, PAGE_SIZE),
                      lambda bi, p, pt, g: (pt[bi, p * PAGES_PER_STEP + g], 1, 0, 0, 0)),
        out_specs=pl.BlockSpec((1, dec_seq, N_HEADS * HEAD_DIM), lambda bi, p, pt: (bi, 0, 0)),
        scratch_shapes=[pltpu.VMEM((rows, slc_lanes), F32), pltpu.VMEM((rows, KV_WIDTH), F32),
                        pltpu.VMEM((rows, KV_WIDTH), F32), pltpu.VMEM((rows, 1), F32), pltpu.VMEM((rows, 1), F32),
                        pltpu.VMEM((rows, KV_WIDTH), F32),
                        pltpu.VMEM((PAGE_SIZE, KV_WIDTH), F32), pltpu.VMEM((PAGE_SIZE, KV_WIDTH), F32)],
    )
    return pl.pallas_call(
        functools.partial(_nsa_decode_kernel, n_steps=n_pages // PAGES_PER_STEP, dec_seq=dec_seq, past_len=past_len),
        grid_spec=grid_spec,
        out_shape=jax.ShapeDtypeStruct((b, dec_seq, N_HEADS * HEAD_DIM), F32),
        compiler_params=_params("parallel", "arbitrary"),
        name="nsa_decode",
    )(page_table, qbd, kc, vc, state_win, new_kv, new_win, gates, *([pool] * PAGES_PER_STEP))


EVEN_PLAN = (
    (0, 512, "copy", 0, 0, 0),
    (512, 512, "rope", 0, 0, 512),
    (1024, 512, "copy", 0, 1, 0),
    (1536, 512, "rope", 1, 1, 512),
    (2048, 1024, "copy", 0, 1, 1024),
)
ODD_PLAN = (
    (0, 1024, "rope", 0, 0, 0),
    (1024, 256, "rope", 1, 1, 0),
    (1280, 256, "copy", 0, 1, 256),
    (1536, 256, "rope", 2, 1, 512),
    (1792, 256, "copy", 0, 1, 768),
    (2048, 256, "rope", 3, 2, 0),
    (2304, 256, "copy", 0, 2, 256),
    (2560, 512, "sigmoid", 0, 3, 0),
)
ODD_QKV = N_HEADS * HEAD_DIM + 6 * NSA_HKV * HEAD_DIM


def _gate_columns():
    idx = np.full((NSA_HKV * LANES,), ODD_QKV + 3 * N_HEADS, np.int32)
    for h in range(N_HEADS):
        for r in range(3):
            idx[(h // NSA_GROUP) * LANES + (h % NSA_GROUP) * 3 + r] = ODD_QKV + h * 3 + r
    return idx


def _rope_tables(pos):
    half = HEAD_DIM // 2
    inv = jnp.power(ROPE_THETA, -jnp.arange(half, dtype=F32) / half)
    ang = pos.astype(F32)[:, None] * inv[None, :]
    cos, sin = jnp.cos(ang), jnp.sin(ang)
    return (jnp.concatenate([cos] * (LANES // half), axis=1),
            jnp.concatenate([-sin, sin] * (LANES // HEAD_DIM), axis=1))


def _gain_rows(*gains):
    return jnp.stack([jnp.tile(g.astype(F32), LANES // HEAD_DIM) for g in gains])


def _block_diag_queries(q, heads_per_group, n_groups, group_major):
    b, dec, n_heads, _ = q.shape
    h = np.arange(n_heads)
    grp = h // heads_per_group
    onehot = jnp.asarray(np.eye(n_groups, dtype=np.float32)[grp])
    bd = q.transpose(0, 2, 1, 3)[:, :, :, None, :] * onehot[None, :, None, :, None]
    if group_major:
        bd = bd.reshape(b, n_groups, heads_per_group, dec, n_groups, HEAD_DIM).transpose(0, 2, 1, 3, 4, 5)
    return bd.reshape(b, n_heads * dec, n_groups * HEAD_DIM)


def _prepare_weights(w_in0, w_out0, w_in1, w_out1, cmp_w1, cmp_w2, w_gu, w_down):
    w_in1_ext = jnp.concatenate([w_in1[0], jnp.zeros((D_MODEL, 1), w_in1.dtype)], axis=1)
    cols = np.concatenate([np.arange(ODD_QKV, dtype=np.int32), _gate_columns()])
    return dict(
        w_in0=w_in0[0].astype(BF16),
        w_out0_sb=w_out0[0, :SB_WIDTH].astype(BF16),
        w_out0_mb=w_out0[0, SB_WIDTH:].astype(BF16),
        w_in1=w_in1_ext[:, cols].astype(BF16),
        w_out1=w_out1[0].astype(BF16),
        cmp_w1=cmp_w1[0].astype(BF16),
        cmp_w2=cmp_w2[0].astype(BF16),
        w_gu=w_gu.astype(BF16),
        w_down=w_down.astype(BF16),
    )


def _cmp_pe_rows(cmp_pe):
    return cmp_pe[0].reshape(2, CMP_BLOCK // CMP_STRIDE, CMP_HALF)


def _tokens_last(x):
    return x.transpose(0, 2, 3, 4, 1)


def _trunk(x, q0, caches, w, g_mix0, g_q0, g_k0, g_mix1, g_q1, g_k1, cmp_pe, g_ffn):
    b, t, _ = x.shape
    m = b * t
    tm = min(512, m)
    xf = x.reshape(m, D_MODEL)
    pos = q0 + jnp.arange(t)
    cos, sin = _rope_tables(pos)
    if m // tm * tm != m or t % tm != 0:
        cos, sin = jnp.tile(cos, (m // t, 1)), jnp.tile(sin, (m // t, 1))

    q_l0, kv_l0 = norm_proj_post(xf, g_mix0[0], w["w_in0"], cos, sin, _gain_rows(g_q0[0], g_k0[0]), EVEN_PLAN,
                                 (1024, 2048), tm)
    if caches is None:
        o_sb = sb_prefill(q_l0.reshape(b, t, 1024), kv_l0.reshape(b, t, 2048)).reshape(m, SB_WIDTH)
        o_mb = moba_prefill(q_l0.reshape(b, t, 1024), kv_l0.reshape(b, t, 2048)).reshape(m, SB_WIDTH)
    else:
        pool0, pool1, state_win, page_table = caches
        q4 = q_l0.reshape(b, t, N_HEADS, HEAD_DIM)
        new_kv0 = kv_l0.reshape(b, t, 2048)
        o_sb, kmean = sb_decode(_block_diag_queries(q4[:, :, :H_SB], 1, H_SB, False), new_kv0, pool0, page_table)
        o_mb = moba_decode(_block_diag_queries(q4[:, :, H_SB:], 1, N_HEADS - H_SB, False), kmean, new_kv0, pool0,
                           page_table, q0)
        o_sb, o_mb = o_sb.reshape(m, SB_WIDTH), o_mb.reshape(m, SB_WIDTH)
    h1 = matmul_residual([o_sb, o_mb], [w["w_out0_sb"], w["w_out0_mb"]], xf, tm)
    h2 = ffn_residual(h1, g_ffn[0], w["w_gu"][0], w["w_down"][0], tm, D_FF // 2)

    gains1 = _gain_rows(g_q1[0], g_k1[0, 0], g_k1[0, 1], g_k1[0, 2])
    q_l1, kv_l1, win_l1, gates = norm_proj_post(h2, g_mix1[0], w["w_in1"], cos, sin, gains1, ODD_PLAN,
                                                (1024, 1024, 512, 512), tm)
    pe_rows = _cmp_pe_rows(cmp_pe)
    new_win = win_l1.reshape(b, t, 2, NSA_HKV, HEAD_DIM)
    if caches is None:
        kv1_t = _tokens_last(kv_l1.reshape(b, t, 4, NSA_HKV, HEAD_DIM))
        kc, vc = compress_pages(kv1_t, None, pe_rows, w["cmp_w1"], w["cmp_w2"])
        o_nsa = nsa_prefill(q_l1.reshape(b, t, 1024), kv_l1.reshape(b, t, 1024), win_l1.reshape(b, t, 512),
                            kc, vc, gates.reshape(b, t, 512)).reshape(m, 1024)
        win_state = new_win[:, t - min(WINDOW, t):]
    else:
        kc, vc = compress_pages(pool1, page_table, pe_rows, w["cmp_w1"], w["cmp_w2"])
        qbd1 = _block_diag_queries(q_l1.reshape(b, t, N_HEADS, HEAD_DIM), NSA_GROUP, NSA_HKV, True)
        o_nsa = nsa_decode(qbd1, kc, vc, _tokens_last(state_win), kv_l1.reshape(b, t, 1024), win_l1.reshape(b, t, 512),
                           gates.reshape(b, t, 512), pool1, page_table, q0).reshape(m, 1024)
        win_state = jnp.concatenate([state_win[:, t:], new_win], axis=1)
    h3 = matmul_residual([o_nsa], [w["w_out1"]], h2, tm)
    y = ffn_residual(h3, g_ffn[1], w["w_gu"][1], w["w_down"][1], tm, D_FF // 2)

    return (y.reshape(b, t, D_MODEL),
            kv_l0.reshape(1, b, t, 2, N_HEADS, HEAD_DIM),
            kv_l1.reshape(1, b, t, 4, NSA_HKV, HEAD_DIM),
            win_state[None])


def kernel(x_prompt, x_sample, cache_kv0, cache_kv1, state_win, page_table, g_mix0, w_in0, g_q0, g_k0, w_out0,
           g_mix1, w_in1, g_q1, g_k1, cmp_pe, cmp_w1, cmp_w2, w_out1, g_ffn, w_gu, w_down):
    assert w_in0.shape[0] == 1 and w_in1.shape[0] == 1, "one even and one odd layer"
    w = _prepare_weights(w_in0, w_out0, w_in1, w_out1, cmp_w1, cmp_w2, w_gu, w_down)
    norms = (g_mix0, g_q0, g_k0, g_mix1, g_q1, g_k1, cmp_pe, g_ffn)
    y_p, kv0_p, kv1_p, win_p = _trunk(x_prompt, 0, None, w, *norms)
    past_len = page_table.shape[1] * cache_kv0.shape[2]
    caches = (_tokens_last(cache_kv0[0]), _tokens_last(cache_kv1[0]), state_win[0], page_table)
    y_s, kv0_s, kv1_s, win_s = _trunk(x_sample, past_len, caches, w, *norms)
    return (y_p, y_s, kv0_p, kv0_s, kv1_p, kv1_s, win_p, win_s)
```

```python
import functools

import numpy as np
import jax
import jax.numpy as jnp
from jax import lax
from jax.experimental import pallas as pl
from jax.experimental.pallas import tpu as pltpu

F32 = jnp.float32
BF16 = jnp.bfloat16

D_MODEL = 1024
HEAD_DIM = 64
N_HEADS = 16
H_SB = 8
PAGE_SIZE = 128
MOBA_BLOCK = 256
MOBA_TOPK = 3
NSA_HKV = 4
NSA_GROUP = 4
CMP_BLOCK = 32
CMP_STRIDE = 16
CMP_HIDDEN = 256
SLC_BLOCK = 64
SLC_TOPK = 16
WINDOW = 512
D_FF = 2816
ROPE_THETA = 10000.0
NORM_EPS = 1e-6
NEG_BIG = -1e30
FORCE_SCORE = 1e9
ATTN_SCALE = HEAD_DIM ** -0.5

LANES = 128
HEADS_PER_SLAB = LANES // HEAD_DIM
VMEM_LIMIT_BYTES = 56 * 1024 * 1024


def _params(*sem):
    return pltpu.CompilerParams(dimension_semantics=sem, vmem_limit_bytes=VMEM_LIMIT_BYTES)


def _iota(shape, dim):
    return lax.broadcasted_iota(jnp.int32, shape, dim)


def _dot(a, b):
    return jnp.dot(a, b, preferred_element_type=F32)


def _dot_nt(a, b):
    return lax.dot_general(a, b, (((1,), (1,)), ((), ())), preferred_element_type=F32)


def _split_bf16(x):
    hi = x.astype(BF16)
    lo = (x - hi.astype(F32)).astype(BF16)
    return hi, lo


def _dot_exact_rhs(x, rhs_bf16):
    hi, lo = _split_bf16(x)
    return _dot(hi, rhs_bf16) + _dot(lo, rhs_bf16)


def _log_sigmoid_pair(z):
    t = jnp.log(1.0 + jnp.exp(-jnp.abs(z)))
    return jnp.minimum(z, 0.0) - t, jnp.minimum(-z, 0.0) - t


def _later_matrix(n):
    return jnp.where(_iota((n, n), 0) > _iota((n, n), 1), 1.0, 0.0).astype(BF16)


def _osm_init(rows, width):
    return (jnp.full((rows, 1), NEG_BIG, F32), jnp.zeros((rows, 1), F32), jnp.zeros((rows, width), F32))


def _osm_update(state, s, mask, weigh):
    m, l, acc = state
    sm = jnp.where(mask, s, NEG_BIG)
    m_new = jnp.maximum(m, jnp.max(sm, axis=1, keepdims=True))
    p = jnp.where(mask, jnp.exp(sm - m_new), 0.0)
    a = jnp.exp(m - m_new)
    return (m_new, a * l + jnp.sum(p, axis=1, keepdims=True), a * acc + weigh(p.astype(BF16)))


def _osm_final(state):
    _, l, acc = state
    return acc / jnp.maximum(l, 1e-30)


def _osm_reset(m_ref, l_ref, acc_ref):
    m_ref[...] = jnp.full_like(m_ref, NEG_BIG)
    l_ref[...] = jnp.zeros_like(l_ref)
    acc_ref[...] = jnp.zeros_like(acc_ref)


def _osm_update_biased(state, s_biased, weigh):
    m, l, acc = state
    m_new = jnp.maximum(m, jnp.max(s_biased, axis=1, keepdims=True))
    p = jnp.exp(s_biased - m_new)
    a = jnp.exp(m - m_new)
    return (m_new, a * l + jnp.sum(p, axis=1, keepdims=True), a * acc + weigh(p.astype(BF16)))


def _top_rank(score, n_real, axis):
    j = _iota(score.shape, axis)
    rank = jnp.zeros(score.shape, jnp.int32)
    for m in range(n_real):
        other = score[:, m:m + 1] if axis == 1 else score[m:m + 1, :]
        ahead = (other > score) | ((other == score) & (j > m))
        rank = rank + jnp.where(ahead, 1, 0)
    return rank


def _norm_proj_post_kernel(x_ref, g_ref, w_ref, cos_ref, sin_ref, gain_ref, *refs, plan, n_out, t_plan):
    out_refs, t_refs, p_ref = refs[:n_out], refs[n_out:-1], refs[-1]
    x = x_ref[...]
    ms = jnp.mean(x * x, axis=-1, keepdims=True)
    p_ref[...] = _dot((x * lax.rsqrt(ms + NORM_EPS) * g_ref[...]).astype(BF16), w_ref[...])
    lane = _iota((1, LANES), 1)
    first_half = (lane & (HEAD_DIM - 1)) < HEAD_DIM // 2
    same_head = (_iota((LANES, LANES), 0) // HEAD_DIM) == (_iota((LANES, LANES), 1) // HEAD_DIM)
    head_mean = jnp.where(same_head, 1.0 / HEAD_DIM, 0.0).astype(BF16)
    cos = cos_ref[...]
    sin = sin_ref[...]
    for src, width, mode, gi, oi, dst in plan:
        if mode == "copy":
            out_refs[oi][:, dst:dst + width] = p_ref[:, src:src + width]
        elif mode == "sigmoid":
            x = p_ref[:, src:src + width]
            out_refs[oi][:, dst:dst + width] = 1.0 / (1.0 + jnp.exp(-x))
        else:
            for s in range(width // LANES):
                x = p_ref[:, src + s * LANES:src + (s + 1) * LANES]
                ms = _dot_exact_rhs(x * x, head_mean)
                y = x * lax.rsqrt(ms + NORM_EPS) * gain_ref[gi:gi + 1, :]
                other = jnp.where(first_half, pltpu.roll(y, LANES - HEAD_DIM // 2, 1),
                                  pltpu.roll(y, HEAD_DIM // 2, 1))
                out_refs[oi][:, dst + s * LANES:dst + (s + 1) * LANES] = y * cos + other * sin
    for t_ref, (oi, lead) in zip(t_refs, t_plan):
        t_ref[0] = out_refs[oi][...].T.reshape(lead + (HEAD_DIM, x_ref.shape[0]))


def norm_proj_post(x, g, w_bf16, cos, sin, gains, plan, out_widths, tm, t_plan=(), seq_len=None):
    m, k = x.shape
    n = w_bf16.shape[1]
    period = cos.shape[0] // tm
    tiles_per_seq = seq_len // tm if t_plan else 1
    t_specs = [pl.BlockSpec((1,) + lead + (HEAD_DIM, tm), lambda i: (i // tiles_per_seq, 0, 0, 0, i % tiles_per_seq))
               for _, lead in t_plan]
    t_shapes = [jax.ShapeDtypeStruct((m // seq_len,) + lead + (HEAD_DIM, seq_len), F32) for _, lead in t_plan]
    return pl.pallas_call(
        functools.partial(_norm_proj_post_kernel, plan=plan, n_out=len(out_widths), t_plan=t_plan),
        grid=(m // tm,),
        in_specs=[pl.BlockSpec((tm, k), lambda i: (i, 0)),
                  pl.BlockSpec((1, k), lambda i: (0, 0)),
                  pl.BlockSpec((k, n), lambda i: (0, 0)),
                  pl.BlockSpec((tm, LANES), lambda i: (i % period, 0)),
                  pl.BlockSpec((tm, LANES), lambda i: (i % period, 0)),
                  pl.BlockSpec(gains.shape, lambda i: (0, 0))],
        out_specs=[pl.BlockSpec((tm, w), lambda i: (i, 0)) for w in out_widths] + t_specs,
        out_shape=[jax.ShapeDtypeStruct((m, w), F32) for w in out_widths] + t_shapes,
        scratch_shapes=[pltpu.VMEM((tm, n), F32)],
        compiler_params=_params("parallel"),
        name="norm_proj_post",
    )(x, g.reshape(1, k), w_bf16, cos, sin, gains)


def _mixer_out_ffn_kernel(*refs, n_in):
    a_refs, wo_refs = refs[:n_in], refs[n_in:2 * n_in]
    h_ref, g_ref, wg_ref, wu_ref, wd_ref, o_ref, xn_ref, acc_ref = refs[2 * n_in:]
    j = pl.program_id(1)

    @pl.when(j == 0)
    def _():
        x = h_ref[...]
        for a_ref, wo_ref in zip(a_refs, wo_refs):
            x = x + _dot(a_ref[...].astype(BF16), wo_ref[...])
        ms = jnp.mean(x * x, axis=-1, keepdims=True)
        xn_ref[...] = (x * lax.rsqrt(ms + NORM_EPS) * g_ref[...]).astype(BF16)
        acc_ref[...] = x

    xn = xn_ref[...]
    gate = _dot(xn, wg_ref[...])
    up = _dot(xn, wu_ref[...])
    act = (gate / (1.0 + jnp.exp(-gate))) * up
    acc_ref[...] += _dot(act.astype(BF16), wd_ref[...])

    @pl.when(j == pl.num_programs(1) - 1)
    def _():
        o_ref[...] = acc_ref[...]


def mixer_out_ffn(a_list, wo_list, res, g, w_gu_bf16, w_down_bf16, tm, tf):
    m, k = res.shape
    n_chunks = D_FF // tf
    return pl.pallas_call(
        functools.partial(_mixer_out_ffn_kernel, n_in=len(a_list)),
        grid=(m // tm, n_chunks),
        in_specs=[pl.BlockSpec((tm, a.shape[1]), lambda i, j: (i, 0)) for a in a_list]
        + [pl.BlockSpec(wo.shape, lambda i, j: (0, 0)) for wo in wo_list]
        + [pl.BlockSpec((tm, k), lambda i, j: (i, 0)),
           pl.BlockSpec((1, k), lambda i, j: (0, 0)),
           pl.BlockSpec((k, tf), lambda i, j: (0, j)),
           pl.BlockSpec((k, tf), lambda i, j: (0, j + n_chunks)),
           pl.BlockSpec((tf, k), lambda i, j: (j, 0))],
        out_specs=pl.BlockSpec((tm, k), lambda i, j: (i, 0)),
        out_shape=jax.ShapeDtypeStruct((m, k), F32),
        scratch_shapes=[pltpu.VMEM((tm, k), BF16), pltpu.VMEM((tm, k), F32)],
        compiler_params=_params("parallel", "arbitrary"),
        name="mixer_out_ffn",
    )(*a_list, *wo_list, res, g.reshape(1, k), w_gu_bf16, w_gu_bf16, w_down_bf16)


ATTN_TQ = 256
ATTN_TK = 256


def _stage_keys_values(k_ref, k_cols, v_ref, v_cols, kb_ref, vt_ref, n_tiles, tk):
    kb_ref[...] = k_ref[0, :, k_cols].astype(BF16)
    for c in range(n_tiles):
        vt_ref[c] = v_ref[0, c * tk:(c + 1) * tk, v_cols].T.astype(BF16)


def _split_heads_t(qt, scale):
    row_head = _iota((LANES, 1), 0) // HEAD_DIM
    return [jnp.where(row_head == h, qt * scale, 0.0) for h in range(HEADS_PER_SLAB)]


def _osm_t_init(width, cols):
    return (jnp.full((1, cols), NEG_BIG, F32), jnp.zeros((1, cols), F32), jnp.zeros((width, cols), F32))


def _osm_t_update(state, s_biased_t, weigh):
    m, l, acc = state
    m_new = jnp.maximum(m, jnp.max(s_biased_t, axis=0, keepdims=True))
    p = jnp.exp(s_biased_t - m_new)
    a = jnp.exp(m - m_new)
    return (m_new, a * l + jnp.sum(p, axis=0, keepdims=True), a * acc + weigh(p.astype(BF16)))


def _osm_t_update2(state, s_a, s_b, weigh_a, weigh_b):
    m, l, acc = state
    m_new = jnp.maximum(m, jnp.maximum(jnp.max(s_a, axis=0, keepdims=True), jnp.max(s_b, axis=0, keepdims=True)))
    p_a = jnp.exp(s_a - m_new)
    p_b = jnp.exp(s_b - m_new)
    a = jnp.exp(m - m_new)
    l_new = a * l + jnp.sum(p_a, axis=0, keepdims=True) + jnp.sum(p_b, axis=0, keepdims=True)
    return (m_new, l_new, a * acc + weigh_a(p_a.astype(BF16)) + weigh_b(p_b.astype(BF16)))


def _osm_t_final(state):
    _, l, acc = state
    return acc / jnp.maximum(l, 1e-30)


def _weigh_heads_t(vt_ref, tile, w, tq):
    return jnp.concatenate([_dot(vt_ref[tile, h * HEAD_DIM:(h + 1) * HEAD_DIM, :], w[:, h * tq:(h + 1) * tq])
                            for h in range(HEADS_PER_SLAB)], axis=1)


def _unstack_heads_t(x, tq):
    return jnp.concatenate([x[:, h * tq:(h + 1) * tq] for h in range(HEADS_PER_SLAB)], axis=0)


def _sb_prefill_kernel(q_ref, k_ref, v_ref, o_ref, kb_ref, vt_ref, *, n_tiles):
    qi = pl.program_id(2)
    tq, tk = ATTN_TQ, ATTN_TK

    @pl.when(qi == 0)
    def _():
        _stage_keys_values(k_ref, slice(None), v_ref, slice(None), kb_ref, vt_ref, n_tiles, tk)

    qpos = jnp.concatenate([qi * tq + _iota((1, tq), 1)] * HEADS_PER_SLAB, axis=1)
    sooner = jnp.where(_iota((tk, tk), 0) < _iota((tk, tk), 1), 1.0, 0.0).astype(BF16)
    qs = jnp.concatenate(_split_heads_t(q_ref[0].T, ATTN_SCALE), axis=1).astype(BF16)

    def log_terms(kb):
        z = _dot(kb_ref[pl.ds(pl.multiple_of(kb * tk, tk), tk), :], qs)
        return _log_sigmoid_pair(z)

    def gap_within(lk):
        hi, lo = _split_bf16(lk)
        return _dot(sooner, hi) + _dot(sooner, lo)

    def diagonal_tile():
        past = (qi * tk + _iota((tk, 1), 0)) < qpos
        ls, lk = log_terms(qi)
        lk = jnp.where(past, lk, 0.0)
        w = jnp.where(past, jnp.exp(ls + gap_within(lk)), 0.0)
        return jnp.sum(lk, axis=0, keepdims=True), _weigh_heads_t(vt_ref, qi, w.astype(BF16), tq)

    def pair(i, carry):
        c, acc = carry
        ka = qi - 1 - 2 * i
        kb = jnp.maximum(ka - 1, 0)
        there = jnp.where(ka >= 1, 1.0, 0.0)
        ls_a, lk_a = log_terms(ka)
        ls_b, lk_b = log_terms(kb)
        lk_b = lk_b * there
        sum_a = jnp.sum(lk_a, axis=0, keepdims=True)
        w_a = jnp.exp(ls_a + gap_within(lk_a) + c)
        w_b = jnp.exp(ls_b + gap_within(lk_b) + (c + sum_a)) * there
        acc = acc + _weigh_heads_t(vt_ref, ka, w_a.astype(BF16), tq) + _weigh_heads_t(vt_ref, kb, w_b.astype(BF16), tq)
        return c + sum_a + jnp.sum(lk_b, axis=0, keepdims=True), acc

    _, acc = lax.fori_loop(0, (qi + 1) // 2, pair, diagonal_tile())
    o_ref[0] = _unstack_heads_t(acc, tq).T


def sb_prefill(q, kv):
    b, t, _ = q.shape
    n_slab = H_SB // HEADS_PER_SLAB
    v_off = N_HEADS // HEADS_PER_SLAB
    assert t % ATTN_TK == 0 and ATTN_TQ == ATTN_TK
    return pl.pallas_call(
        functools.partial(_sb_prefill_kernel, n_tiles=t // ATTN_TK),
        grid=(b, n_slab, t // ATTN_TQ),
        in_specs=[pl.BlockSpec((1, ATTN_TQ, LANES), lambda bi, s, qi: (bi, qi, s)),
                  pl.BlockSpec((1, t, LANES), lambda bi, s, qi: (bi, 0, s)),
                  pl.BlockSpec((1, t, LANES), lambda bi, s, qi: (bi, 0, v_off + s))],
        out_specs=pl.BlockSpec((1, ATTN_TQ, LANES), lambda bi, s, qi: (bi, qi, s)),
        out_shape=jax.ShapeDtypeStruct((b, t, H_SB * HEAD_DIM), F32),
        scratch_shapes=[pltpu.VMEM((t, LANES), BF16), pltpu.VMEM((t // ATTN_TK, LANES, ATTN_TK), BF16)],
        compiler_params=_params("parallel", "parallel", "arbitrary"),
        name="sb_prefill",
    )(q, kv, kv)


def _moba_prefill_kernel(q_ref, k_ref, v_ref, o_ref, kmean_ref, kb_ref, vt_ref, *, n_blk):
    qi = pl.program_id(2)
    tq, tk = ATTN_TQ, ATTN_TK
    t_total = n_blk * MOBA_BLOCK
    blk_rows = -(-n_blk // 8) * 8

    @pl.when(qi == 0)
    def _():
        kmean_ref[...] = jnp.zeros_like(kmean_ref)
        for n in range(n_blk):
            blk = k_ref[0, n * MOBA_BLOCK:(n + 1) * MOBA_BLOCK, :]
            kmean_ref[n:n + 1, :] = jnp.sum(blk, axis=0, keepdims=True) * (1.0 / MOBA_BLOCK)
        _stage_keys_values(k_ref, slice(None), v_ref, slice(None), kb_ref, vt_ref, n_blk, tk)

    cols = HEADS_PER_SLAB * tq
    qpos = jnp.concatenate([qi * tq + _iota((1, tq), 1)] * HEADS_PER_SLAB, axis=1)
    blk_id = _iota((blk_rows, cols), 0)
    k_hi, k_lo = _split_bf16(kmean_ref[0:blk_rows, :])
    qf = jnp.concatenate(_split_heads_t(q_ref[0].T, 1.0), axis=1)
    qs = (qf * ATTN_SCALE).astype(BF16)
    q_hi, q_lo = _split_bf16(qf)
    gate = _dot(k_hi, q_hi) + _dot(k_lo, q_hi) + _dot(k_hi, q_lo)
    fully_past = (blk_id < qi) & (blk_id < n_blk)
    gate = jnp.where(fully_past, gate, -jnp.inf)
    sel = jnp.where(fully_past & (_top_rank(gate, n_blk, 0) < MOBA_TOPK), 1.0, 0.0)

    def scores(kb, limit):
        bias = jnp.where((kb * tk + _iota((tk, 1), 0)) <= limit, 0.0, NEG_BIG)
        return _dot(kb_ref[pl.ds(pl.multiple_of(kb * tk, tk), tk), :], qs) + bias

    def past_limit(kb, valid):
        picked = jnp.sum(jnp.where(blk_id == kb, sel, 0.0), axis=0, keepdims=True) > 0.5
        return jnp.where(picked, jnp.where(valid, t_total, -1), -1)

    def body(i, st):
        ka = qi - 2 * i
        kb = ka - 1
        kb_safe = jnp.maximum(kb, 0)
        s_a = scores(ka, jnp.where(ka == qi, qpos, past_limit(ka, True)))
        s_b = scores(kb_safe, past_limit(kb_safe, kb >= 0))
        return _osm_t_update2(st, s_a, s_b, lambda p: _weigh_heads_t(vt_ref, ka, p, tq),
                              lambda p: _weigh_heads_t(vt_ref, kb_safe, p, tq))

    out = _osm_t_final(lax.fori_loop(0, (qi + 2) // 2, body, _osm_t_init(HEAD_DIM, cols)))
    o_ref[0] = _unstack_heads_t(out, tq).T


def moba_prefill(q, kv):
    b, t, _ = q.shape
    n_slab = (N_HEADS - H_SB) // HEADS_PER_SLAB
    q_off = H_SB // HEADS_PER_SLAB
    v_off = N_HEADS // HEADS_PER_SLAB
    assert ATTN_TQ == MOBA_BLOCK and t % MOBA_BLOCK == 0 and t // MOBA_BLOCK <= LANES
    return pl.pallas_call(
        functools.partial(_moba_prefill_kernel, n_blk=t // MOBA_BLOCK),
        grid=(b, n_slab, t // ATTN_TQ),
        in_specs=[pl.BlockSpec((1, ATTN_TQ, LANES), lambda bi, s, qi: (bi, qi, q_off + s)),
                  pl.BlockSpec((1, t, LANES), lambda bi, s, qi: (bi, 0, q_off + s)),
                  pl.BlockSpec((1, t, LANES), lambda bi, s, qi: (bi, 0, v_off + q_off + s))],
        out_specs=pl.BlockSpec((1, ATTN_TQ, LANES), lambda bi, s, qi: (bi, qi, s)),
        out_shape=jax.ShapeDtypeStruct((b, t, (N_HEADS - H_SB) * HEAD_DIM), F32),
        scratch_shapes=[pltpu.VMEM((LANES, LANES), F32), pltpu.VMEM((t, LANES), BF16),
                        pltpu.VMEM((t // ATTN_TK, LANES, ATTN_TK), BF16)],
        compiler_params=_params("parallel", "parallel", "arbitrary"),
        name="moba_prefill",
    )(q, kv, kv)


CHUNKS_PER_PAGE = PAGE_SIZE // CMP_STRIDE
CMP_HALF = CMP_STRIDE * HEAD_DIM
PAGES_PER_STEP = 8


def _page_specs(block, index_of_page):
    return [pl.BlockSpec(block, functools.partial(index_of_page, g=g)) for g in range(PAGES_PER_STEP)]


def _compress_kernel(pt_ref, *refs, n_steps):
    page_refs = refs[:PAGES_PER_STEP]
    pe_ref, w1_ref, w2_ref, kc_ref, vc_ref, x_ref = refs[PAGES_PER_STEP:]
    p = pl.program_id(1)
    lane = _iota((1, LANES), 1)
    low = lane < HEAD_DIM
    r = _iota((PAGE_SIZE, PAGE_SIZE), 0)
    token = _iota((PAGE_SIZE, PAGE_SIZE), 1)
    perm = jnp.where(token == CMP_STRIDE * (r % CHUNKS_PER_PAGE) + r // CHUNKS_PER_PAGE, 1.0, 0.0).astype(BF16)
    for g, page_ref in enumerate(page_refs):
        rows = pl.ds(pl.multiple_of((p * PAGES_PER_STEP + g) * CHUNKS_PER_PAGE, CHUNKS_PER_PAGE), CHUNKS_PER_PAGE)
        for kv in range(2):
            for s in range(NSA_HKV // HEADS_PER_SLAB):
                two_heads = page_ref[0, kv, HEADS_PER_SLAB * s:HEADS_PER_SLAB * (s + 1)].reshape(LANES, PAGE_SIZE)
                hi, lo = _split_bf16(two_heads)
                slab = _dot_nt(perm, hi) + _dot_nt(perm, lo)
                for pp in range(CMP_STRIDE // 2):
                    even = slab[2 * pp * CHUNKS_PER_PAGE:(2 * pp + 1) * CHUNKS_PER_PAGE]
                    odd = slab[(2 * pp + 1) * CHUNKS_PER_PAGE:(2 * pp + 2) * CHUNKS_PER_PAGE]
                    head0 = jnp.where(low, even, pltpu.roll(odd, HEAD_DIM, 1))
                    head1 = jnp.where(low, pltpu.roll(even, HEAD_DIM, 1), odd)
                    x_ref[kv * NSA_HKV + 2 * s, rows, pp * LANES:(pp + 1) * LANES] = head0
                    x_ref[kv * NSA_HKV + 2 * s + 1, rows, pp * LANES:(pp + 1) * LANES] = head1

    @pl.when(p == n_steps - 1)
    def _():
        n_chunk = n_steps * PAGES_PER_STEP * CHUNKS_PER_PAGE
        for kv, out_ref in ((0, kc_ref), (1, vc_ref)):
            pe_a = pe_ref[kv, 0:1, :]
            pe_b = pe_ref[kv, 1:2, :]
            w1a = w1_ref[kv, 0:CMP_HALF, :]
            w1b = w1_ref[kv, CMP_HALF:2 * CMP_HALF, :]
            w2 = w2_ref[kv]
            for h in range(NSA_HKV):
                x = x_ref[kv * NSA_HKV + h]
                first = _dot((x + pe_a).astype(BF16), w1a)
                second = _dot((x + pe_b).astype(BF16), w1b)
                hid = first + pltpu.roll(second, n_chunk - 1, 0)
                hid = hid / (1.0 + jnp.exp(-hid))
                out_ref[0, :, h * HEAD_DIM:(h + 1) * HEAD_DIM] = _dot(hid.astype(BF16), w2)


def compress_pages(pages, page_table, pe, w1_bf16, w2_bf16):
    if page_table is None:
        b, n_pages = pages.shape[0], pages.shape[-1] // PAGE_SIZE
        page_table = jnp.zeros((1, 1), jnp.int32)

        def index_of_page(bi, p, pt, g):
            return (bi, 0, 0, 0, p * PAGES_PER_STEP + g)
    else:
        b, n_pages = page_table.shape

        def index_of_page(bi, p, pt, g):
            return (pt[bi, p * PAGES_PER_STEP + g], 0, 0, 0, 0)
    assert n_pages % PAGES_PER_STEP == 0
    n_chunk = n_pages * CHUNKS_PER_PAGE
    width = NSA_HKV * HEAD_DIM
    grid_spec = pltpu.PrefetchScalarGridSpec(
        num_scalar_prefetch=1,
        grid=(b, n_pages // PAGES_PER_STEP),
        in_specs=_page_specs((1, 2, NSA_HKV, HEAD_DIM, PAGE_SIZE), index_of_page) + [
                  pl.BlockSpec(pe.shape, lambda bi, p, pt: (0, 0, 0)),
                  pl.BlockSpec(w1_bf16.shape, lambda bi, p, pt: (0, 0, 0)),
                  pl.BlockSpec(w2_bf16.shape, lambda bi, p, pt: (0, 0, 0))],
        out_specs=[pl.BlockSpec((1, n_chunk, width), lambda bi, p, pt: (bi, 0, 0)),
                   pl.BlockSpec((1, n_chunk, width), lambda bi, p, pt: (bi, 0, 0))],
        scratch_shapes=[pltpu.VMEM((2 * NSA_HKV, n_chunk, CMP_HALF), F32)],
    )
    return pl.pallas_call(
        functools.partial(_compress_kernel, n_steps=n_pages // PAGES_PER_STEP),
        grid_spec=grid_spec,
        out_shape=[jax.ShapeDtypeStruct((b, n_chunk, width), F32)] * 2,
        compiler_params=_params("parallel", "arbitrary"),
        name="nsa_compress",
    )(page_table, *([pages] * PAGES_PER_STEP), pe, w1_bf16, w2_bf16)


NSA_TQ = 256
SLC_TK = 256
WIN_TK = 128


def _cover_matrix(n_cmp_rows, n_cols, cmp_axis):
    i = _iota((n_cmp_rows, n_cols) if cmp_axis == 0 else (n_cols, n_cmp_rows), cmp_axis)
    j = _iota((n_cmp_rows, n_cols) if cmp_axis == 0 else (n_cols, n_cmp_rows), 1 - cmp_axis)
    ratio = SLC_BLOCK // CMP_STRIDE
    reach = CMP_BLOCK // CMP_STRIDE - 1
    return jnp.where((i >= ratio * j - reach) & (i <= ratio * j + ratio - 1), 1.0, 0.0).astype(BF16)


def _slc_score(imp, cur, n_slc, axis):
    j = _iota(imp.shape, axis)
    forced = (j == 0) | (j == cur) | (j == cur - 1)
    score = jnp.where(forced, FORCE_SCORE, jnp.where(j > cur, -FORCE_SCORE, imp))
    return jnp.where(j < n_slc, score, -jnp.inf)


def _nsa_prefill_kernel(q_ref, slc_ref, win_ref, kc_ref, vc_ref, g_ref, o_ref, ks_ref, vst_ref, kw_ref, vwt_ref, *, t):
    qi = pl.program_id(1)
    tq = NSA_TQ
    n_slc = t // SLC_BLOCK
    top = min(SLC_TOPK, n_slc)
    n_cmp_rows = t // CMP_STRIDE
    n_slab = NSA_HKV // HEADS_PER_SLAB

    @pl.when(qi == 0)
    def _():
        for s in range(n_slab):
            v_cols = slice(KV_WIDTH + s * LANES, KV_WIDTH + (s + 1) * LANES)
            k_cols = slice(s * LANES, (s + 1) * LANES)
            _stage_keys_values(slc_ref, k_cols, slc_ref, v_cols, ks_ref.at[s], vst_ref.at[s], t // SLC_TK, SLC_TK)
            _stage_keys_values(win_ref, k_cols, win_ref, v_cols, kw_ref.at[s], vwt_ref.at[s], t // WIN_TK, WIN_TK)

    qt = q_ref[0].T * ATTN_SCALE
    gates_t = g_ref[0].T
    qpos = qi * tq + _iota((1, tq), 1)
    qpos4 = jnp.concatenate([qpos] * NSA_GROUP, axis=1)
    cover_t = _cover_matrix(n_cmp_rows, LANES, 1)
    zeros_head = jnp.zeros((HEAD_DIM, tq), F32)
    outs = []
    for hk in range(NSA_HKV):
        half, slab = hk % HEADS_PER_SLAB, hk // HEADS_PER_SLAB
        cols = slice(slab * LANES, (slab + 1) * LANES)
        head_rows = slice(half * HEAD_DIM, (half + 1) * HEAD_DIM)
        parts = []
        for g in range(NSA_GROUP):
            h = hk * NSA_GROUP + g
            qg = qt[h * HEAD_DIM:(h + 1) * HEAD_DIM]
            parts.append(jnp.concatenate([qg, zeros_head] if half == 0 else [zeros_head, qg], axis=0))
        qs = jnp.concatenate(parts, axis=1).astype(BF16)

        s_c = _dot(kc_ref[0, :, cols].astype(BF16), qs)
        ok_c = (CMP_STRIDE * _iota((n_cmp_rows, 1), 0) + CMP_BLOCK - 1) <= qpos4
        sm = jnp.where(ok_c, s_c, NEG_BIG)
        pc = jnp.where(ok_c, jnp.exp(sm - jnp.max(sm, axis=0, keepdims=True)), 0.0)
        pc = pc / jnp.maximum(jnp.sum(pc, axis=0, keepdims=True), 1e-30)
        o_c = _dot(vc_ref[0, :, cols].T[head_rows].astype(BF16), pc.astype(BF16))

        psum = pc[:, 0:tq] + pc[:, tq:2 * tq] + pc[:, 2 * tq:3 * tq] + pc[:, 3 * tq:4 * tq]
        p_hi, p_lo = _split_bf16(psum)
        imp_t = (_dot(cover_t, p_hi) + _dot(cover_t, p_lo))[0:n_slc]
        score_t = _slc_score(imp_t, qpos // SLC_BLOCK, n_slc, 0)
        sel_t = jnp.where(_top_rank(score_t, n_slc, 0) < top, 1.0, 0.0)
        sel = jnp.concatenate([sel_t, jnp.zeros((LANES - n_slc, tq), F32)], axis=0).astype(BF16)

        n_slc_tiles = (qi * tq + tq + SLC_TK - 1) // SLC_TK

        def slc_scores(kt, valid, on_diagonal, qs=qs, sel=sel, slab=slab):
            blk_of_key = kt * (SLC_TK // SLC_BLOCK) + _iota((SLC_TK, LANES), 0) // SLC_BLOCK
            expand = jnp.where(_iota((SLC_TK, LANES), 1) == blk_of_key, jnp.where(valid, 1.0, 0.0), 0.0).astype(BF16)
            bias = _dot(expand, sel) * (-NEG_BIG) + NEG_BIG
            if on_diagonal:
                bias = jnp.where(kt * SLC_TK + _iota((SLC_TK, 1), 0) <= qpos, bias, NEG_BIG)
            s = _dot(ks_ref[slab, pl.ds(pl.multiple_of(kt * SLC_TK, SLC_TK), SLC_TK), :], qs)
            return s + jnp.concatenate([bias] * NSA_GROUP, axis=1)

        def slc_pair(st, ka, on_diagonal, slc_scores=slc_scores, slab=slab, head_rows=head_rows):
            kb = jnp.maximum(ka - 1, 0)
            return _osm_t_update2(st, slc_scores(ka, True, on_diagonal), slc_scores(kb, ka >= 1, False),
                                  lambda p: _dot(vst_ref[slab, ka, head_rows, :], p),
                                  lambda p: _dot(vst_ref[slab, kb, head_rows, :], p))

        st = slc_pair(_osm_t_init(HEAD_DIM, NSA_GROUP * tq), n_slc_tiles - 1, True)
        o_s = _osm_t_final(lax.fori_loop(1, (n_slc_tiles + 1) // 2,
                                         lambda i, st, slc_pair=slc_pair: slc_pair(st, n_slc_tiles - 1 - 2 * i, False), st))

        first_tile = jnp.maximum(qi * (tq // WIN_TK) - WINDOW // WIN_TK, 0)
        last_tile = qi * (tq // WIN_TK) + tq // WIN_TK - 1

        def win_scores(kt, valid, qs=qs, slab=slab):
            dist = lax.bitcast_convert_type(qpos - (kt * WIN_TK + _iota((WIN_TK, 1), 0)), jnp.uint32)
            bias = jnp.where(dist < WINDOW, jnp.where(valid, 0.0, NEG_BIG), NEG_BIG)
            s = _dot(kw_ref[slab, pl.ds(pl.multiple_of(kt * WIN_TK, WIN_TK), WIN_TK), :], qs)
            return s + jnp.concatenate([bias] * NSA_GROUP, axis=1)

        def win_body(i, st, win_scores=win_scores, slab=slab, head_rows=head_rows):
            ka = last_tile - 2 * i
            kb = jnp.maximum(ka - 1, first_tile)
            return _osm_t_update2(st, win_scores(ka, True), win_scores(kb, ka - 1 >= first_tile),
                                  lambda p: _dot(vwt_ref[slab, ka, head_rows, :], p),
                                  lambda p: _dot(vwt_ref[slab, kb, head_rows, :], p))

        o_w = _osm_t_final(lax.fori_loop(0, (last_tile - first_tile + 2) // 2, win_body,
                                         _osm_t_init(HEAD_DIM, NSA_GROUP * tq)))

        for g in range(NSA_GROUP):
            cs = slice(g * tq, (g + 1) * tq)
            gc = hk * LANES + g * 3
            outs.append(gates_t[gc:gc + 1] * o_c[:, cs] + gates_t[gc + 1:gc + 2] * o_s[:, cs]
                        + gates_t[gc + 2:gc + 3] * o_w[:, cs])
    o_ref[0] = jnp.concatenate(outs, axis=0).T


def nsa_prefill(q, kv, win, kc, vc, gates):
    b, t, _ = q.shape
    width = NSA_HKV * HEAD_DIM
    n_chunk = t // CMP_STRIDE
    n_slc = t // SLC_BLOCK
    assert t % SLC_TK == 0 and n_chunk % LANES == 0 and n_slc <= LANES and n_slc % 8 == 0
    assert NSA_TQ % WIN_TK == 0 and NSA_TQ % SLC_TK == 0
    n_slab = NSA_HKV // HEADS_PER_SLAB
    return pl.pallas_call(
        functools.partial(_nsa_prefill_kernel, t=t),
        grid=(b, t // NSA_TQ),
        in_specs=[pl.BlockSpec((1, NSA_TQ, N_HEADS * HEAD_DIM), lambda bi, qi: (bi, qi, 0)),
                  pl.BlockSpec((1, t, 2 * width), lambda bi, qi: (bi, 0, 1)),
                  pl.BlockSpec((1, t, 2 * width), lambda bi, qi: (bi, 0, 0)),
                  pl.BlockSpec((1, n_chunk, width), lambda bi, qi: (bi, 0, 0)),
                  pl.BlockSpec((1, n_chunk, width), lambda bi, qi: (bi, 0, 0)),
                  pl.BlockSpec((1, NSA_TQ, NSA_HKV * LANES), lambda bi, qi: (bi, qi, 0))],
        out_specs=pl.BlockSpec((1, NSA_TQ, N_HEADS * HEAD_DIM), lambda bi, qi: (bi, qi, 0)),
        out_shape=jax.ShapeDtypeStruct((b, t, N_HEADS * HEAD_DIM), F32),
        scratch_shapes=[pltpu.VMEM((n_slab, t, LANES), BF16), pltpu.VMEM((n_slab, t // SLC_TK, LANES, SLC_TK), BF16),
                        pltpu.VMEM((n_slab, t, LANES), BF16), pltpu.VMEM((n_slab, t // WIN_TK, LANES, WIN_TK), BF16)],
        compiler_params=_params("parallel", "arbitrary"),
        name="nsa_prefill",
    )(q, kv, win, kc, vc, gates)


SB_WIDTH = H_SB * HEAD_DIM
KV0_V_OFF = N_HEADS * HEAD_DIM
PAGES_PER_MOBA_BLOCK = MOBA_BLOCK // PAGE_SIZE


def _load_new_rows(dst_ref, new_ref, col, width, dec_seq):
    dst_ref[...] = jnp.zeros_like(dst_ref)
    dst_ref[0:dec_seq, :] = new_ref[0, :, col:col + width]


def _heads_by_dim(page_tile):
    return page_tile.reshape(page_tile.shape[0] * HEAD_DIM, page_tile.shape[2])


def _fold_heads(res, dec_seq, n_heads):
    lane_head = _iota((1, n_heads * HEAD_DIM), 1) // HEAD_DIM
    out = jnp.zeros((dec_seq, n_heads * HEAD_DIM), F32)
    for h in range(n_heads):
        out = out + jnp.where(lane_head == h, res[h * dec_seq:(h + 1) * dec_seq, :], 0.0)
    return out


def _sb_decode_kernel(pt_ref, qbd_ref, new_ref, *refs, n_steps, dec_seq):
    k_refs = refs[:PAGES_PER_STEP]
    v_refs = refs[PAGES_PER_STEP:2 * PAGES_PER_STEP]
    o_ref, kmean_ref, c_ref, acc_ref, kpad_ref, vpad_ref = refs[2 * PAGES_PER_STEP:]
    p = pl.program_id(1)
    qbd = (qbd_ref[0] * ATTN_SCALE).astype(BF16)
    rows = qbd.shape[0]
    later = _later_matrix(PAGE_SIZE)
    n_blk = n_steps * PAGES_PER_STEP // PAGES_PER_MOBA_BLOCK

    @pl.when(p == 0)
    def _():
        kmean_ref[...] = jnp.zeros_like(kmean_ref)
        _load_new_rows(kpad_ref, new_ref, 0, SB_WIDTH, dec_seq)
        _load_new_rows(vpad_ref, new_ref, KV0_V_OFF, SB_WIDTH, dec_seq)
        valid = _iota((1, PAGE_SIZE), 1) < lax.rem(_iota((rows, 1), 0), dec_seq)
        ls, lk = _log_sigmoid_pair(_dot_nt(qbd, kpad_ref[...].astype(BF16)))
        lk = jnp.where(valid, lk, 0.0)
        w = jnp.where(valid, jnp.exp(ls + _dot_exact_rhs(lk, later)), 0.0)
        acc_ref[...] = _dot(w.astype(BF16), vpad_ref[...].astype(BF16))
        c_ref[...] = jnp.sum(lk, axis=1, keepdims=True)

    zs = [_dot(qbd, _heads_by_dim(k_ref[0, 0, 0:H_SB]).astype(BF16)) for k_ref in k_refs]
    c = c_ref[...]
    acc = acc_ref[...]
    for z, v_ref in zip(zs, v_refs):
        ls, lk = _log_sigmoid_pair(z)
        w = jnp.exp(ls + _dot_exact_rhs(lk, later) + c)
        acc = acc + _dot_nt(w.astype(BF16), _heads_by_dim(v_ref[0, 0]).astype(BF16))
        c = c + jnp.sum(lk, axis=1, keepdims=True)
    c_ref[...] = c
    acc_ref[...] = acc

    lane = _iota((1, LANES), 1)
    kmean = kmean_ref[0]
    for j in range(PAGES_PER_STEP // PAGES_PER_MOBA_BLOCK):
        total = jnp.zeros(((N_HEADS - H_SB) * HEAD_DIM, 1), F32)
        for k_ref in k_refs[j * PAGES_PER_MOBA_BLOCK:(j + 1) * PAGES_PER_MOBA_BLOCK]:
            total = total + jnp.sum(_heads_by_dim(k_ref[0, 0, H_SB:N_HEADS]), axis=1, keepdims=True)
        blk = n_blk - 1 - (p * (PAGES_PER_STEP // PAGES_PER_MOBA_BLOCK) + j)
        kmean = jnp.where(lane == blk, total * (1.0 / MOBA_BLOCK), kmean)
    kmean_ref[0] = kmean

    @pl.when(p == n_steps - 1)
    def _():
        o_ref[0] = _fold_heads(acc_ref[...], dec_seq, H_SB)


def sb_decode(qbd, new_kv, pool, page_table):
    b, n_pages = page_table.shape
    dec_seq = new_kv.shape[1]
    rows = qbd.shape[1]
    assert n_pages % PAGES_PER_STEP == 0 and PAGES_PER_STEP % PAGES_PER_MOBA_BLOCK == 0
    assert n_pages // PAGES_PER_MOBA_BLOCK <= LANES

    def page_of(bi, p, pt, g):
        return pt[bi, n_pages - 1 - (p * PAGES_PER_STEP + g)]

    grid_spec = pltpu.PrefetchScalarGridSpec(
        num_scalar_prefetch=1,
        grid=(b, n_pages // PAGES_PER_STEP),
        in_specs=[pl.BlockSpec((1, rows, SB_WIDTH), lambda bi, p, pt: (bi, 0, 0)),
                  pl.BlockSpec((1, dec_seq, new_kv.shape[2]), lambda bi, p, pt: (bi, 0, 0))]
        + _page_specs((1, 1, N_HEADS, HEAD_DIM, PAGE_SIZE), lambda bi, p, pt, g: (page_of(bi, p, pt, g), 0, 0, 0, 0))
        + _page_specs((1, 1, H_SB, HEAD_DIM, PAGE_SIZE), lambda bi, p, pt, g: (page_of(bi, p, pt, g), 1, 0, 0, 0)),
        out_specs=[pl.BlockSpec((1, dec_seq, SB_WIDTH), lambda bi, p, pt: (bi, 0, 0)),
                   pl.BlockSpec((1, SB_WIDTH, LANES), lambda bi, p, pt: (bi, 0, 0))],
        scratch_shapes=[pltpu.VMEM((rows, 1), F32), pltpu.VMEM((rows, SB_WIDTH), F32),
                        pltpu.VMEM((PAGE_SIZE, SB_WIDTH), F32), pltpu.VMEM((PAGE_SIZE, SB_WIDTH), F32)],
    )
    return pl.pallas_call(
        functools.partial(_sb_decode_kernel, n_steps=n_pages // PAGES_PER_STEP, dec_seq=dec_seq),
        grid_spec=grid_spec,
        out_shape=[jax.ShapeDtypeStruct((b, dec_seq, SB_WIDTH), F32),
                   jax.ShapeDtypeStruct((b, SB_WIDTH, LANES), F32)],
        compiler_params=_params("parallel", "arbitrary"),
        name="sb_decode",
    )(page_table, qbd, new_kv, *([pool] * (2 * PAGES_PER_STEP)))


def _moba_decode_kernel(pt_ref, qbd_ref, kmean_ref, new_ref, *refs, n_steps, dec_seq, past_len):
    kv_refs = refs[:PAGES_PER_STEP]
    o_ref, sel_ref, m_ref, l_ref, acc_ref, kpad_ref, vpad_ref = refs[PAGES_PER_STEP:]
    p = pl.program_id(1)
    n_blk = n_steps * PAGES_PER_STEP // PAGES_PER_MOBA_BLOCK
    qf = qbd_ref[0]
    qb = (qf * ATTN_SCALE).astype(BF16)
    rows = qf.shape[0]
    t_of_row = lax.rem(_iota((rows, 1), 0), dec_seq)
    blk_id = _iota((1, LANES), 1)

    @pl.when(p == 0)
    def _():
        q_hi, q_lo = _split_bf16(qf)
        k_hi, k_lo = _split_bf16(kmean_ref[0])
        gate = _dot(q_hi, k_hi) + _dot(q_hi, k_lo) + _dot(q_lo, k_hi)
        fully_past = (blk_id < (past_len + t_of_row) // MOBA_BLOCK) & (blk_id < n_blk)
        gate = jnp.where(fully_past, gate, -jnp.inf)
        sel_ref[...] = jnp.where(fully_past & (_top_rank(gate, n_blk, 1) < MOBA_TOPK), 1.0, 0.0)
        _osm_reset(m_ref, l_ref, acc_ref)

    sel = sel_ref[...]
    s_parts, mask_parts = [], []
    for g, kv_ref in enumerate(kv_refs):
        s_parts.append(_dot(qb, _heads_by_dim(kv_ref[0, 0]).astype(BF16)))
        blk = (p * PAGES_PER_STEP + g) // PAGES_PER_MOBA_BLOCK
        picked = jnp.sum(jnp.where(blk_id == blk, sel, 0.0), axis=1, keepdims=True) > 0.5
        mask_parts.append(jnp.broadcast_to(picked, (rows, PAGE_SIZE)))

    def weigh_pages(pb):
        out = jnp.zeros((rows, SB_WIDTH), F32)
        for g, kv_ref in enumerate(kv_refs):
            out = out + _dot_nt(pb[:, g * PAGE_SIZE:(g + 1) * PAGE_SIZE], _heads_by_dim(kv_ref[0, 1]).astype(BF16))
        return out

    st = _osm_update((m_ref[...], l_ref[...], acc_ref[...]), jnp.concatenate(s_parts, axis=1),
                     jnp.concatenate(mask_parts, axis=1), weigh_pages)
    m_ref[...], l_ref[...], acc_ref[...] = st

    @pl.when(p == n_steps - 1)
    def _():
        _load_new_rows(kpad_ref, new_ref, SB_WIDTH, SB_WIDTH, dec_seq)
        _load_new_rows(vpad_ref, new_ref, KV0_V_OFF + SB_WIDTH, SB_WIDTH, dec_seq)
        key = _iota((1, PAGE_SIZE), 1)
        vb = vpad_ref[...].astype(BF16)
        fin = _osm_update((m_ref[...], l_ref[...], acc_ref[...]), _dot_nt(qb, kpad_ref[...].astype(BF16)),
                          (key <= t_of_row) & (key < dec_seq), lambda pb: _dot(pb, vb))
        o_ref[0] = _fold_heads(_osm_final(fin), dec_seq, N_HEADS - H_SB)


def moba_decode(qbd, kmean, new_kv, pool, page_table, past_len):
    b, n_pages = page_table.shape
    dec_seq = new_kv.shape[1]
    rows = qbd.shape[1]
    assert past_len % MOBA_BLOCK == 0 and n_pages * PAGE_SIZE == past_len and dec_seq <= PAGE_SIZE
    assert n_pages % PAGES_PER_STEP == 0
    grid_spec = pltpu.PrefetchScalarGridSpec(
        num_scalar_prefetch=1,
        grid=(b, n_pages // PAGES_PER_STEP),
        in_specs=[pl.BlockSpec((1, rows, SB_WIDTH), lambda bi, p, pt: (bi, 0, 0)),
                  pl.BlockSpec((1, SB_WIDTH, LANES), lambda bi, p, pt: (bi, 0, 0)),
                  pl.BlockSpec((1, dec_seq, new_kv.shape[2]), lambda bi, p, pt: (bi, 0, 0))]
        + _page_specs((1, 2, N_HEADS - H_SB, HEAD_DIM, PAGE_SIZE),
                      lambda bi, p, pt, g: (pt[bi, p * PAGES_PER_STEP + g], 0, 1, 0, 0)),
        out_specs=pl.BlockSpec((1, dec_seq, SB_WIDTH), lambda bi, p, pt: (bi, 0, 0)),
        scratch_shapes=[pltpu.VMEM((rows, LANES), F32), pltpu.VMEM((rows, 1), F32), pltpu.VMEM((rows, 1), F32),
                        pltpu.VMEM((rows, SB_WIDTH), F32),
                        pltpu.VMEM((PAGE_SIZE, SB_WIDTH), F32), pltpu.VMEM((PAGE_SIZE, SB_WIDTH), F32)],
    )
    return pl.pallas_call(
        functools.partial(_moba_decode_kernel, n_steps=n_pages // PAGES_PER_STEP, dec_seq=dec_seq, past_len=past_len),
        grid_spec=grid_spec,
        out_shape=jax.ShapeDtypeStruct((b, dec_seq, SB_WIDTH), F32),
        compiler_params=_params("parallel", "arbitrary"),
        name="moba_decode",
    )(page_table, qbd, kmean, new_kv, *([pool] * PAGES_PER_STEP))


KV_WIDTH = NSA_HKV * HEAD_DIM


def _nsa_decode_kernel(pt_ref, qbd_ref, kc_ref, vc_ref, state_ref, new_kv_ref, new_win_ref, g_ref, *refs,
                       n_steps, dec_seq, past_len):
    page_refs = refs[:PAGES_PER_STEP]
    o_ref, sel_ref, oc_ref, ow_ref, m_ref, l_ref, acc_ref, kpad_ref, vpad_ref = refs[PAGES_PER_STEP:]
    p = pl.program_id(1)
    qb = (qbd_ref[0] * ATTN_SCALE).astype(BF16)
    rows = qb.shape[0]
    grp_rows = NSA_HKV * dec_seq
    n_cmp_rows = kc_ref.shape[1]
    n_slc = past_len // SLC_BLOCK + 1
    slc_lanes = sel_ref.shape[1]
    t_of_row = lax.rem(_iota((rows, 1), 0), dec_seq)
    qpos = past_len + t_of_row
    key = _iota((1, PAGE_SIZE), 1)

    @pl.when(p == 0)
    def _():
        s_c = _dot_nt(qb, kc_ref[0].astype(BF16))
        ok_c = (CMP_STRIDE * _iota((1, n_cmp_rows), 1) + CMP_BLOCK - 1) <= qpos
        sm = jnp.where(ok_c, s_c, NEG_BIG)
        pc = jnp.where(ok_c, jnp.exp(sm - jnp.max(sm, axis=1, keepdims=True)), 0.0)
        pc = pc / jnp.maximum(jnp.sum(pc, axis=1, keepdims=True), 1e-30)
        oc_ref[...] = _dot(pc.astype(BF16), vc_ref[0].astype(BF16))
        psum = pc[0:grp_rows]
        for g in range(1, NSA_GROUP):
            psum = psum + pc[g * grp_rows:(g + 1) * grp_rows]
        imp = _dot_exact_rhs(psum, _cover_matrix(n_cmp_rows, slc_lanes, 0))
        score = _slc_score(imp, qpos[0:grp_rows] // SLC_BLOCK, n_slc, 1)
        sel = jnp.where(_top_rank(score, n_slc, 1) < min(SLC_TOPK, n_slc), 1.0, 0.0)
        sel_ref[...] = jnp.concatenate([sel] * NSA_GROUP, axis=0)
        vwt = _heads_by_dim(state_ref[0, 1]).astype(BF16)
        st = _osm_update(_osm_init(rows, KV_WIDTH), _dot(qb, _heads_by_dim(state_ref[0, 0]).astype(BF16)),
                         _iota((1, WINDOW), 1) > t_of_row, lambda pb: _dot_nt(pb, vwt))
        _load_new_rows(kpad_ref, new_win_ref, 0, KV_WIDTH, dec_seq)
        _load_new_rows(vpad_ref, new_win_ref, KV_WIDTH, KV_WIDTH, dec_seq)
        vb = vpad_ref[...].astype(BF16)
        st = _osm_update(st, _dot_nt(qb, kpad_ref[...].astype(BF16)), (key <= t_of_row) & (key < dec_seq),
                         lambda pb: _dot(pb, vb))
        ow_ref[...] = _osm_final(st)
        _osm_reset(m_ref, l_ref, acc_ref)

    keys_per_step = PAGES_PER_STEP * PAGE_SIZE
    blk_of_key = p * (keys_per_step // SLC_BLOCK) + _iota((slc_lanes, keys_per_step), 1) // SLC_BLOCK
    expand = jnp.where(_iota((slc_lanes, keys_per_step), 0) == blk_of_key, 1.0, 0.0).astype(BF16)
    s_all = jnp.concatenate([_dot(qb, _heads_by_dim(ref[0, 0]).astype(BF16)) for ref in page_refs], axis=1)

    def weigh_pages(pb):
        out = jnp.zeros((rows, KV_WIDTH), F32)
        for g, ref in enumerate(page_refs):
            out = out + _dot_nt(pb[:, g * PAGE_SIZE:(g + 1) * PAGE_SIZE], _heads_by_dim(ref[0, 1]).astype(BF16))
        return out

    st = _osm_update((m_ref[...], l_ref[...], acc_ref[...]), s_all,
                     _dot(sel_ref[...].astype(BF16), expand) > 0.5, weigh_pages)
    m_ref[...], l_ref[...], acc_ref[...] = st

    @pl.when(p == n_steps - 1)
    def _():
        _load_new_rows(kpad_ref, new_kv_ref, 2 * KV_WIDTH, KV_WIDTH, dec_seq)
        _load_new_rows(vpad_ref, new_kv_ref, 3 * KV_WIDTH, KV_WIDTH, dec_seq)
        cur = past_len // SLC_BLOCK
        picked = sel_ref[:, cur:cur + 1] > 0.5
        vb = vpad_ref[...].astype(BF16)
        fin = _osm_update((m_ref[...], l_ref[...], acc_ref[...]), _dot_nt(qb, kpad_ref[...].astype(BF16)),
                          picked & (key <= t_of_row) & (key < dec_seq), lambda pb: _dot(pb, vb))
        o_s = _osm_final(fin)
        o_c = oc_ref[...]
        o_w = ow_ref[...]
        gates = g_ref[0]
        for hk in range(NSA_HKV):
            for g in range(NSA_GROUP):
                rs = slice(g * grp_rows + hk * dec_seq, g * grp_rows + (hk + 1) * dec_seq)
                hs = slice(hk * HEAD_DIM, (hk + 1) * HEAD_DIM)
                gc = hk * LANES + g * 3
                h = hk * NSA_GROUP + g
                o_ref[0, :, h * HEAD_DIM:(h + 1) * HEAD_DIM] = (
                    gates[:, gc:gc + 1] * o_c[rs, hs] + gates[:, gc + 1:gc + 2] * o_s[rs, hs]
                    + gates[:, gc + 2:gc + 3] * o_w[rs, hs])


def nsa_decode(qbd, kc, vc, state_win, new_kv, new_win, gates, pool, page_table, past_len):
    b, n_pages = page_table.shape
    dec_seq = new_kv.shape[1]
    rows = qbd.shape[1]
    n_chunk = kc.shape[1]
    n_slc = past_len // SLC_BLOCK + 1
    slc_lanes = -(-n_slc // LANES) * LANES
    assert past_len % SLC_BLOCK == 0 and past_len >= WINDOW and state_win.shape[-1] == WINDOW and dec_seq <= SLC_BLOCK
    assert n_pages % PAGES_PER_STEP == 0
    grid_spec = pltpu.PrefetchScalarGridSpec(
        num_scalar_prefetch=1,
        grid=(b, n_pages // PAGES_PER_STEP),
        in_specs=[pl.BlockSpec((1, rows, KV_WIDTH), lambda bi, p, pt: (bi, 0, 0)),
                  pl.BlockSpec((1, n_chunk, KV_WIDTH), lambda bi, p, pt: (bi, 0, 0)),
                  pl.BlockSpec((1, n_chunk, KV_WIDTH), lambda bi, p, pt: (bi, 0, 0)),
                  pl.BlockSpec((1, 2, NSA_HKV, HEAD_DIM, WINDOW), lambda bi, p, pt: (bi, 0, 0, 0, 0)),
                  pl.BlockSpec((1, dec_seq, 4 * KV_WIDTH), lambda bi, p, pt: (bi, 0, 0)),
                  pl.BlockSpec((1, dec_seq, 2 * KV_WIDTH), lambda bi, p, pt: (bi, 0, 0)),
                  pl.BlockSpec((1, dec_seq, NSA_HKV * LANES), lambda bi, p, pt: (bi, 0, 0))]
        + _page_specs((1, 2, NSA_HKV, HEAD_DIM, PAGE_SIZE),
                      lambda bi, p, pt, g: (pt[bi, p * PAGES_PER_STEP + g], 1, 0, 0, 0)),
        out_specs=pl.BlockSpec((1, dec_seq, N_HEADS * HEAD_DIM), lambda bi, p, pt: (bi, 0, 0)),
        scratch_shapes=[pltpu.VMEM((rows, slc_lanes), F32), pltpu.VMEM((rows, KV_WIDTH), F32),
                        pltpu.VMEM((rows, KV_WIDTH), F32), pltpu.VMEM((rows, 1), F32), pltpu.VMEM((rows, 1), F32),
                        pltpu.VMEM((rows, KV_WIDTH), F32),
                        pltpu.VMEM((PAGE_SIZE, KV_WIDTH), F32), pltpu.VMEM((PAGE_SIZE, KV_WIDTH), F32)],
    )
    return pl.pallas_call(
        functools.partial(_nsa_decode_kernel, n_steps=n_pages // PAGES_PER_STEP, dec_seq=dec_seq, past_len=past_len),
        grid_spec=grid_spec,
        out_shape=jax.ShapeDtypeStruct((b, dec_seq, N_HEADS * HEAD_DIM), F32),
        compiler_params=_params("parallel", "arbitrary"),
        name="nsa_decode",
    )(page_table, qbd, kc, vc, state_win, new_kv, new_win, gates, *([pool] * PAGES_PER_STEP))


EVEN_PLAN = (
    (0, 512, "copy", 0, 0, 0),
    (512, 512, "rope", 0, 0, 512),
    (1024, 512, "copy", 0, 1, 0),
    (1536, 512, "rope", 1, 1, 512),
    (2048, 1024, "copy", 0, 1, 1024),
)
ODD_PLAN = (
    (0, 1024, "rope", 0, 0, 0),
    (1024, 256, "rope", 1, 1, 0),
    (1280, 256, "copy", 0, 1, 256),
    (1536, 256, "rope", 2, 1, 512),
    (1792, 256, "copy", 0, 1, 768),
    (2048, 256, "rope", 3, 2, 0),
    (2304, 256, "copy", 0, 2, 256),
    (2560, 512, "sigmoid", 0, 3, 0),
)
ODD_QKV = N_HEADS * HEAD_DIM + 6 * NSA_HKV * HEAD_DIM


def _gate_columns():
    idx = np.full((NSA_HKV * LANES,), ODD_QKV + 3 * N_HEADS, np.int32)
    for h in range(N_HEADS):
        for r in range(3):
            idx[(h // NSA_GROUP) * LANES + (h % NSA_GROUP) * 3 + r] = ODD_QKV + h * 3 + r
    return idx


def _rope_tables(pos):
    half = HEAD_DIM // 2
    inv = jnp.power(ROPE_THETA, -jnp.arange(half, dtype=F32) / half)
    ang = pos.astype(F32)[:, None] * inv[None, :]
    cos, sin = jnp.cos(ang), jnp.sin(ang)
    return (jnp.concatenate([cos] * (LANES // half), axis=1),
            jnp.concatenate([-sin, sin] * (LANES // HEAD_DIM), axis=1))


def _gain_rows(*gains):
    return jnp.stack([jnp.tile(g.astype(F32), LANES // HEAD_DIM) for g in gains])


def _block_diag_queries(q, heads_per_group, n_groups, group_major):
    b, dec, n_heads, _ = q.shape
    h = np.arange(n_heads)
    grp = h // heads_per_group
    onehot = jnp.asarray(np.eye(n_groups, dtype=np.float32)[grp])
    bd = q.transpose(0, 2, 1, 3)[:, :, :, None, :] * onehot[None, :, None, :, None]
    if group_major:
        bd = bd.reshape(b, n_groups, heads_per_group, dec, n_groups, HEAD_DIM).transpose(0, 2, 1, 3, 4, 5)
    return bd.reshape(b, n_heads * dec, n_groups * HEAD_DIM)


def _prepare_weights(w_in0, w_out0, w_in1, w_out1, cmp_w1, cmp_w2, w_gu, w_down):
    w_in1_ext = jnp.concatenate([w_in1[0], jnp.zeros((D_MODEL, 1), w_in1.dtype)], axis=1)
    cols = np.concatenate([np.arange(ODD_QKV, dtype=np.int32), _gate_columns()])
    return dict(
        w_in0=w_in0[0].astype(BF16),
        w_out0_sb=w_out0[0, :SB_WIDTH].astype(BF16),
        w_out0_mb=w_out0[0, SB_WIDTH:].astype(BF16),
        w_in1=w_in1_ext[:, cols].astype(BF16),
        w_out1=w_out1[0].astype(BF16),
        cmp_w1=cmp_w1[0].astype(BF16),
        cmp_w2=cmp_w2[0].astype(BF16),
        w_gu=w_gu.astype(BF16),
        w_down=w_down.astype(BF16),
    )


def _cmp_pe_rows(cmp_pe):
    return cmp_pe[0].reshape(2, CMP_BLOCK // CMP_STRIDE, CMP_HALF)


def _tokens_last(x):
    return x.transpose(0, 2, 3, 4, 1)


def _tokens_first(x):
    return x.transpose(0, 4, 1, 2, 3)


def _trunk(x, q0, caches, w, g_mix0, g_q0, g_k0, g_mix1, g_q1, g_k1, cmp_pe, g_ffn):
    b, t, _ = x.shape
    m = b * t
    tm = min(512, m)
    xf = x.reshape(m, D_MODEL)
    pos = q0 + jnp.arange(t)
    cos, sin = _rope_tables(pos)
    if m // tm * tm != m or t % tm != 0:
        cos, sin = jnp.tile(cos, (m // t, 1)), jnp.tile(sin, (m // t, 1))

    prompt = caches is None
    q_l0, kv_l0, *kv0_t = norm_proj_post(xf, g_mix0[0], w["w_in0"], cos, sin, _gain_rows(g_q0[0], g_k0[0]), EVEN_PLAN,
                                         (1024, 2048), tm, ((1, (2, N_HEADS)),) if prompt else (), t)
    kv0_out = _tokens_first(kv0_t[0]) if prompt else kv_l0.reshape(b, t, 2, N_HEADS, HEAD_DIM)
    if caches is None:
        o_sb = sb_prefill(q_l0.reshape(b, t, 1024), kv_l0.reshape(b, t, 2048)).reshape(m, SB_WIDTH)
        o_mb = moba_prefill(q_l0.reshape(b, t, 1024), kv_l0.reshape(b, t, 2048)).reshape(m, SB_WIDTH)
    else:
        pool0, pool1, state_win, page_table = caches
        q4 = q_l0.reshape(b, t, N_HEADS, HEAD_DIM)
        new_kv0 = kv_l0.reshape(b, t, 2048)
        o_sb, kmean = sb_decode(_block_diag_queries(q4[:, :, :H_SB], 1, H_SB, False), new_kv0, pool0, page_table)
        o_mb = moba_decode(_block_diag_queries(q4[:, :, H_SB:], 1, N_HEADS - H_SB, False), kmean, new_kv0, pool0,
                           page_table, q0)
        o_sb, o_mb = o_sb.reshape(m, SB_WIDTH), o_mb.reshape(m, SB_WIDTH)
    h2 = mixer_out_ffn([o_sb, o_mb], [w["w_out0_sb"], w["w_out0_mb"]], xf, g_ffn[0], w["w_gu"][0], w["w_down"][0],
                       tm, D_FF // 2)

    gains1 = _gain_rows(g_q1[0], g_k1[0, 0], g_k1[0, 1], g_k1[0, 2])
    q_l1, kv_l1, win_l1, gates, *kv1_t = norm_proj_post(h2, g_mix1[0], w["w_in1"], cos, sin, gains1, ODD_PLAN,
                                                        (1024, 1024, 512, 512), tm,
                                                        ((1, (4, NSA_HKV)),) if prompt else (), t)
    kv1_out = _tokens_first(kv1_t[0]) if prompt else kv_l1.reshape(b, t, 4, NSA_HKV, HEAD_DIM)
    pe_rows = _cmp_pe_rows(cmp_pe)
    new_win = win_l1.reshape(b, t, 2, NSA_HKV, HEAD_DIM)
    if caches is None:
        kc, vc = compress_pages(kv1_t[0], None, pe_rows, w["cmp_w1"], w["cmp_w2"])
        o_nsa = nsa_prefill(q_l1.reshape(b, t, 1024), kv_l1.reshape(b, t, 1024), win_l1.reshape(b, t, 512),
                            kc, vc, gates.reshape(b, t, 512)).reshape(m, 1024)
        win_state = new_win[:, t - min(WINDOW, t):]
    else:
        kc, vc = compress_pages(pool1, page_table, pe_rows, w["cmp_w1"], w["cmp_w2"])
        qbd1 = _block_diag_queries(q_l1.reshape(b, t, N_HEADS, HEAD_DIM), NSA_GROUP, NSA_HKV, True)
        o_nsa = nsa_decode(qbd1, kc, vc, _tokens_last(state_win), kv_l1.reshape(b, t, 1024), win_l1.reshape(b, t, 512),
                           gates.reshape(b, t, 512), pool1, page_table, q0).reshape(m, 1024)
        win_state = jnp.concatenate([state_win[:, t:], new_win], axis=1)
    y = mixer_out_ffn([o_nsa], [w["w_out1"]], h2, g_ffn[1], w["w_gu"][1], w["w_down"][1], tm, D_FF // 2)

    return (y.reshape(b, t, D_MODEL), kv0_out[None], kv1_out[None], win_state[None])


def kernel(x_prompt, x_sample, cache_kv0, cache_kv1, state_win, page_table, g_mix0, w_in0, g_q0, g_k0, w_out0,
           g_mix1, w_in1, g_q1, g_k1, cmp_pe, cmp_w1, cmp_w2, w_out1, g_ffn, w_gu, w_down):
    assert w_in0.shape[0] == 1 and w_in1.shape[0] == 1, "one even and one odd layer"
    w = _prepare_weights(w_in0, w_out0, w_in1, w_out1, cmp_w1, cmp_w2, w_gu, w_down)
    norms = (g_mix0, g_q0, g_k0, g_mix1, g_q1, g_k1, cmp_pe, g_ffn)
    y_p, kv0_p, kv1_p, win_p = _trunk(x_prompt, 0, None, w, *norms)
    past_len = page_table.shape[1] * cache_kv0.shape[2]
    caches = (_tokens_last(cache_kv0[0]), _tokens_last(cache_kv1[0]), state_win[0], page_table)
    y_s, kv0_s, kv1_s, win_s = _trunk(x_sample, past_len, caches, w, *norms)
    return (y_p, y_s, kv0_p, kv0_s, kv1_p, kv1_s, win_p, win_s)
```

```python
import functools

import numpy as np
import jax
import jax.numpy as jnp
from jax import lax
from jax.experimental import pallas as pl
from jax.experimental.pallas import tpu as pltpu

F32 = jnp.float32
BF16 = jnp.bfloat16

D_MODEL = 1024
HEAD_DIM = 64
N_HEADS = 16
H_SB = 8
PAGE_SIZE = 128
MOBA_BLOCK = 256
MOBA_TOPK = 3
NSA_HKV = 4
NSA_GROUP = 4
CMP_BLOCK = 32
CMP_STRIDE = 16
CMP_HIDDEN = 256
SLC_BLOCK = 64
SLC_TOPK = 16
WINDOW = 512
D_FF = 2816
ROPE_THETA = 10000.0
NORM_EPS = 1e-6
NEG_BIG = -1e30
FORCE_SCORE = 1e9
ATTN_SCALE = HEAD_DIM ** -0.5

LANES = 128
HEADS_PER_SLAB = LANES // HEAD_DIM
VMEM_LIMIT_BYTES = 56 * 1024 * 1024


def _params(*sem):
    return pltpu.CompilerParams(dimension_semantics=sem, vmem_limit_bytes=VMEM_LIMIT_BYTES)


def _iota(shape, dim):
    return lax.broadcasted_iota(jnp.int32, shape, dim)


def _dot(a, b):
    return jnp.dot(a, b, preferred_element_type=F32)


def _dot_nt(a, b):
    return lax.dot_general(a, b, (((1,), (1,)), ((), ())), preferred_element_type=F32)


def _split_bf16(x):
    hi = x.astype(BF16)
    lo = (x - hi.astype(F32)).astype(BF16)
    return hi, lo


def _dot_exact_rhs(x, rhs_bf16):
    hi, lo = _split_bf16(x)
    return _dot(hi, rhs_bf16) + _dot(lo, rhs_bf16)


def _log_sigmoid_pair(z):
    t = jnp.log(1.0 + jnp.exp(-jnp.abs(z)))
    return jnp.minimum(z, 0.0) - t, jnp.minimum(-z, 0.0) - t


def _later_matrix(n):
    return jnp.where(_iota((n, n), 0) > _iota((n, n), 1), 1.0, 0.0).astype(BF16)


def _osm_init(rows, width):
    return (jnp.full((rows, 1), NEG_BIG, F32), jnp.zeros((rows, 1), F32), jnp.zeros((rows, width), F32))


def _osm_update(state, s, mask, weigh):
    m, l, acc = state
    sm = jnp.where(mask, s, NEG_BIG)
    m_new = jnp.maximum(m, jnp.max(sm, axis=1, keepdims=True))
    p = jnp.where(mask, jnp.exp(sm - m_new), 0.0)
    a = jnp.exp(m - m_new)
    return (m_new, a * l + jnp.sum(p, axis=1, keepdims=True), a * acc + weigh(p.astype(BF16)))


def _osm_final(state):
    _, l, acc = state
    return acc / jnp.maximum(l, 1e-30)


def _osm_reset(m_ref, l_ref, acc_ref):
    m_ref[...] = jnp.full_like(m_ref, NEG_BIG)
    l_ref[...] = jnp.zeros_like(l_ref)
    acc_ref[...] = jnp.zeros_like(acc_ref)


def _osm_update_biased(state, s_biased, weigh):
    m, l, acc = state
    m_new = jnp.maximum(m, jnp.max(s_biased, axis=1, keepdims=True))
    p = jnp.exp(s_biased - m_new)
    a = jnp.exp(m - m_new)
    return (m_new, a * l + jnp.sum(p, axis=1, keepdims=True), a * acc + weigh(p.astype(BF16)))


def _top_rank(score, n_real, axis):
    j = _iota(score.shape, axis)
    rank = jnp.zeros(score.shape, jnp.int32)
    for m in range(n_real):
        other = score[:, m:m + 1] if axis == 1 else score[m:m + 1, :]
        ahead = (other > score) | ((other == score) & (j > m))
        rank = rank + jnp.where(ahead, 1, 0)
    return rank


def _norm_proj_post_kernel(x_ref, g_ref, w_ref, cos_ref, sin_ref, gain_ref, *refs, plan, n_out, t_plan):
    out_refs, t_refs, p_ref = refs[:n_out], refs[n_out:-1], refs[-1]
    x = x_ref[...]
    ms = jnp.mean(x * x, axis=-1, keepdims=True)
    p_ref[...] = _dot((x * lax.rsqrt(ms + NORM_EPS) * g_ref[...]).astype(BF16), w_ref[...])
    lane = _iota((1, LANES), 1)
    first_half = (lane & (HEAD_DIM - 1)) < HEAD_DIM // 2
    same_head = (_iota((LANES, LANES), 0) // HEAD_DIM) == (_iota((LANES, LANES), 1) // HEAD_DIM)
    head_mean = jnp.where(same_head, 1.0 / HEAD_DIM, 0.0).astype(BF16)
    cos = cos_ref[...]
    sin = sin_ref[...]
    for src, width, mode, gi, oi, dst in plan:
        if mode == "copy":
            out_refs[oi][:, dst:dst + width] = p_ref[:, src:src + width]
        elif mode == "sigmoid":
            x = p_ref[:, src:src + width]
            out_refs[oi][:, dst:dst + width] = 1.0 / (1.0 + jnp.exp(-x))
        else:
            for s in range(width // LANES):
                x = p_ref[:, src + s * LANES:src + (s + 1) * LANES]
                ms = _dot_exact_rhs(x * x, head_mean)
                y = x * lax.rsqrt(ms + NORM_EPS) * gain_ref[gi:gi + 1, :]
                other = jnp.where(first_half, pltpu.roll(y, LANES - HEAD_DIM // 2, 1),
                                  pltpu.roll(y, HEAD_DIM // 2, 1))
                out_refs[oi][:, dst + s * LANES:dst + (s + 1) * LANES] = y * cos + other * sin
    for t_ref, (oi, lead) in zip(t_refs, t_plan):
        t_ref[0] = out_refs[oi][...].T.reshape(lead + (HEAD_DIM, x_ref.shape[0]))


def norm_proj_post(x, g, w_bf16, cos, sin, gains, plan, out_widths, tm, t_plan=(), seq_len=None):
    m, k = x.shape
    n = w_bf16.shape[1]
    period = cos.shape[0] // tm
    tiles_per_seq = seq_len // tm if t_plan else 1
    t_specs = [pl.BlockSpec((1,) + lead + (HEAD_DIM, tm), lambda i: (i // tiles_per_seq, 0, 0, 0, i % tiles_per_seq))
               for _, lead in t_plan]
    t_shapes = [jax.ShapeDtypeStruct((m // seq_len,) + lead + (HEAD_DIM, seq_len), F32) for _, lead in t_plan]
    return pl.pallas_call(
        functools.partial(_norm_proj_post_kernel, plan=plan, n_out=len(out_widths), t_plan=t_plan),
        grid=(m // tm,),
        in_specs=[pl.BlockSpec((tm, k), lambda i: (i, 0)),
                  pl.BlockSpec((1, k), lambda i: (0, 0)),
                  pl.BlockSpec((k, n), lambda i: (0, 0)),
                  pl.BlockSpec((tm, LANES), lambda i: (i % period, 0)),
                  pl.BlockSpec((tm, LANES), lambda i: (i % period, 0)),
                  pl.BlockSpec(gains.shape, lambda i: (0, 0))],
        out_specs=[pl.BlockSpec((tm, w), lambda i: (i, 0)) for w in out_widths] + t_specs,
        out_shape=[jax.ShapeDtypeStruct((m, w), F32) for w in out_widths] + t_shapes,
        scratch_shapes=[pltpu.VMEM((tm, n), F32)],
        compiler_params=_params("parallel"),
        name="norm_proj_post",
    )(x, g.reshape(1, k), w_bf16, cos, sin, gains)


def _mixer_out_ffn_kernel(*refs, n_in):
    a_refs, wo_refs = refs[:n_in], refs[n_in:2 * n_in]
    h_ref, g_ref, wg_ref, wu_ref, wd_ref, o_ref, xn_ref, acc_ref = refs[2 * n_in:]
    j = pl.program_id(1)

    @pl.when(j == 0)
    def _():
        x = h_ref[...]
        for a_ref, wo_ref in zip(a_refs, wo_refs):
            x = x + _dot(a_ref[...].astype(BF16), wo_ref[...])
        ms = jnp.mean(x * x, axis=-1, keepdims=True)
        xn_ref[...] = (x * lax.rsqrt(ms + NORM_EPS) * g_ref[...]).astype(BF16)
        acc_ref[...] = x

    xn = xn_ref[...]
    gate = _dot(xn, wg_ref[...])
    up = _dot(xn, wu_ref[...])
    act = (gate / (1.0 + jnp.exp(-gate))) * up
    acc_ref[...] += _dot(act.astype(BF16), wd_ref[...])

    @pl.when(j == pl.num_programs(1) - 1)
    def _():
        o_ref[...] = acc_ref[...]


def mixer_out_ffn(a_list, wo_list, res, g, w_gu_bf16, w_down_bf16, tm, tf):
    m, k = res.shape
    n_chunks = D_FF // tf
    return pl.pallas_call(
        functools.partial(_mixer_out_ffn_kernel, n_in=len(a_list)),
        grid=(m // tm, n_chunks),
        in_specs=[pl.BlockSpec((tm, a.shape[1]), lambda i, j: (i, 0)) for a in a_list]
        + [pl.BlockSpec(wo.shape, lambda i, j: (0, 0)) for wo in wo_list]
        + [pl.BlockSpec((tm, k), lambda i, j: (i, 0)),
           pl.BlockSpec((1, k), lambda i, j: (0, 0)),
           pl.BlockSpec((k, tf), lambda i, j: (0, j)),
           pl.BlockSpec((k, tf), lambda i, j: (0, j + n_chunks)),
           pl.BlockSpec((tf, k), lambda i, j: (j, 0))],
        out_specs=pl.BlockSpec((tm, k), lambda i, j: (i, 0)),
        out_shape=jax.ShapeDtypeStruct((m, k), F32),
        scratch_shapes=[pltpu.VMEM((tm, k), BF16), pltpu.VMEM((tm, k), F32)],
        compiler_params=_params("parallel", "arbitrary"),
        name="mixer_out_ffn",
    )(*a_list, *wo_list, res, g.reshape(1, k), w_gu_bf16, w_gu_bf16, w_down_bf16)


ATTN_TQ = 256
ATTN_TK = 256


def _stage_keys_values(k_ref, k_cols, v_ref, v_cols, kb_ref, vt_ref, n_tiles, tk):
    kb_ref[...] = k_ref[0, :, k_cols].astype(BF16)
    for c in range(n_tiles):
        vt_ref[c] = v_ref[0, c * tk:(c + 1) * tk, v_cols].T.astype(BF16)


def _split_heads_t(qt, scale):
    row_head = _iota((LANES, 1), 0) // HEAD_DIM
    return [jnp.where(row_head == h, qt * scale, 0.0) for h in range(HEADS_PER_SLAB)]


def _osm_t_init(width, cols):
    return (jnp.full((1, cols), NEG_BIG, F32), jnp.zeros((1, cols), F32), jnp.zeros((width, cols), F32))


def _osm_t_update(state, s_biased_t, weigh):
    m, l, acc = state
    m_new = jnp.maximum(m, jnp.max(s_biased_t, axis=0, keepdims=True))
    p = jnp.exp(s_biased_t - m_new)
    a = jnp.exp(m - m_new)
    return (m_new, a * l + jnp.sum(p, axis=0, keepdims=True), a * acc + weigh(p.astype(BF16)))


def _osm_t_update2(state, s_a, s_b, weigh_a, weigh_b):
    m, l, acc = state
    m_new = jnp.maximum(m, jnp.maximum(jnp.max(s_a, axis=0, keepdims=True), jnp.max(s_b, axis=0, keepdims=True)))
    p_a = jnp.exp(s_a - m_new)
    p_b = jnp.exp(s_b - m_new)
    a = jnp.exp(m - m_new)
    l_new = a * l + jnp.sum(p_a, axis=0, keepdims=True) + jnp.sum(p_b, axis=0, keepdims=True)
    return (m_new, l_new, a * acc + weigh_a(p_a.astype(BF16)) + weigh_b(p_b.astype(BF16)))


def _osm_t_final(state):
    _, l, acc = state
    return acc / jnp.maximum(l, 1e-30)


def _weigh_heads_t(vt_ref, tile, w, tq):
    return jnp.concatenate([_dot(vt_ref[tile, h * HEAD_DIM:(h + 1) * HEAD_DIM, :], w[:, h * tq:(h + 1) * tq])
                            for h in range(HEADS_PER_SLAB)], axis=1)


def _unstack_heads_t(x, tq):
    return jnp.concatenate([x[:, h * tq:(h + 1) * tq] for h in range(HEADS_PER_SLAB)], axis=0)


def _sb_prefill_kernel(q_ref, k_ref, v_ref, o_ref, kb_ref, vt_ref, *, n_tiles):
    qi = pl.program_id(2)
    tq, tk = ATTN_TQ, ATTN_TK

    @pl.when(qi == 0)
    def _():
        _stage_keys_values(k_ref, slice(None), v_ref, slice(None), kb_ref, vt_ref, n_tiles, tk)

    qpos = jnp.concatenate([qi * tq + _iota((1, tq), 1)] * HEADS_PER_SLAB, axis=1)
    sooner = jnp.where(_iota((tk, tk), 0) < _iota((tk, tk), 1), 1.0, 0.0).astype(BF16)
    qs = jnp.concatenate(_split_heads_t(q_ref[0].T, ATTN_SCALE), axis=1).astype(BF16)

    def log_terms(kb):
        z = _dot(kb_ref[pl.ds(pl.multiple_of(kb * tk, tk), tk), :], qs)
        return _log_sigmoid_pair(z)

    def gap_within(lk):
        hi, lo = _split_bf16(lk)
        return _dot(sooner, hi) + _dot(sooner, lo)

    def diagonal_tile():
        past = (qi * tk + _iota((tk, 1), 0)) < qpos
        ls, lk = log_terms(qi)
        lk = jnp.where(past, lk, 0.0)
        w = jnp.where(past, jnp.exp(ls + gap_within(lk)), 0.0)
        return jnp.sum(lk, axis=0, keepdims=True), _weigh_heads_t(vt_ref, qi, w.astype(BF16), tq)

    def pair(i, carry):
        c, acc = carry
        ka = qi - 1 - 2 * i
        kb = jnp.maximum(ka - 1, 0)
        there = jnp.where(ka >= 1, 1.0, 0.0)
        ls_a, lk_a = log_terms(ka)
        ls_b, lk_b = log_terms(kb)
        lk_b = lk_b * there
        sum_a = jnp.sum(lk_a, axis=0, keepdims=True)
        w_a = jnp.exp(ls_a + gap_within(lk_a) + c)
        w_b = jnp.exp(ls_b + gap_within(lk_b) + (c + sum_a)) * there
        acc = acc + _weigh_heads_t(vt_ref, ka, w_a.astype(BF16), tq) + _weigh_heads_t(vt_ref, kb, w_b.astype(BF16), tq)
        return c + sum_a + jnp.sum(lk_b, axis=0, keepdims=True), acc

    _, acc = lax.fori_loop(0, (qi + 1) // 2, pair, diagonal_tile())
    o_ref[0] = _unstack_heads_t(acc, tq).T


def sb_prefill(q, kv):
    b, t, _ = q.shape
    n_slab = H_SB // HEADS_PER_SLAB
    v_off = N_HEADS // HEADS_PER_SLAB
    assert t % ATTN_TK == 0 and ATTN_TQ == ATTN_TK
    return pl.pallas_call(
        functools.partial(_sb_prefill_kernel, n_tiles=t // ATTN_TK),
        grid=(b, n_slab, t // ATTN_TQ),
        in_specs=[pl.BlockSpec((1, ATTN_TQ, LANES), lambda bi, s, qi: (bi, qi, s)),
                  pl.BlockSpec((1, t, LANES), lambda bi, s, qi: (bi, 0, s)),
                  pl.BlockSpec((1, t, LANES), lambda bi, s, qi: (bi, 0, v_off + s))],
        out_specs=pl.BlockSpec((1, ATTN_TQ, LANES), lambda bi, s, qi: (bi, qi, s)),
        out_shape=jax.ShapeDtypeStruct((b, t, H_SB * HEAD_DIM), F32),
        scratch_shapes=[pltpu.VMEM((t, LANES), BF16), pltpu.VMEM((t // ATTN_TK, LANES, ATTN_TK), BF16)],
        compiler_params=_params("parallel", "parallel", "arbitrary"),
        name="sb_prefill",
    )(q, kv, kv)


def _moba_prefill_kernel(q_ref, k_ref, v_ref, o_ref, kmean_ref, kb_ref, vt_ref, *, n_blk):
    qi = pl.program_id(2)
    tq, tk = ATTN_TQ, ATTN_TK
    t_total = n_blk * MOBA_BLOCK
    blk_rows = -(-n_blk // 8) * 8

    @pl.when(qi == 0)
    def _():
        kmean_ref[...] = jnp.zeros_like(kmean_ref)
        for n in range(n_blk):
            blk = k_ref[0, n * MOBA_BLOCK:(n + 1) * MOBA_BLOCK, :]
            kmean_ref[n:n + 1, :] = jnp.sum(blk, axis=0, keepdims=True) * (1.0 / MOBA_BLOCK)
        _stage_keys_values(k_ref, slice(None), v_ref, slice(None), kb_ref, vt_ref, n_blk, tk)

    cols = HEADS_PER_SLAB * tq
    qpos = jnp.concatenate([qi * tq + _iota((1, tq), 1)] * HEADS_PER_SLAB, axis=1)
    blk_id = _iota((blk_rows, cols), 0)
    k_hi, k_lo = _split_bf16(kmean_ref[0:blk_rows, :])
    qf = jnp.concatenate(_split_heads_t(q_ref[0].T, 1.0), axis=1)
    qs = (qf * ATTN_SCALE).astype(BF16)
    q_hi, q_lo = _split_bf16(qf)
    gate = _dot(k_hi, q_hi) + _dot(k_lo, q_hi) + _dot(k_hi, q_lo)
    fully_past = (blk_id < qi) & (blk_id < n_blk)
    gate = jnp.where(fully_past, gate, -jnp.inf)
    sel = jnp.where(fully_past & (_top_rank(gate, n_blk, 0) < MOBA_TOPK), 1.0, 0.0)

    def scores(kb, limit):
        bias = jnp.where((kb * tk + _iota((tk, 1), 0)) <= limit, 0.0, NEG_BIG)
        return _dot(kb_ref[pl.ds(pl.multiple_of(kb * tk, tk), tk), :], qs) + bias

    def past_limit(kb, valid):
        picked = jnp.sum(jnp.where(blk_id == kb, sel, 0.0), axis=0, keepdims=True) > 0.5
        return jnp.where(picked, jnp.where(valid, t_total, -1), -1)

    def body(i, st):
        ka = qi - 2 * i
        kb = ka - 1
        kb_safe = jnp.maximum(kb, 0)
        s_a = scores(ka, jnp.where(ka == qi, qpos, past_limit(ka, True)))
        s_b = scores(kb_safe, past_limit(kb_safe, kb >= 0))
        return _osm_t_update2(st, s_a, s_b, lambda p: _weigh_heads_t(vt_ref, ka, p, tq),
                              lambda p: _weigh_heads_t(vt_ref, kb_safe, p, tq))

    out = _osm_t_final(lax.fori_loop(0, (qi + 2) // 2, body, _osm_t_init(HEAD_DIM, cols)))
    o_ref[0] = _unstack_heads_t(out, tq).T


def moba_prefill(q, kv):
    b, t, _ = q.shape
    n_slab = (N_HEADS - H_SB) // HEADS_PER_SLAB
    q_off = H_SB // HEADS_PER_SLAB
    v_off = N_HEADS // HEADS_PER_SLAB
    assert ATTN_TQ == MOBA_BLOCK and t % MOBA_BLOCK == 0 and t // MOBA_BLOCK <= LANES
    return pl.pallas_call(
        functools.partial(_moba_prefill_kernel, n_blk=t // MOBA_BLOCK),
        grid=(b, n_slab, t // ATTN_TQ),
        in_specs=[pl.BlockSpec((1, ATTN_TQ, LANES), lambda bi, s, qi: (bi, qi, q_off + s)),
                  pl.BlockSpec((1, t, LANES), lambda bi, s, qi: (bi, 0, q_off + s)),
                  pl.BlockSpec((1, t, LANES), lambda bi, s, qi: (bi, 0, v_off + q_off + s))],
        out_specs=pl.BlockSpec((1, ATTN_TQ, LANES), lambda bi, s, qi: (bi, qi, s)),
        out_shape=jax.ShapeDtypeStruct((b, t, (N_HEADS - H_SB) * HEAD_DIM), F32),
        scratch_shapes=[pltpu.VMEM((LANES, LANES), F32), pltpu.VMEM((t, LANES), BF16),
                        pltpu.VMEM((t // ATTN_TK, LANES, ATTN_TK), BF16)],
        compiler_params=_params("parallel", "parallel", "arbitrary"),
        name="moba_prefill",
    )(q, kv, kv)


CHUNKS_PER_PAGE = PAGE_SIZE // CMP_STRIDE
CMP_HALF = CMP_STRIDE * HEAD_DIM
PAGES_PER_STEP = 16


def _page_specs(block, index_of_page):
    return [pl.BlockSpec(block, functools.partial(index_of_page, g=g)) for g in range(PAGES_PER_STEP)]


def _compress_kernel(pt_ref, *refs, n_steps):
    page_refs = refs[:PAGES_PER_STEP]
    pe_ref, w1_ref, w2_ref, kc_ref, vc_ref, x_ref = refs[PAGES_PER_STEP:]
    p = pl.program_id(1)
    lane = _iota((1, LANES), 1)
    low = lane < HEAD_DIM
    r = _iota((PAGE_SIZE, PAGE_SIZE), 0)
    token = _iota((PAGE_SIZE, PAGE_SIZE), 1)
    perm = jnp.where(token == CMP_STRIDE * (r % CHUNKS_PER_PAGE) + r // CHUNKS_PER_PAGE, 1.0, 0.0).astype(BF16)
    for g, page_ref in enumerate(page_refs):
        rows = pl.ds(pl.multiple_of((p * PAGES_PER_STEP + g) * CHUNKS_PER_PAGE, CHUNKS_PER_PAGE), CHUNKS_PER_PAGE)
        for kv in range(2):
            for s in range(NSA_HKV // HEADS_PER_SLAB):
                two_heads = page_ref[0, kv, HEADS_PER_SLAB * s:HEADS_PER_SLAB * (s + 1)].reshape(LANES, PAGE_SIZE)
                hi, lo = _split_bf16(two_heads)
                slab = _dot_nt(perm, hi) + _dot_nt(perm, lo)
                for pp in range(CMP_STRIDE // 2):
                    even = slab[2 * pp * CHUNKS_PER_PAGE:(2 * pp + 1) * CHUNKS_PER_PAGE]
                    odd = slab[(2 * pp + 1) * CHUNKS_PER_PAGE:(2 * pp + 2) * CHUNKS_PER_PAGE]
                    head0 = jnp.where(low, even, pltpu.roll(odd, HEAD_DIM, 1))
                    head1 = jnp.where(low, pltpu.roll(even, HEAD_DIM, 1), odd)
                    x_ref[kv * NSA_HKV + 2 * s, rows, pp * LANES:(pp + 1) * LANES] = head0
                    x_ref[kv * NSA_HKV + 2 * s + 1, rows, pp * LANES:(pp + 1) * LANES] = head1

    @pl.when(p == n_steps - 1)
    def _():
        n_chunk = n_steps * PAGES_PER_STEP * CHUNKS_PER_PAGE
        for kv, out_ref in ((0, kc_ref), (1, vc_ref)):
            pe_a = pe_ref[kv, 0:1, :]
            pe_b = pe_ref[kv, 1:2, :]
            w1a = w1_ref[kv, 0:CMP_HALF, :]
            w1b = w1_ref[kv, CMP_HALF:2 * CMP_HALF, :]
            w2 = w2_ref[kv]
            for h in range(NSA_HKV):
                x = x_ref[kv * NSA_HKV + h]
                first = _dot((x + pe_a).astype(BF16), w1a)
                second = _dot((x + pe_b).astype(BF16), w1b)
                hid = first + pltpu.roll(second, n_chunk - 1, 0)
                hid = hid / (1.0 + jnp.exp(-hid))
                out_ref[0, :, h * HEAD_DIM:(h + 1) * HEAD_DIM] = _dot(hid.astype(BF16), w2)


def compress_pages(pages, page_table, pe, w1_bf16, w2_bf16):
    if page_table is None:
        b, n_pages = pages.shape[0], pages.shape[-1] // PAGE_SIZE
        page_table = jnp.zeros((1, 1), jnp.int32)

        def index_of_page(bi, p, pt, g):
            return (bi, 0, 0, 0, p * PAGES_PER_STEP + g)
    else:
        b, n_pages = page_table.shape

        def index_of_page(bi, p, pt, g):
            return (pt[bi, p * PAGES_PER_STEP + g], 0, 0, 0, 0)
    assert n_pages % PAGES_PER_STEP == 0
    n_chunk = n_pages * CHUNKS_PER_PAGE
    width = NSA_HKV * HEAD_DIM
    grid_spec = pltpu.PrefetchScalarGridSpec(
        num_scalar_prefetch=1,
        grid=(b, n_pages // PAGES_PER_STEP),
        in_specs=_page_specs((1, 2, NSA_HKV, HEAD_DIM, PAGE_SIZE), index_of_page) + [
                  pl.BlockSpec(pe.shape, lambda bi, p, pt: (0, 0, 0)),
                  pl.BlockSpec(w1_bf16.shape, lambda bi, p, pt: (0, 0, 0)),
                  pl.BlockSpec(w2_bf16.shape, lambda bi, p, pt: (0, 0, 0))],
        out_specs=[pl.BlockSpec((1, n_chunk, width), lambda bi, p, pt: (bi, 0, 0)),
                   pl.BlockSpec((1, n_chunk, width), lambda bi, p, pt: (bi, 0, 0))],
        scratch_shapes=[pltpu.VMEM((2 * NSA_HKV, n_chunk, CMP_HALF), F32)],
    )
    return pl.pallas_call(
        functools.partial(_compress_kernel, n_steps=n_pages // PAGES_PER_STEP),
        grid_spec=grid_spec,
        out_shape=[jax.ShapeDtypeStruct((b, n_chunk, width), F32)] * 2,
        compiler_params=_params("parallel", "arbitrary"),
        name="nsa_compress",
    )(page_table, *([pages] * PAGES_PER_STEP), pe, w1_bf16, w2_bf16)


NSA_TQ = 256
SLC_TK = 256
WIN_TK = 128


def _cover_matrix(n_cmp_rows, n_cols, cmp_axis):
    i = _iota((n_cmp_rows, n_cols) if cmp_axis == 0 else (n_cols, n_cmp_rows), cmp_axis)
    j = _iota((n_cmp_rows, n_cols) if cmp_axis == 0 else (n_cols, n_cmp_rows), 1 - cmp_axis)
    ratio = SLC_BLOCK // CMP_STRIDE
    reach = CMP_BLOCK // CMP_STRIDE - 1
    return jnp.where((i >= ratio * j - reach) & (i <= ratio * j + ratio - 1), 1.0, 0.0).astype(BF16)


def _slc_score(imp, cur, n_slc, axis):
    j = _iota(imp.shape, axis)
    forced = (j == 0) | (j == cur) | (j == cur - 1)
    score = jnp.where(forced, FORCE_SCORE, jnp.where(j > cur, -FORCE_SCORE, imp))
    return jnp.where(j < n_slc, score, -jnp.inf)


def _nsa_prefill_kernel(q_ref, slc_ref, win_ref, kc_ref, vc_ref, g_ref, o_ref, ks_ref, vst_ref, kw_ref, vwt_ref, *, t):
    qi = pl.program_id(1)
    tq = NSA_TQ
    n_slc = t // SLC_BLOCK
    top = min(SLC_TOPK, n_slc)
    n_cmp_rows = t // CMP_STRIDE
    n_slab = NSA_HKV // HEADS_PER_SLAB

    @pl.when(qi == 0)
    def _():
        for s in range(n_slab):
            v_cols = slice(KV_WIDTH + s * LANES, KV_WIDTH + (s + 1) * LANES)
            k_cols = slice(s * LANES, (s + 1) * LANES)
            _stage_keys_values(slc_ref, k_cols, slc_ref, v_cols, ks_ref.at[s], vst_ref.at[s], t // SLC_TK, SLC_TK)
            _stage_keys_values(win_ref, k_cols, win_ref, v_cols, kw_ref.at[s], vwt_ref.at[s], t // WIN_TK, WIN_TK)

    qt = q_ref[0].T * ATTN_SCALE
    gates_t = g_ref[0].T
    qpos = qi * tq + _iota((1, tq), 1)
    qpos4 = jnp.concatenate([qpos] * NSA_GROUP, axis=1)
    cover_t = _cover_matrix(n_cmp_rows, LANES, 1)
    zeros_head = jnp.zeros((HEAD_DIM, tq), F32)
    outs = []
    for hk in range(NSA_HKV):
        half, slab = hk % HEADS_PER_SLAB, hk // HEADS_PER_SLAB
        cols = slice(slab * LANES, (slab + 1) * LANES)
        head_rows = slice(half * HEAD_DIM, (half + 1) * HEAD_DIM)
        parts = []
        for g in range(NSA_GROUP):
            h = hk * NSA_GROUP + g
            qg = qt[h * HEAD_DIM:(h + 1) * HEAD_DIM]
            parts.append(jnp.concatenate([qg, zeros_head] if half == 0 else [zeros_head, qg], axis=0))
        qs = jnp.concatenate(parts, axis=1).astype(BF16)

        s_c = _dot(kc_ref[0, :, cols].astype(BF16), qs)
        ok_c = (CMP_STRIDE * _iota((n_cmp_rows, 1), 0) + CMP_BLOCK - 1) <= qpos4
        sm = jnp.where(ok_c, s_c, NEG_BIG)
        pc = jnp.where(ok_c, jnp.exp(sm - jnp.max(sm, axis=0, keepdims=True)), 0.0)
        pc = pc / jnp.maximum(jnp.sum(pc, axis=0, keepdims=True), 1e-30)
        o_c = _dot(vc_ref[0, :, cols].T[head_rows].astype(BF16), pc.astype(BF16))

        psum = pc[:, 0:tq] + pc[:, tq:2 * tq] + pc[:, 2 * tq:3 * tq] + pc[:, 3 * tq:4 * tq]
        p_hi, p_lo = _split_bf16(psum)
        imp_t = (_dot(cover_t, p_hi) + _dot(cover_t, p_lo))[0:n_slc]
        score_t = _slc_score(imp_t, qpos // SLC_BLOCK, n_slc, 0)
        sel_t = jnp.where(_top_rank(score_t, n_slc, 0) < top, 1.0, 0.0)
        sel = jnp.concatenate([sel_t, jnp.zeros((LANES - n_slc, tq), F32)], axis=0).astype(BF16)

        n_slc_tiles = (qi * tq + tq + SLC_TK - 1) // SLC_TK

        def slc_scores(kt, valid, on_diagonal, qs=qs, sel=sel, slab=slab):
            blk_of_key = kt * (SLC_TK // SLC_BLOCK) + _iota((SLC_TK, LANES), 0) // SLC_BLOCK
            expand = jnp.where(_iota((SLC_TK, LANES), 1) == blk_of_key, jnp.where(valid, 1.0, 0.0), 0.0).astype(BF16)
            bias = _dot(expand, sel) * (-NEG_BIG) + NEG_BIG
            if on_diagonal:
                bias = jnp.where(kt * SLC_TK + _iota((SLC_TK, 1), 0) <= qpos, bias, NEG_BIG)
            s = _dot(ks_ref[slab, pl.ds(pl.multiple_of(kt * SLC_TK, SLC_TK), SLC_TK), :], qs)
            return s + jnp.concatenate([bias] * NSA_GROUP, axis=1)

        def slc_pair(st, ka, on_diagonal, slc_scores=slc_scores, slab=slab, head_rows=head_rows):
            kb = jnp.maximum(ka - 1, 0)
            return _osm_t_update2(st, slc_scores(ka, True, on_diagonal), slc_scores(kb, ka >= 1, False),
                                  lambda p: _dot(vst_ref[slab, ka, head_rows, :], p),
                                  lambda p: _dot(vst_ref[slab, kb, head_rows, :], p))

        st = slc_pair(_osm_t_init(HEAD_DIM, NSA_GROUP * tq), n_slc_tiles - 1, True)
        o_s = _osm_t_final(lax.fori_loop(1, (n_slc_tiles + 1) // 2,
                                         lambda i, st, slc_pair=slc_pair: slc_pair(st, n_slc_tiles - 1 - 2 * i, False), st))

        first_tile = jnp.maximum(qi * (tq // WIN_TK) - WINDOW // WIN_TK, 0)
        last_tile = qi * (tq // WIN_TK) + tq // WIN_TK - 1

        def win_scores(kt, valid, qs=qs, slab=slab):
            dist = lax.bitcast_convert_type(qpos - (kt * WIN_TK + _iota((WIN_TK, 1), 0)), jnp.uint32)
            bias = jnp.where(dist < WINDOW, jnp.where(valid, 0.0, NEG_BIG), NEG_BIG)
            s = _dot(kw_ref[slab, pl.ds(pl.multiple_of(kt * WIN_TK, WIN_TK), WIN_TK), :], qs)
            return s + jnp.concatenate([bias] * NSA_GROUP, axis=1)

        def win_body(i, st, win_scores=win_scores, slab=slab, head_rows=head_rows):
            ka = last_tile - 2 * i
            kb = jnp.maximum(ka - 1, first_tile)
            return _osm_t_update2(st, win_scores(ka, True), win_scores(kb, ka - 1 >= first_tile),
                                  lambda p: _dot(vwt_ref[slab, ka, head_rows, :], p),
                                  lambda p: _dot(vwt_ref[slab, kb, head_rows, :], p))

        o_w = _osm_t_final(lax.fori_loop(0, (last_tile - first_tile + 2) // 2, win_body,
                                         _osm_t_init(HEAD_DIM, NSA_GROUP * tq)))

        for g in range(NSA_GROUP):
            cs = slice(g * tq, (g + 1) * tq)
            gc = hk * LANES + g * 3
            outs.append(gates_t[gc:gc + 1] * o_c[:, cs] + gates_t[gc + 1:gc + 2] * o_s[:, cs]
                        + gates_t[gc + 2:gc + 3] * o_w[:, cs])
    o_ref[0] = jnp.concatenate(outs, axis=0).T


def nsa_prefill(q, kv, win, kc, vc, gates):
    b, t, _ = q.shape
    width = NSA_HKV * HEAD_DIM
    n_chunk = t // CMP_STRIDE
    n_slc = t // SLC_BLOCK
    assert t % SLC_TK == 0 and n_chunk % LANES == 0 and n_slc <= LANES and n_slc % 8 == 0
    assert NSA_TQ % WIN_TK == 0 and NSA_TQ % SLC_TK == 0
    n_slab = NSA_HKV // HEADS_PER_SLAB
    return pl.pallas_call(
        functools.partial(_nsa_prefill_kernel, t=t),
        grid=(b, t // NSA_TQ),
        in_specs=[pl.BlockSpec((1, NSA_TQ, N_HEADS * HEAD_DIM), lambda bi, qi: (bi, qi, 0)),
                  pl.BlockSpec((1, t, 2 * width), lambda bi, qi: (bi, 0, 1)),
                  pl.BlockSpec((1, t, 2 * width), lambda bi, qi: (bi, 0, 0)),
                  pl.BlockSpec((1, n_chunk, width), lambda bi, qi: (bi, 0, 0)),
                  pl.BlockSpec((1, n_chunk, width), lambda bi, qi: (bi, 0, 0)),
                  pl.BlockSpec((1, NSA_TQ, NSA_HKV * LANES), lambda bi, qi: (bi, qi, 0))],
        out_specs=pl.BlockSpec((1, NSA_TQ, N_HEADS * HEAD_DIM), lambda bi, qi: (bi, qi, 0)),
        out_shape=jax.ShapeDtypeStruct((b, t, N_HEADS * HEAD_DIM), F32),
        scratch_shapes=[pltpu.VMEM((n_slab, t, LANES), BF16), pltpu.VMEM((n_slab, t // SLC_TK, LANES, SLC_TK), BF16),
                        pltpu.VMEM((n_slab, t, LANES), BF16), pltpu.VMEM((n_slab, t // WIN_TK, LANES, WIN_TK), BF16)],
        compiler_params=_params("parallel", "arbitrary"),
        name="nsa_prefill",
    )(q, kv, win, kc, vc, gates)


SB_WIDTH = H_SB * HEAD_DIM
KV0_V_OFF = N_HEADS * HEAD_DIM
PAGES_PER_MOBA_BLOCK = MOBA_BLOCK // PAGE_SIZE


def _load_new_rows(dst_ref, new_ref, col, width, dec_seq):
    dst_ref[...] = jnp.zeros_like(dst_ref)
    dst_ref[0:dec_seq, :] = new_ref[0, :, col:col + width]


def _heads_by_dim(page_tile):
    return page_tile.reshape(page_tile.shape[0] * HEAD_DIM, page_tile.shape[2])


def _fold_heads(res, dec_seq, n_heads):
    lane_head = _iota((1, n_heads * HEAD_DIM), 1) // HEAD_DIM
    out = jnp.zeros((dec_seq, n_heads * HEAD_DIM), F32)
    for h in range(n_heads):
        out = out + jnp.where(lane_head == h, res[h * dec_seq:(h + 1) * dec_seq, :], 0.0)
    return out


def _sb_decode_kernel(pt_ref, qbd_ref, new_ref, *refs, n_steps, dec_seq):
    k_refs = refs[:PAGES_PER_STEP]
    v_refs = refs[PAGES_PER_STEP:2 * PAGES_PER_STEP]
    o_ref, kmean_ref, c_ref, acc_ref, kpad_ref, vpad_ref = refs[2 * PAGES_PER_STEP:]
    p = pl.program_id(1)
    qbd = (qbd_ref[0] * ATTN_SCALE).astype(BF16)
    rows = qbd.shape[0]
    later = _later_matrix(PAGE_SIZE)
    n_blk = n_steps * PAGES_PER_STEP // PAGES_PER_MOBA_BLOCK

    @pl.when(p == 0)
    def _():
        kmean_ref[...] = jnp.zeros_like(kmean_ref)
        _load_new_rows(kpad_ref, new_ref, 0, SB_WIDTH, dec_seq)
        _load_new_rows(vpad_ref, new_ref, KV0_V_OFF, SB_WIDTH, dec_seq)
        valid = _iota((1, PAGE_SIZE), 1) < lax.rem(_iota((rows, 1), 0), dec_seq)
        ls, lk = _log_sigmoid_pair(_dot_nt(qbd, kpad_ref[...].astype(BF16)))
        lk = jnp.where(valid, lk, 0.0)
        w = jnp.where(valid, jnp.exp(ls + _dot_exact_rhs(lk, later)), 0.0)
        acc_ref[...] = _dot(w.astype(BF16), vpad_ref[...].astype(BF16))
        c_ref[...] = jnp.sum(lk, axis=1, keepdims=True)

    zs = [_dot(qbd, _heads_by_dim(k_ref[0, 0, 0:H_SB]).astype(BF16)) for k_ref in k_refs]
    c = c_ref[...]
    acc = acc_ref[...]
    for z, v_ref in zip(zs, v_refs):
        ls, lk = _log_sigmoid_pair(z)
        w = jnp.exp(ls + _dot_exact_rhs(lk, later) + c)
        acc = acc + _dot_nt(w.astype(BF16), _heads_by_dim(v_ref[0, 0]).astype(BF16))
        c = c + jnp.sum(lk, axis=1, keepdims=True)
    c_ref[...] = c
    acc_ref[...] = acc

    lane = _iota((1, LANES), 1)
    kmean = kmean_ref[0]
    for j in range(PAGES_PER_STEP // PAGES_PER_MOBA_BLOCK):
        pages = jnp.zeros(((N_HEADS - H_SB) * HEAD_DIM, PAGE_SIZE), F32)
        for k_ref in k_refs[j * PAGES_PER_MOBA_BLOCK:(j + 1) * PAGES_PER_MOBA_BLOCK]:
            pages = pages + _heads_by_dim(k_ref[0, 0, H_SB:N_HEADS])
        total = jnp.sum(pages, axis=1, keepdims=True)
        blk = n_blk - 1 - (p * (PAGES_PER_STEP // PAGES_PER_MOBA_BLOCK) + j)
        kmean = jnp.where(lane == blk, total * (1.0 / MOBA_BLOCK), kmean)
    kmean_ref[0] = kmean

    @pl.when(p == n_steps - 1)
    def _():
        o_ref[0] = _fold_heads(acc_ref[...], dec_seq, H_SB)


def sb_decode(qbd, new_kv, pool, page_table):
    b, n_pages = page_table.shape
    dec_seq = new_kv.shape[1]
    rows = qbd.shape[1]
    assert n_pages % PAGES_PER_STEP == 0 and PAGES_PER_STEP % PAGES_PER_MOBA_BLOCK == 0
    assert n_pages // PAGES_PER_MOBA_BLOCK <= LANES

    def page_of(bi, p, pt, g):
        return pt[bi, n_pages - 1 - (p * PAGES_PER_STEP + g)]

    grid_spec = pltpu.PrefetchScalarGridSpec(
        num_scalar_prefetch=1,
        grid=(b, n_pages // PAGES_PER_STEP),
        in_specs=[pl.BlockSpec((1, rows, SB_WIDTH), lambda bi, p, pt: (bi, 0, 0)),
                  pl.BlockSpec((1, dec_seq, new_kv.shape[2]), lambda bi, p, pt: (bi, 0, 0))]
        + _page_specs((1, 1, N_HEADS, HEAD_DIM, PAGE_SIZE), lambda bi, p, pt, g: (page_of(bi, p, pt, g), 0, 0, 0, 0))
        + _page_specs((1, 1, H_SB, HEAD_DIM, PAGE_SIZE), lambda bi, p, pt, g: (page_of(bi, p, pt, g), 1, 0, 0, 0)),
        out_specs=[pl.BlockSpec((1, dec_seq, SB_WIDTH), lambda bi, p, pt: (bi, 0, 0)),
                   pl.BlockSpec((1, SB_WIDTH, LANES), lambda bi, p, pt: (bi, 0, 0))],
        scratch_shapes=[pltpu.VMEM((rows, 1), F32), pltpu.VMEM((rows, SB_WIDTH), F32),
                        pltpu.VMEM((PAGE_SIZE, SB_WIDTH), F32), pltpu.VMEM((PAGE_SIZE, SB_WIDTH), F32)],
    )
    return pl.pallas_call(
        functools.partial(_sb_decode_kernel, n_steps=n_pages // PAGES_PER_STEP, dec_seq=dec_seq),
        grid_spec=grid_spec,
        out_shape=[jax.ShapeDtypeStruct((b, dec_seq, SB_WIDTH), F32),
                   jax.ShapeDtypeStruct((b, SB_WIDTH, LANES), F32)],
        compiler_params=_params("parallel", "arbitrary"),
        name="sb_decode",
    )(page_table, qbd, new_kv, *([pool] * (2 * PAGES_PER_STEP)))


def _moba_decode_kernel(pt_ref, qbd_ref, kmean_ref, new_ref, *refs, n_steps, dec_seq, past_len):
    kv_refs = refs[:PAGES_PER_STEP]
    o_ref, sel_ref, m_ref, l_ref, acc_ref, kpad_ref, vpad_ref = refs[PAGES_PER_STEP:]
    p = pl.program_id(1)
    n_blk = n_steps * PAGES_PER_STEP // PAGES_PER_MOBA_BLOCK
    qf = qbd_ref[0]
    qb = (qf * ATTN_SCALE).astype(BF16)
    rows = qf.shape[0]
    t_of_row = lax.rem(_iota((rows, 1), 0), dec_seq)
    blk_id = _iota((1, LANES), 1)

    @pl.when(p == 0)
    def _():
        q_hi, q_lo = _split_bf16(qf)
        k_hi, k_lo = _split_bf16(kmean_ref[0])
        gate = _dot(q_hi, k_hi) + _dot(q_hi, k_lo) + _dot(q_lo, k_hi)
        fully_past = (blk_id < (past_len + t_of_row) // MOBA_BLOCK) & (blk_id < n_blk)
        gate = jnp.where(fully_past, gate, -jnp.inf)
        sel_ref[...] = jnp.where(fully_past & (_top_rank(gate, n_blk, 1) < MOBA_TOPK), 1.0, 0.0)
        _osm_reset(m_ref, l_ref, acc_ref)

    sel = sel_ref[...]
    s_parts, mask_parts = [], []
    for g, kv_ref in enumerate(kv_refs):
        s_parts.append(_dot(qb, _heads_by_dim(kv_ref[0, 0]).astype(BF16)))
        blk = (p * PAGES_PER_STEP + g) // PAGES_PER_MOBA_BLOCK
        picked = jnp.sum(jnp.where(blk_id == blk, sel, 0.0), axis=1, keepdims=True) > 0.5
        mask_parts.append(jnp.broadcast_to(picked, (rows, PAGE_SIZE)))

    def weigh_pages(pb):
        out = jnp.zeros((rows, SB_WIDTH), F32)
        for g, kv_ref in enumerate(kv_refs):
            out = out + _dot_nt(pb[:, g * PAGE_SIZE:(g + 1) * PAGE_SIZE], _heads_by_dim(kv_ref[0, 1]).astype(BF16))
        return out

    st = _osm_update((m_ref[...], l_ref[...], acc_ref[...]), jnp.concatenate(s_parts, axis=1),
                     jnp.concatenate(mask_parts, axis=1), weigh_pages)
    m_ref[...], l_ref[...], acc_ref[...] = st

    @pl.when(p == n_steps - 1)
    def _():
        _load_new_rows(kpad_ref, new_ref, SB_WIDTH, SB_WIDTH, dec_seq)
        _load_new_rows(vpad_ref, new_ref, KV0_V_OFF + SB_WIDTH, SB_WIDTH, dec_seq)
        key = _iota((1, PAGE_SIZE), 1)
        vb = vpad_ref[...].astype(BF16)
        fin = _osm_update((m_ref[...], l_ref[...], acc_ref[...]), _dot_nt(qb, kpad_ref[...].astype(BF16)),
                          (key <= t_of_row) & (key < dec_seq), lambda pb: _dot(pb, vb))
        o_ref[0] = _fold_heads(_osm_final(fin), dec_seq, N_HEADS - H_SB)


def moba_decode(qbd, kmean, new_kv, pool, page_table, past_len):
    b, n_pages = page_table.shape
    dec_seq = new_kv.shape[1]
    rows = qbd.shape[1]
    assert past_len % MOBA_BLOCK == 0 and n_pages * PAGE_SIZE == past_len and dec_seq <= PAGE_SIZE
    assert n_pages % PAGES_PER_STEP == 0
    grid_spec = pltpu.PrefetchScalarGridSpec(
        num_scalar_prefetch=1,
        grid=(b, n_pages // PAGES_PER_STEP),
        in_specs=[pl.BlockSpec((1, rows, SB_WIDTH), lambda bi, p, pt: (bi, 0, 0)),
                  pl.BlockSpec((1, SB_WIDTH, LANES), lambda bi, p, pt: (bi, 0, 0)),
                  pl.BlockSpec((1, dec_seq, new_kv.shape[2]), lambda bi, p, pt: (bi, 0, 0))]
        + _page_specs((1, 2, N_HEADS - H_SB, HEAD_DIM, PAGE_SIZE),
                      lambda bi, p, pt, g: (pt[bi, p * PAGES_PER_STEP + g], 0, 1, 0, 0)),
        out_specs=pl.BlockSpec((1, dec_seq, SB_WIDTH), lambda bi, p, pt: (bi, 0, 0)),
        scratch_shapes=[pltpu.VMEM((rows, LANES), F32), pltpu.VMEM((rows, 1), F32), pltpu.VMEM((rows, 1), F32),
                        pltpu.VMEM((rows, SB_WIDTH), F32),
                        pltpu.VMEM((PAGE_SIZE, SB_WIDTH), F32), pltpu.VMEM((PAGE_SIZE, SB_WIDTH), F32)],
    )
    return pl.pallas_call(
        functools.partial(_moba_decode_kernel, n_steps=n_pages // PAGES_PER_STEP, dec_seq=dec_seq, past_len=past_len),
        grid_spec=grid_spec,
        out_shape=jax.ShapeDtypeStruct((b, dec_seq, SB_WIDTH), F32),
        compiler_params=_params("parallel", "arbitrary"),
        name="moba_decode",
    )(page_table, qbd, kmean, new_kv, *([pool] * PAGES_PER_STEP))


KV_WIDTH = NSA_HKV * HEAD_DIM


def _nsa_decode_kernel(pt_ref, qbd_ref, kc_ref, vc_ref, state_ref, new_kv_ref, new_win_ref, g_ref, *refs,
                       n_steps, dec_seq, past_len):
    page_refs = refs[:PAGES_PER_STEP]
    o_ref, sel_ref, oc_ref, ow_ref, m_ref, l_ref, acc_ref, kpad_ref, vpad_ref = refs[PAGES_PER_STEP:]
    p = pl.program_id(1)
    qb = (qbd_ref[0] * ATTN_SCALE).astype(BF16)
    rows = qb.shape[0]
    grp_rows = NSA_HKV * dec_seq
    n_cmp_rows = kc_ref.shape[1]
    n_slc = past_len // SLC_BLOCK + 1
    slc_lanes = sel_ref.shape[1]
    t_of_row = lax.rem(_iota((rows, 1), 0), dec_seq)
    qpos = past_len + t_of_row
    key = _iota((1, PAGE_SIZE), 1)

    @pl.when(p == 0)
    def _():
        s_c = _dot_nt(qb, kc_ref[0].astype(BF16))
        ok_c = (CMP_STRIDE * _iota((1, n_cmp_rows), 1) + CMP_BLOCK - 1) <= qpos
        sm = jnp.where(ok_c, s_c, NEG_BIG)
        pc = jnp.where(ok_c, jnp.exp(sm - jnp.max(sm, axis=1, keepdims=True)), 0.0)
        pc = pc / jnp.maximum(jnp.sum(pc, axis=1, keepdims=True), 1e-30)
        oc_ref[...] = _dot(pc.astype(BF16), vc_ref[0].astype(BF16))
        psum = pc[0:grp_rows]
        for g in range(1, NSA_GROUP):
            psum = psum + pc[g * grp_rows:(g + 1) * grp_rows]
        imp = _dot_exact_rhs(psum, _cover_matrix(n_cmp_rows, slc_lanes, 0))
        score = _slc_score(imp, qpos[0:grp_rows] // SLC_BLOCK, n_slc, 1)
        sel = jnp.where(_top_rank(score, n_slc, 1) < min(SLC_TOPK, n_slc), 1.0, 0.0)
        sel_ref[...] = jnp.concatenate([sel] * NSA_GROUP, axis=0)
        vwt = _heads_by_dim(state_ref[0, 1]).astype(BF16)
        st = _osm_update(_osm_init(rows, KV_WIDTH), _dot(qb, _heads_by_dim(state_ref[0, 0]).astype(BF16)),
                         _iota((1, WINDOW), 1) > t_of_row, lambda pb: _dot_nt(pb, vwt))
        _load_new_rows(kpad_ref, new_win_ref, 0, KV_WIDTH, dec_seq)
        _load_new_rows(vpad_ref, new_win_ref, KV_WIDTH, KV_WIDTH, dec_seq)
        vb = vpad_ref[...].astype(BF16)
        st = _osm_update(st, _dot_nt(qb, kpad_ref[...].astype(BF16)), (key <= t_of_row) & (key < dec_seq),
                         lambda pb: _dot(pb, vb))
        ow_ref[...] = _osm_final(st)
        _osm_reset(m_ref, l_ref, acc_ref)

    keys_per_step = PAGES_PER_STEP * PAGE_SIZE
    blk_of_key = p * (keys_per_step // SLC_BLOCK) + _iota((slc_lanes, keys_per_step), 1) // SLC_BLOCK
    expand = jnp.where(_iota((slc_lanes, keys_per_step), 0) == blk_of_key, 1.0, 0.0).astype(BF16)
    s_all = jnp.concatenate([_dot(qb, _heads_by_dim(ref[0, 0]).astype(BF16)) for ref in page_refs], axis=1)

    def weigh_pages(pb):
        out = jnp.zeros((rows, KV_WIDTH), F32)
        for g, ref in enumerate(page_refs):
            out = out + _dot_nt(pb[:, g * PAGE_SIZE:(g + 1) * PAGE_SIZE], _heads_by_dim(ref[0, 1]).astype(BF16))
        return out

    st = _osm_update((m_ref[...], l_ref[...], acc_ref[...]), s_all,
                     _dot(sel_ref[...].astype(BF16), expand) > 0.5, weigh_pages)
    m_ref[...], l_ref[...], acc_ref[...] = st

    @pl.when(p == n_steps - 1)
    def _():
        _load_new_rows(kpad_ref, new_kv_ref, 2 * KV_WIDTH, KV_WIDTH, dec_seq)
        _load_new_rows(vpad_ref, new_kv_ref, 3 * KV_WIDTH, KV_WIDTH, dec_seq)
        cur = past_len // SLC_BLOCK
        picked = sel_ref[:, cur:cur + 1] > 0.5
        vb = vpad_ref[...].astype(BF16)
        fin = _osm_update((m_ref[...], l_ref[...], acc_ref[...]), _dot_nt(qb, kpad_ref[...].astype(BF16)),
                          picked & (key <= t_of_row) & (key < dec_seq), lambda pb: _dot(pb, vb))
        o_s = _osm_final(fin)
        o_c = oc_ref[...]
        o_w = ow_ref[...]
        gates = g_ref[0]
        for hk in range(NSA_HKV):
            for g in range(NSA_GROUP):
                rs = slice(g * grp_rows + hk * dec_seq, g * grp_rows + (hk + 1) * dec_seq)
                hs = slice(hk * HEAD_DIM, (hk + 1) * HEAD_DIM)
                gc = hk * LANES + g * 3
                h = hk * NSA_GROUP + g
                o_ref[0, :, h * HEAD_DIM:(h + 1) * HEAD_DIM] = (
                    gates[:, gc:gc + 1] * o_c[rs, hs] + gates[:, gc + 1:gc + 2] * o_s[rs, hs]
                    + gates[:, gc + 2:gc + 3] * o_w[rs, hs])


def nsa_decode(qbd, kc, vc, state_win, new_kv, new_win, gates, pool, page_table, past_len):
    b, n_pages = page_table.shape
    dec_seq = new_kv.shape[1]
    rows = qbd.shape[1]
    n_chunk = kc.shape[1]
    n_slc = past_len // SLC_BLOCK + 1
    slc_lanes = -(-n_slc // LANES) * LANES
    assert past_len % SLC_BLOCK == 0 and past_len >= WINDOW and state_win.shape[-1] == WINDOW and dec_seq <= SLC_BLOCK
    assert n_pages % PAGES_PER_STEP == 0
    grid_spec = pltpu.PrefetchScalarGridSpec(
        num_scalar_prefetch=1,
        grid=(b, n_pages // PAGES_PER_STEP),
        in_specs=[pl.BlockSpec((1, rows, KV_WIDTH), lambda bi, p, pt: (bi, 0, 0)),
                  pl.BlockSpec((1, n_chunk, KV_WIDTH), lambda bi, p, pt: (bi, 0, 0)),
                  pl.BlockSpec((1, n_chunk, KV_WIDTH), lambda bi, p, pt: (bi, 0, 0)),
                  pl.BlockSpec((1, 2, NSA_HKV, HEAD_DIM, WINDOW), lambda bi, p, pt: (bi, 0, 0, 0, 0)),
                  pl.BlockSpec((1, dec_seq, 4 * KV_WIDTH), lambda bi, p, pt: (bi, 0, 0)),
                  pl.BlockSpec((1, dec_seq, 2 * KV_WIDTH), lambda bi, p, pt: (bi, 0, 0)),
                  pl.BlockSpec((1, dec_seq, NSA_HKV * LANES), lambda bi, p, pt: (bi, 0, 0))]
        + _page_specs((1, 2, NSA_HKV, HEAD_DIM, PAGE_SIZE),
                      lambda bi, p, pt, g: (pt[bi, p * PAGES_PER_STEP + g], 1, 0, 0, 0)),
        out_specs=pl.BlockSpec((1, dec_seq, N_HEADS * HEAD_DIM), lambda bi, p, pt: (bi, 0, 0)),
        scratch_shapes=[pltpu.VMEM((rows, slc_lanes), F32), pltpu.VMEM((rows, KV_WIDTH), F32),
                        pltpu.VMEM((rows, KV_WIDTH), F32), pltpu.VMEM((rows, 1), F32), pltpu.VMEM((rows, 1), F32),
                        pltpu.VMEM((rows, KV_WIDTH), F32),
                        pltpu.VMEM((PAGE_SIZE, KV_WIDTH), F32), pltpu.VMEM((PAGE_SIZE, KV_WIDTH), F32)],
    )
    return pl.pallas_call(
        functools.partial(_nsa_decode_kernel, n_steps=n_pages // PAGES_PER_STEP, dec_seq=dec_seq, past_len=past_len),
        grid_spec=grid_spec,
        out_shape=jax.ShapeDtypeStruct((b, dec_seq, N_HEADS * HEAD_DIM), F32),
        compiler_params=_params("parallel", "arbitrary"),
        name="nsa_decode",
    )(page_table, qbd, kc, vc, state_win, new_kv, new_win, gates, *([pool] * PAGES_PER_STEP))


EVEN_PLAN = (
    (0, 512, "copy", 0, 0, 0),
    (512, 512, "rope", 0, 0, 512),
    (1024, 512, "copy", 0, 1, 0),
    (1536, 512, "rope", 1, 1, 512),
    (2048, 1024, "copy", 0, 1, 1024),
)
ODD_PLAN = (
    (0, 1024, "rope", 0, 0, 0),
    (1024, 256, "rope", 1, 1, 0),
    (1280, 256, "copy", 0, 1, 256),
    (1536, 256, "rope", 2, 1, 512),
    (1792, 256, "copy", 0, 1, 768),
    (2048, 256, "rope", 3, 2, 0),
    (2304, 256, "copy", 0, 2, 256),
    (2560, 512, "sigmoid", 0, 3, 0),
)
ODD_QKV = N_HEADS * HEAD_DIM + 6 * NSA_HKV * HEAD_DIM


def _gate_columns():
    idx = np.full((NSA_HKV * LANES,), ODD_QKV + 3 * N_HEADS, np.int32)
    for h in range(N_HEADS):
        for r in range(3):
            idx[(h // NSA_GROUP) * LANES + (h % NSA_GROUP) * 3 + r] = ODD_QKV + h * 3 + r
    return idx


def _rope_tables(pos):
    half = HEAD_DIM // 2
    inv = jnp.power(ROPE_THETA, -jnp.arange(half, dtype=F32) / half)
    ang = pos.astype(F32)[:, None] * inv[None, :]
    cos, sin = jnp.cos(ang), jnp.sin(ang)
    return (jnp.concatenate([cos] * (LANES // half), axis=1),
            jnp.concatenate([-sin, sin] * (LANES // HEAD_DIM), axis=1))


def _gain_rows(*gains):
    return jnp.stack([jnp.tile(g.astype(F32), LANES // HEAD_DIM) for g in gains])


def _block_diag_queries(q, heads_per_group, n_groups, group_major):
    b, dec, n_heads, _ = q.shape
    h = np.arange(n_heads)
    grp = h // heads_per_group
    onehot = jnp.asarray(np.eye(n_groups, dtype=np.float32)[grp])
    bd = q.transpose(0, 2, 1, 3)[:, :, :, None, :] * onehot[None, :, None, :, None]
    if group_major:
        bd = bd.reshape(b, n_groups, heads_per_group, dec, n_groups, HEAD_DIM).transpose(0, 2, 1, 3, 4, 5)
    return bd.reshape(b, n_heads * dec, n_groups * HEAD_DIM)


def _prepare_weights(w_in0, w_out0, w_in1, w_out1, cmp_w1, cmp_w2, w_gu, w_down):
    w_in1_ext = jnp.concatenate([w_in1[0], jnp.zeros((D_MODEL, 1), w_in1.dtype)], axis=1)
    cols = np.concatenate([np.arange(ODD_QKV, dtype=np.int32), _gate_columns()])
    return dict(
        w_in0=w_in0[0].astype(BF16),
        w_out0_sb=w_out0[0, :SB_WIDTH].astype(BF16),
        w_out0_mb=w_out0[0, SB_WIDTH:].astype(BF16),
        w_in1=w_in1_ext[:, cols].astype(BF16),
        w_out1=w_out1[0].astype(BF16),
        cmp_w1=cmp_w1[0].astype(BF16),
        cmp_w2=cmp_w2[0].astype(BF16),
        w_gu=w_gu.astype(BF16),
        w_down=w_down.astype(BF16),
    )


def _cmp_pe_rows(cmp_pe):
    return cmp_pe[0].reshape(2, CMP_BLOCK // CMP_STRIDE, CMP_HALF)


def _tokens_last(x):
    return x.transpose(0, 2, 3, 4, 1)


def _tokens_first(x):
    return x.transpose(0, 4, 1, 2, 3)


def _trunk(x, q0, caches, w, g_mix0, g_q0, g_k0, g_mix1, g_q1, g_k1, cmp_pe, g_ffn):
    b, t, _ = x.shape
    m = b * t
    tm = min(512, m)
    xf = x.reshape(m, D_MODEL)
    pos = q0 + jnp.arange(t)
    cos, sin = _rope_tables(pos)
    if m // tm * tm != m or t % tm != 0:
        cos, sin = jnp.tile(cos, (m // t, 1)), jnp.tile(sin, (m // t, 1))

    prompt = caches is None
    q_l0, kv_l0, *kv0_t = norm_proj_post(xf, g_mix0[0], w["w_in0"], cos, sin, _gain_rows(g_q0[0], g_k0[0]), EVEN_PLAN,
                                         (1024, 2048), tm, ((1, (2, N_HEADS)),) if prompt else (), t)
    kv0_out = _tokens_first(kv0_t[0]) if prompt else kv_l0.reshape(b, t, 2, N_HEADS, HEAD_DIM)
    if caches is None:
        o_sb = sb_prefill(q_l0.reshape(b, t, 1024), kv_l0.reshape(b, t, 2048)).reshape(m, SB_WIDTH)
        o_mb = moba_prefill(q_l0.reshape(b, t, 1024), kv_l0.reshape(b, t, 2048)).reshape(m, SB_WIDTH)
    else:
        pool0, pool1, state_win, page_table = caches
        q4 = q_l0.reshape(b, t, N_HEADS, HEAD_DIM)
        new_kv0 = kv_l0.reshape(b, t, 2048)
        o_sb, kmean = sb_decode(_block_diag_queries(q4[:, :, :H_SB], 1, H_SB, False), new_kv0, pool0, page_table)
        o_mb = moba_decode(_block_diag_queries(q4[:, :, H_SB:], 1, N_HEADS - H_SB, False), kmean, new_kv0, pool0,
                           page_table, q0)
        o_sb, o_mb = o_sb.reshape(m, SB_WIDTH), o_mb.reshape(m, SB_WIDTH)
    h2 = mixer_out_ffn([o_sb, o_mb], [w["w_out0_sb"], w["w_out0_mb"]], xf, g_ffn[0], w["w_gu"][0], w["w_down"][0],
                       tm, D_FF // 2)

    gains1 = _gain_rows(g_q1[0], g_k1[0, 0], g_k1[0, 1], g_k1[0, 2])
    q_l1, kv_l1, win_l1, gates, *kv1_t = norm_proj_post(h2, g_mix1[0], w["w_in1"], cos, sin, gains1, ODD_PLAN,
                                                        (1024, 1024, 512, 512), tm,
                                                        ((1, (4, NSA_HKV)),) if prompt else (), t)
    kv1_out = _tokens_first(kv1_t[0]) if prompt else kv_l1.reshape(b, t, 4, NSA_HKV, HEAD_DIM)
    pe_rows = _cmp_pe_rows(cmp_pe)
    new_win = win_l1.reshape(b, t, 2, NSA_HKV, HEAD_DIM)
    if caches is None:
        kc, vc = compress_pages(kv1_t[0], None, pe_rows, w["cmp_w1"], w["cmp_w2"])
        o_nsa = nsa_prefill(q_l1.reshape(b, t, 1024), kv_l1.reshape(b, t, 1024), win_l1.reshape(b, t, 512),
                            kc, vc, gates.reshape(b, t, 512)).reshape(m, 1024)
        win_state = new_win[:, t - min(WINDOW, t):]
    else:
        kc, vc = compress_pages(pool1, page_table, pe_rows, w["cmp_w1"], w["cmp_w2"])
        qbd1 = _block_diag_queries(q_l1.reshape(b, t, N_HEADS, HEAD_DIM), NSA_GROUP, NSA_HKV, True)
        o_nsa = nsa_decode(qbd1, kc, vc, _tokens_last(state_win), kv_l1.reshape(b, t, 1024), win_l1.reshape(b, t, 512),
                           gates.reshape(b, t, 512), pool1, page_table, q0).reshape(m, 1024)
        win_state = jnp.concatenate([state_win[:, t:], new_win], axis=1)
    y = mixer_out_ffn([o_nsa], [w["w_out1"]], h2, g_ffn[1], w["w_gu"][1], w["w_down"][1], tm, D_FF // 2)

    return (y.reshape(b, t, D_MODEL), kv0_out[None], kv1_out[None], win_state[None])


def kernel(x_prompt, x_sample, cache_kv0, cache_kv1, state_win, page_table, g_mix0, w_in0, g_q0, g_k0, w_out0,
           g_mix1, w_in1, g_q1, g_k1, cmp_pe, cmp_w1, cmp_w2, w_out1, g_ffn, w_gu, w_down):
    assert w_in0.shape[0] == 1 and w_in1.shape[0] == 1, "one even and one odd layer"
    w = _prepare_weights(w_in0, w_out0, w_in1, w_out1, cmp_w1, cmp_w2, w_gu, w_down)
    norms = (g_mix0, g_q0, g_k0, g_mix1, g_q1, g_k1, cmp_pe, g_ffn)
    y_p, kv0_p, kv1_p, win_p = _trunk(x_prompt, 0, None, w, *norms)
    past_len = page_table.shape[1] * cache_kv0.shape[2]
    caches = (_tokens_last(cache_kv0[0]), _tokens_last(cache_kv1[0]), state_win[0], page_table)
    y_s, kv0_s, kv1_s, win_s = _trunk(x_sample, past_len, caches, w, *norms)
    return (y_p, y_s, kv0_p, kv0_s, kv1_p, kv1_s, win_p, win_s)
```

```python
import functools

import numpy as np
import jax
import jax.numpy as jnp
from jax import lax
from jax.experimental import pallas as pl
from jax.experimental.pallas import tpu as pltpu

F32 = jnp.float32
BF16 = jnp.bfloat16

D_MODEL = 1024
HEAD_DIM = 64
N_HEADS = 16
H_SB = 8
PAGE_SIZE = 128
MOBA_BLOCK = 256
MOBA_TOPK = 3
NSA_HKV = 4
NSA_GROUP = 4
CMP_BLOCK = 32
CMP_STRIDE = 16
CMP_HIDDEN = 256
SLC_BLOCK = 64
SLC_TOPK = 16
WINDOW = 512
D_FF = 2816
ROPE_THETA = 10000.0
NORM_EPS = 1e-6
NEG_BIG = -1e30
FORCE_SCORE = 1e9
ATTN_SCALE = HEAD_DIM ** -0.5

LANES = 128
HEADS_PER_SLAB = LANES // HEAD_DIM
VMEM_LIMIT_BYTES = 56 * 1024 * 1024


def _params(*sem):
    return pltpu.CompilerParams(dimension_semantics=sem, vmem_limit_bytes=VMEM_LIMIT_BYTES)


def _iota(shape, dim):
    return lax.broadcasted_iota(jnp.int32, shape, dim)


def _dot(a, b):
    return jnp.dot(a, b, preferred_element_type=F32)


def _dot_nt(a, b):
    return lax.dot_general(a, b, (((1,), (1,)), ((), ())), preferred_element_type=F32)


def _split_bf16(x):
    hi = x.astype(BF16)
    lo = (x - hi.astype(F32)).astype(BF16)
    return hi, lo


def _dot_exact_rhs(x, rhs_bf16):
    hi, lo = _split_bf16(x)
    return _dot(hi, rhs_bf16) + _dot(lo, rhs_bf16)


def _log_sigmoid_pair(z):
    t = jnp.log(1.0 + jnp.exp(-jnp.abs(z)))
    return jnp.minimum(z, 0.0) - t, jnp.minimum(-z, 0.0) - t


def _later_matrix(n):
    return jnp.where(_iota((n, n), 0) > _iota((n, n), 1), 1.0, 0.0).astype(BF16)


def _osm_init(rows, width):
    return (jnp.full((rows, 1), NEG_BIG, F32), jnp.zeros((rows, 1), F32), jnp.zeros((rows, width), F32))


def _osm_update(state, s, mask, weigh):
    m, l, acc = state
    sm = jnp.where(mask, s, NEG_BIG)
    m_new = jnp.maximum(m, jnp.max(sm, axis=1, keepdims=True))
    p = jnp.where(mask, jnp.exp(sm - m_new), 0.0)
    a = jnp.exp(m - m_new)
    return (m_new, a * l + jnp.sum(p, axis=1, keepdims=True), a * acc + weigh(p.astype(BF16)))


def _osm_final(state):
    _, l, acc = state
    return acc / jnp.maximum(l, 1e-30)


def _osm_reset(m_ref, l_ref, acc_ref):
    m_ref[...] = jnp.full_like(m_ref, NEG_BIG)
    l_ref[...] = jnp.zeros_like(l_ref)
    acc_ref[...] = jnp.zeros_like(acc_ref)


def _osm_update_biased(state, s_biased, weigh):
    m, l, acc = state
    m_new = jnp.maximum(m, jnp.max(s_biased, axis=1, keepdims=True))
    p = jnp.exp(s_biased - m_new)
    a = jnp.exp(m - m_new)
    return (m_new, a * l + jnp.sum(p, axis=1, keepdims=True), a * acc + weigh(p.astype(BF16)))


def _top_rank(score, n_real, axis):
    j = _iota(score.shape, axis)
    rank = jnp.zeros(score.shape, jnp.int32)
    for m in range(n_real):
        other = score[:, m:m + 1] if axis == 1 else score[m:m + 1, :]
        ahead = (other > score) | ((other == score) & (j > m))
        rank = rank + jnp.where(ahead, 1, 0)
    return rank


def _norm_proj_post_kernel(x_ref, g_ref, w_ref, cos_ref, sin_ref, gain_ref, *refs, plan, n_out, t_plan):
    out_refs, t_refs, p_ref = refs[:n_out], refs[n_out:-1], refs[-1]
    x = x_ref[...]
    ms = jnp.mean(x * x, axis=-1, keepdims=True)
    p_ref[...] = _dot((x * lax.rsqrt(ms + NORM_EPS) * g_ref[...]).astype(BF16), w_ref[...])
    lane = _iota((1, LANES), 1)
    first_half = (lane & (HEAD_DIM - 1)) < HEAD_DIM // 2
    same_head = (_iota((LANES, LANES), 0) // HEAD_DIM) == (_iota((LANES, LANES), 1) // HEAD_DIM)
    head_mean = jnp.where(same_head, 1.0 / HEAD_DIM, 0.0).astype(BF16)
    cos = cos_ref[...]
    sin = sin_ref[...]
    for src, width, mode, gi, oi, dst in plan:
        if mode == "copy":
            out_refs[oi][:, dst:dst + width] = p_ref[:, src:src + width]
        elif mode == "sigmoid":
            x = p_ref[:, src:src + width]
            out_refs[oi][:, dst:dst + width] = 1.0 / (1.0 + jnp.exp(-x))
        else:
            for s in range(width // LANES):
                x = p_ref[:, src + s * LANES:src + (s + 1) * LANES]
                ms = _dot_exact_rhs(x * x, head_mean)
                y = x * lax.rsqrt(ms + NORM_EPS) * gain_ref[gi:gi + 1, :]
                other = jnp.where(first_half, pltpu.roll(y, LANES - HEAD_DIM // 2, 1),
                                  pltpu.roll(y, HEAD_DIM // 2, 1))
                out_refs[oi][:, dst + s * LANES:dst + (s + 1) * LANES] = y * cos + other * sin
    for t_ref, (oi, lead) in zip(t_refs, t_plan):
        t_ref[0] = out_refs[oi][...].T.reshape(lead + (HEAD_DIM, x_ref.shape[0]))


def norm_proj_post(x, g, w_bf16, cos, sin, gains, plan, out_widths, tm, t_plan=(), seq_len=None):
    m, k = x.shape
    n = w_bf16.shape[1]
    period = cos.shape[0] // tm
    tiles_per_seq = seq_len // tm if t_plan else 1
    t_specs = [pl.BlockSpec((1,) + lead + (HEAD_DIM, tm), lambda i: (i // tiles_per_seq, 0, 0, 0, i % tiles_per_seq))
               for _, lead in t_plan]
    t_shapes = [jax.ShapeDtypeStruct((m // seq_len,) + lead + (HEAD_DIM, seq_len), F32) for _, lead in t_plan]
    return pl.pallas_call(
        functools.partial(_norm_proj_post_kernel, plan=plan, n_out=len(out_widths), t_plan=t_plan),
        grid=(m // tm,),
        in_specs=[pl.BlockSpec((tm, k), lambda i: (i, 0)),
                  pl.BlockSpec((1, k), lambda i: (0, 0)),
                  pl.BlockSpec((k, n), lambda i: (0, 0)),
                  pl.BlockSpec((tm, LANES), lambda i: (i % period, 0)),
                  pl.BlockSpec((tm, LANES), lambda i: (i % period, 0)),
                  pl.BlockSpec(gains.shape, lambda i: (0, 0))],
        out_specs=[pl.BlockSpec((tm, w), lambda i: (i, 0)) for w in out_widths] + t_specs,
        out_shape=[jax.ShapeDtypeStruct((m, w), F32) for w in out_widths] + t_shapes,
        scratch_shapes=[pltpu.VMEM((tm, n), F32)],
        compiler_params=_params("parallel"),
        name="norm_proj_post",
    )(x, g.reshape(1, k), w_bf16, cos, sin, gains)


def _mixer_out_ffn_kernel(*refs, n_in):
    a_refs, wo_refs = refs[:n_in], refs[n_in:2 * n_in]
    h_ref, g_ref, wg_ref, wu_ref, wd_ref, o_ref, xn_ref, acc_ref = refs[2 * n_in:]
    j = pl.program_id(1)

    @pl.when(j == 0)
    def _():
        x = h_ref[...]
        for a_ref, wo_ref in zip(a_refs, wo_refs):
            x = x + _dot(a_ref[...].astype(BF16), wo_ref[...])
        ms = jnp.mean(x * x, axis=-1, keepdims=True)
        xn_ref[...] = (x * lax.rsqrt(ms + NORM_EPS) * g_ref[...]).astype(BF16)
        acc_ref[...] = x

    xn = xn_ref[...]
    gate = _dot(xn, wg_ref[...])
    up = _dot(xn, wu_ref[...])
    act = (gate / (1.0 + jnp.exp(-gate))) * up
    acc_ref[...] += _dot(act.astype(BF16), wd_ref[...])

    @pl.when(j == pl.num_programs(1) - 1)
    def _():
        o_ref[...] = acc_ref[...]


def mixer_out_ffn(a_list, wo_list, res, g, w_gu_bf16, w_down_bf16, tm, tf):
    m, k = res.shape
    n_chunks = D_FF // tf
    return pl.pallas_call(
        functools.partial(_mixer_out_ffn_kernel, n_in=len(a_list)),
        grid=(m // tm, n_chunks),
        in_specs=[pl.BlockSpec((tm, a.shape[1]), lambda i, j: (i, 0)) for a in a_list]
        + [pl.BlockSpec(wo.shape, lambda i, j: (0, 0)) for wo in wo_list]
        + [pl.BlockSpec((tm, k), lambda i, j: (i, 0)),
           pl.BlockSpec((1, k), lambda i, j: (0, 0)),
           pl.BlockSpec((k, tf), lambda i, j: (0, j)),
           pl.BlockSpec((k, tf), lambda i, j: (0, j + n_chunks)),
           pl.BlockSpec((tf, k), lambda i, j: (j, 0))],
        out_specs=pl.BlockSpec((tm, k), lambda i, j: (i, 0)),
        out_shape=jax.ShapeDtypeStruct((m, k), F32),
        scratch_shapes=[pltpu.VMEM((tm, k), BF16), pltpu.VMEM((tm, k), F32)],
        compiler_params=_params("parallel", "arbitrary"),
        name="mixer_out_ffn",
    )(*a_list, *wo_list, res, g.reshape(1, k), w_gu_bf16, w_gu_bf16, w_down_bf16)


ATTN_TQ = 256
ATTN_TK = 256


def _stage_keys_values(k_ref, k_cols, v_ref, v_cols, kb_ref, vt_ref, n_tiles, tk):
    kb_ref[...] = k_ref[0, :, k_cols].astype(BF16)
    for c in range(n_tiles):
        vt_ref[c] = v_ref[0, c * tk:(c + 1) * tk, v_cols].T.astype(BF16)


def _split_heads_t(qt, scale):
    row_head = _iota((LANES, 1), 0) // HEAD_DIM
    return [jnp.where(row_head == h, qt * scale, 0.0) for h in range(HEADS_PER_SLAB)]


def _osm_t_init(width, cols):
    return (jnp.full((1, cols), NEG_BIG, F32), jnp.zeros((1, cols), F32), jnp.zeros((width, cols), F32))


def _osm_t_update(state, s_biased_t, weigh):
    m, l, acc = state
    m_new = jnp.maximum(m, jnp.max(s_biased_t, axis=0, keepdims=True))
    p = jnp.exp(s_biased_t - m_new)
    a = jnp.exp(m - m_new)
    return (m_new, a * l + jnp.sum(p, axis=0, keepdims=True), a * acc + weigh(p.astype(BF16)))


def _osm_t_update2(state, s_a, s_b, weigh_a, weigh_b):
    m, l, acc = state
    m_new = jnp.maximum(m, jnp.maximum(jnp.max(s_a, axis=0, keepdims=True), jnp.max(s_b, axis=0, keepdims=True)))
    p_a = jnp.exp(s_a - m_new)
    p_b = jnp.exp(s_b - m_new)
    a = jnp.exp(m - m_new)
    l_new = a * l + jnp.sum(p_a, axis=0, keepdims=True) + jnp.sum(p_b, axis=0, keepdims=True)
    return (m_new, l_new, a * acc + weigh_a(p_a.astype(BF16)) + weigh_b(p_b.astype(BF16)))


def _osm_t_final(state):
    _, l, acc = state
    return acc / jnp.maximum(l, 1e-30)


def _weigh_heads_t(vt_ref, tile, w, tq):
    return jnp.concatenate([_dot(vt_ref[tile, h * HEAD_DIM:(h + 1) * HEAD_DIM, :], w[:, h * tq:(h + 1) * tq])
                            for h in range(HEADS_PER_SLAB)], axis=1)


def _unstack_heads_t(x, tq):
    return jnp.concatenate([x[:, h * tq:(h + 1) * tq] for h in range(HEADS_PER_SLAB)], axis=0)


def _sb_prefill_kernel(q_ref, k_ref, v_ref, o_ref, kb_ref, vt_ref, *, n_tiles):
    qi = pl.program_id(2)
    tq, tk = ATTN_TQ, ATTN_TK

    @pl.when(qi == 0)
    def _():
        _stage_keys_values(k_ref, slice(None), v_ref, slice(None), kb_ref, vt_ref, n_tiles, tk)

    qpos = jnp.concatenate([qi * tq + _iota((1, tq), 1)] * HEADS_PER_SLAB, axis=1)
    sooner = jnp.where(_iota((tk, tk), 0) < _iota((tk, tk), 1), 1.0, 0.0).astype(BF16)
    qs = jnp.concatenate(_split_heads_t(q_ref[0].T, ATTN_SCALE), axis=1).astype(BF16)

    def log_terms(kb):
        z = _dot(kb_ref[pl.ds(pl.multiple_of(kb * tk, tk), tk), :], qs)
        return _log_sigmoid_pair(z)

    def gap_within(lk):
        hi, lo = _split_bf16(lk)
        return _dot(sooner, hi) + _dot(sooner, lo)

    def diagonal_tile():
        past = (qi * tk + _iota((tk, 1), 0)) < qpos
        ls, lk = log_terms(qi)
        lk = jnp.where(past, lk, 0.0)
        w = jnp.where(past, jnp.exp(ls + gap_within(lk)), 0.0)
        return jnp.sum(lk, axis=0, keepdims=True), _weigh_heads_t(vt_ref, qi, w.astype(BF16), tq)

    def pair(i, carry):
        c, acc = carry
        ka = qi - 1 - 2 * i
        kb = jnp.maximum(ka - 1, 0)
        there = jnp.where(ka >= 1, 1.0, 0.0)
        ls_a, lk_a = log_terms(ka)
        ls_b, lk_b = log_terms(kb)
        lk_b = lk_b * there
        sum_a = jnp.sum(lk_a, axis=0, keepdims=True)
        w_a = jnp.exp(ls_a + gap_within(lk_a) + c)
        w_b = jnp.exp(ls_b + gap_within(lk_b) + (c + sum_a)) * there
        acc = acc + _weigh_heads_t(vt_ref, ka, w_a.astype(BF16), tq) + _weigh_heads_t(vt_ref, kb, w_b.astype(BF16), tq)
        return c + sum_a + jnp.sum(lk_b, axis=0, keepdims=True), acc

    _, acc = lax.fori_loop(0, (qi + 1) // 2, pair, diagonal_tile())
    o_ref[0] = _unstack_heads_t(acc, tq).T


def sb_prefill(q, kv):
    b, t, _ = q.shape
    n_slab = H_SB // HEADS_PER_SLAB
    v_off = N_HEADS // HEADS_PER_SLAB
    assert t % ATTN_TK == 0 and ATTN_TQ == ATTN_TK
    return pl.pallas_call(
        functools.partial(_sb_prefill_kernel, n_tiles=t // ATTN_TK),
        grid=(b, n_slab, t // ATTN_TQ),
        in_specs=[pl.BlockSpec((1, ATTN_TQ, LANES), lambda bi, s, qi: (bi, qi, s)),
                  pl.BlockSpec((1, t, LANES), lambda bi, s, qi: (bi, 0, s)),
                  pl.BlockSpec((1, t, LANES), lambda bi, s, qi: (bi, 0, v_off + s))],
        out_specs=pl.BlockSpec((1, ATTN_TQ, LANES), lambda bi, s, qi: (bi, qi, s)),
        out_shape=jax.ShapeDtypeStruct((b, t, H_SB * HEAD_DIM), F32),
        scratch_shapes=[pltpu.VMEM((t, LANES), BF16), pltpu.VMEM((t // ATTN_TK, LANES, ATTN_TK), BF16)],
        compiler_params=_params("parallel", "parallel", "arbitrary"),
        name="sb_prefill",
    )(q, kv, kv)


def _moba_prefill_kernel(q_ref, k_ref, v_ref, o_ref, kmean_ref, kb_ref, vt_ref, *, n_blk):
    qi = pl.program_id(2)
    tq, tk = ATTN_TQ, ATTN_TK
    t_total = n_blk * MOBA_BLOCK
    blk_rows = -(-n_blk // 8) * 8

    @pl.when(qi == 0)
    def _():
        kmean_ref[...] = jnp.zeros_like(kmean_ref)
        for n in range(n_blk):
            blk = k_ref[0, n * MOBA_BLOCK:(n + 1) * MOBA_BLOCK, :]
            kmean_ref[n:n + 1, :] = jnp.sum(blk, axis=0, keepdims=True) * (1.0 / MOBA_BLOCK)
        _stage_keys_values(k_ref, slice(None), v_ref, slice(None), kb_ref, vt_ref, n_blk, tk)

    cols = HEADS_PER_SLAB * tq
    qpos = jnp.concatenate([qi * tq + _iota((1, tq), 1)] * HEADS_PER_SLAB, axis=1)
    blk_id = _iota((blk_rows, cols), 0)
    k_hi, k_lo = _split_bf16(kmean_ref[0:blk_rows, :])
    qf = jnp.concatenate(_split_heads_t(q_ref[0].T, 1.0), axis=1)
    qs = (qf * ATTN_SCALE).astype(BF16)
    q_hi, q_lo = _split_bf16(qf)
    gate = _dot(k_hi, q_hi) + _dot(k_lo, q_hi) + _dot(k_hi, q_lo)
    fully_past = (blk_id < qi) & (blk_id < n_blk)
    gate = jnp.where(fully_past, gate, -jnp.inf)
    sel = jnp.where(fully_past & (_top_rank(gate, n_blk, 0) < MOBA_TOPK), 1.0, 0.0)

    def scores(kb, limit):
        bias = jnp.where((kb * tk + _iota((tk, 1), 0)) <= limit, 0.0, NEG_BIG)
        return _dot(kb_ref[pl.ds(pl.multiple_of(kb * tk, tk), tk), :], qs) + bias

    def past_limit(kb, valid):
        picked = jnp.sum(jnp.where(blk_id == kb, sel, 0.0), axis=0, keepdims=True) > 0.5
        return jnp.where(picked, jnp.where(valid, t_total, -1), -1)

    def body(i, st):
        ka = qi - 2 * i
        kb = ka - 1
        kb_safe = jnp.maximum(kb, 0)
        s_a = scores(ka, jnp.where(ka == qi, qpos, past_limit(ka, True)))
        s_b = scores(kb_safe, past_limit(kb_safe, kb >= 0))
        return _osm_t_update2(st, s_a, s_b, lambda p: _weigh_heads_t(vt_ref, ka, p, tq),
                              lambda p: _weigh_heads_t(vt_ref, kb_safe, p, tq))

    out = _osm_t_final(lax.fori_loop(0, (qi + 2) // 2, body, _osm_t_init(HEAD_DIM, cols)))
    o_ref[0] = _unstack_heads_t(out, tq).T


def moba_prefill(q, kv):
    b, t, _ = q.shape
    n_slab = (N_HEADS - H_SB) // HEADS_PER_SLAB
    q_off = H_SB // HEADS_PER_SLAB
    v_off = N_HEADS // HEADS_PER_SLAB
    assert ATTN_TQ == MOBA_BLOCK and t % MOBA_BLOCK == 0 and t // MOBA_BLOCK <= LANES
    return pl.pallas_call(
        functools.partial(_moba_prefill_kernel, n_blk=t // MOBA_BLOCK),
        grid=(b, n_slab, t // ATTN_TQ),
        in_specs=[pl.BlockSpec((1, ATTN_TQ, LANES), lambda bi, s, qi: (bi, qi, q_off + s)),
                  pl.BlockSpec((1, t, LANES), lambda bi, s, qi: (bi, 0, q_off + s)),
                  pl.BlockSpec((1, t, LANES), lambda bi, s, qi: (bi, 0, v_off + q_off + s))],
        out_specs=pl.BlockSpec((1, ATTN_TQ, LANES), lambda bi, s, qi: (bi, qi, s)),
        out_shape=jax.ShapeDtypeStruct((b, t, (N_HEADS - H_SB) * HEAD_DIM), F32),
        scratch_shapes=[pltpu.VMEM((LANES, LANES), F32), pltpu.VMEM((t, LANES), BF16),
                        pltpu.VMEM((t // ATTN_TK, LANES, ATTN_TK), BF16)],
        compiler_params=_params("parallel", "parallel", "arbitrary"),
        name="moba_prefill",
    )(q, kv, kv)


CHUNKS_PER_PAGE = PAGE_SIZE // CMP_STRIDE
CMP_HALF = CMP_STRIDE * HEAD_DIM
PAGES_PER_STEP = 16


def _page_specs(block, index_of_page):
    return [pl.BlockSpec(block, functools.partial(index_of_page, g=g)) for g in range(PAGES_PER_STEP)]


def _compress_kernel(pt_ref, *refs, n_steps):
    page_refs = refs[:PAGES_PER_STEP]
    pe_ref, w1_ref, w2_ref, kc_ref, vc_ref, x_ref = refs[PAGES_PER_STEP:]
    p = pl.program_id(1)
    lane = _iota((1, LANES), 1)
    low = lane < HEAD_DIM
    r = _iota((PAGE_SIZE, PAGE_SIZE), 0)
    token = _iota((PAGE_SIZE, PAGE_SIZE), 1)
    perm = jnp.where(token == CMP_STRIDE * (r % CHUNKS_PER_PAGE) + r // CHUNKS_PER_PAGE, 1.0, 0.0).astype(BF16)
    for g, page_ref in enumerate(page_refs):
        rows = pl.ds(pl.multiple_of((p * PAGES_PER_STEP + g) * CHUNKS_PER_PAGE, CHUNKS_PER_PAGE), CHUNKS_PER_PAGE)
        for kv in range(2):
            for s in range(NSA_HKV // HEADS_PER_SLAB):
                two_heads = page_ref[0, kv, HEADS_PER_SLAB * s:HEADS_PER_SLAB * (s + 1)].reshape(LANES, PAGE_SIZE)
                slab = _dot_nt(perm, two_heads.astype(BF16))
                for pp in range(CMP_STRIDE // 2):
                    even = slab[2 * pp * CHUNKS_PER_PAGE:(2 * pp + 1) * CHUNKS_PER_PAGE]
                    odd = slab[(2 * pp + 1) * CHUNKS_PER_PAGE:(2 * pp + 2) * CHUNKS_PER_PAGE]
                    head0 = jnp.where(low, even, pltpu.roll(odd, HEAD_DIM, 1))
                    head1 = jnp.where(low, pltpu.roll(even, HEAD_DIM, 1), odd)
                    x_ref[kv * NSA_HKV + 2 * s, rows, pp * LANES:(pp + 1) * LANES] = head0
                    x_ref[kv * NSA_HKV + 2 * s + 1, rows, pp * LANES:(pp + 1) * LANES] = head1

    @pl.when(p == n_steps - 1)
    def _():
        n_chunk = n_steps * PAGES_PER_STEP * CHUNKS_PER_PAGE
        for kv, out_ref in ((0, kc_ref), (1, vc_ref)):
            pe_a = pe_ref[kv, 0:1, :]
            pe_b = pe_ref[kv, 1:2, :]
            w1a = w1_ref[kv, 0:CMP_HALF, :]
            w1b = w1_ref[kv, CMP_HALF:2 * CMP_HALF, :]
            w2 = w2_ref[kv]
            for h in range(NSA_HKV):
                x = x_ref[kv * NSA_HKV + h]
                first = _dot((x + pe_a).astype(BF16), w1a)
                second = _dot((x + pe_b).astype(BF16), w1b)
                hid = first + pltpu.roll(second, n_chunk - 1, 0)
                hid = hid / (1.0 + jnp.exp(-hid))
                out_ref[0, :, h * HEAD_DIM:(h + 1) * HEAD_DIM] = _dot(hid.astype(BF16), w2)


def compress_pages(pages, page_table, pe, w1_bf16, w2_bf16):
    if page_table is None:
        b, n_pages = pages.shape[0], pages.shape[-1] // PAGE_SIZE
        page_table = jnp.zeros((1, 1), jnp.int32)

        def index_of_page(bi, p, pt, g):
            return (bi, 0, 0, 0, p * PAGES_PER_STEP + g)
    else:
        b, n_pages = page_table.shape

        def index_of_page(bi, p, pt, g):
            return (pt[bi, p * PAGES_PER_STEP + g], 0, 0, 0, 0)
    assert n_pages % PAGES_PER_STEP == 0
    n_chunk = n_pages * CHUNKS_PER_PAGE
    width = NSA_HKV * HEAD_DIM
    grid_spec = pltpu.PrefetchScalarGridSpec(
        num_scalar_prefetch=1,
        grid=(b, n_pages // PAGES_PER_STEP),
        in_specs=_page_specs((1, 2, NSA_HKV, HEAD_DIM, PAGE_SIZE), index_of_page) + [
                  pl.BlockSpec(pe.shape, lambda bi, p, pt: (0, 0, 0)),
                  pl.BlockSpec(w1_bf16.shape, lambda bi, p, pt: (0, 0, 0)),
                  pl.BlockSpec(w2_bf16.shape, lambda bi, p, pt: (0, 0, 0))],
        out_specs=[pl.BlockSpec((1, n_chunk, width), lambda bi, p, pt: (bi, 0, 0)),
                   pl.BlockSpec((1, n_chunk, width), lambda bi, p, pt: (bi, 0, 0))],
        scratch_shapes=[pltpu.VMEM((2 * NSA_HKV, n_chunk, CMP_HALF), F32)],
    )
    return pl.pallas_call(
        functools.partial(_compress_kernel, n_steps=n_pages // PAGES_PER_STEP),
        grid_spec=grid_spec,
        out_shape=[jax.ShapeDtypeStruct((b, n_chunk, width), F32)] * 2,
        compiler_params=_params("parallel", "arbitrary"),
        name="nsa_compress",
    )(page_table, *([pages] * PAGES_PER_STEP), pe, w1_bf16, w2_bf16)


NSA_TQ = 256
SLC_TK = 256
WIN_TK = 128


def _cover_matrix(n_cmp_rows, n_cols, cmp_axis):
    i = _iota((n_cmp_rows, n_cols) if cmp_axis == 0 else (n_cols, n_cmp_rows), cmp_axis)
    j = _iota((n_cmp_rows, n_cols) if cmp_axis == 0 else (n_cols, n_cmp_rows), 1 - cmp_axis)
    ratio = SLC_BLOCK // CMP_STRIDE
    reach = CMP_BLOCK // CMP_STRIDE - 1
    return jnp.where((i >= ratio * j - reach) & (i <= ratio * j + ratio - 1), 1.0, 0.0).astype(BF16)


def _slc_score(imp, cur, n_slc, axis):
    j = _iota(imp.shape, axis)
    forced = (j == 0) | (j == cur) | (j == cur - 1)
    score = jnp.where(forced, FORCE_SCORE, jnp.where(j > cur, -FORCE_SCORE, imp))
    return jnp.where(j < n_slc, score, -jnp.inf)


def _nsa_prefill_kernel(q_ref, slc_ref, win_ref, kc_ref, vc_ref, g_ref, o_ref, ks_ref, vst_ref, kw_ref, vwt_ref, *, t):
    qi = pl.program_id(1)
    tq = NSA_TQ
    n_slc = t // SLC_BLOCK
    top = min(SLC_TOPK, n_slc)
    n_cmp_rows = t // CMP_STRIDE
    n_slab = NSA_HKV // HEADS_PER_SLAB

    @pl.when(qi == 0)
    def _():
        for s in range(n_slab):
            v_cols = slice(KV_WIDTH + s * LANES, KV_WIDTH + (s + 1) * LANES)
            k_cols = slice(s * LANES, (s + 1) * LANES)
            _stage_keys_values(slc_ref, k_cols, slc_ref, v_cols, ks_ref.at[s], vst_ref.at[s], t // SLC_TK, SLC_TK)
            _stage_keys_values(win_ref, k_cols, win_ref, v_cols, kw_ref.at[s], vwt_ref.at[s], t // WIN_TK, WIN_TK)

    qt = q_ref[0].T * ATTN_SCALE
    gates_t = g_ref[0].T
    qpos = qi * tq + _iota((1, tq), 1)
    qpos4 = jnp.concatenate([qpos] * NSA_GROUP, axis=1)
    cover_t = _cover_matrix(n_cmp_rows, LANES, 1)
    zeros_head = jnp.zeros((HEAD_DIM, tq), F32)
    outs = []
    for hk in range(NSA_HKV):
        half, slab = hk % HEADS_PER_SLAB, hk // HEADS_PER_SLAB
        cols = slice(slab * LANES, (slab + 1) * LANES)
        head_rows = slice(half * HEAD_DIM, (half + 1) * HEAD_DIM)
        parts = []
        for g in range(NSA_GROUP):
            h = hk * NSA_GROUP + g
            qg = qt[h * HEAD_DIM:(h + 1) * HEAD_DIM]
            parts.append(jnp.concatenate([qg, zeros_head] if half == 0 else [zeros_head, qg], axis=0))
        qs = jnp.concatenate(parts, axis=1).astype(BF16)

        s_c = _dot(kc_ref[0, :, cols].astype(BF16), qs)
        ok_c = (CMP_STRIDE * _iota((n_cmp_rows, 1), 0) + CMP_BLOCK - 1) <= qpos4
        sm = jnp.where(ok_c, s_c, NEG_BIG)
        pc = jnp.where(ok_c, jnp.exp(sm - jnp.max(sm, axis=0, keepdims=True)), 0.0)
        pc = pc / jnp.maximum(jnp.sum(pc, axis=0, keepdims=True), 1e-30)
        o_c = _dot(vc_ref[0, :, cols].T[head_rows].astype(BF16), pc.astype(BF16))

        psum = pc[:, 0:tq] + pc[:, tq:2 * tq] + pc[:, 2 * tq:3 * tq] + pc[:, 3 * tq:4 * tq]
        p_hi, p_lo = _split_bf16(psum)
        imp_t = (_dot(cover_t, p_hi) + _dot(cover_t, p_lo))[0:n_slc]
        score_t = _slc_score(imp_t, qpos // SLC_BLOCK, n_slc, 0)
        sel_t = jnp.where(_top_rank(score_t, n_slc, 0) < top, 1.0, 0.0)
        sel = jnp.concatenate([sel_t, jnp.zeros((LANES - n_slc, tq), F32)], axis=0).astype(BF16)

        n_slc_tiles = (qi * tq + tq + SLC_TK - 1) // SLC_TK

        def slc_scores(kt, valid, on_diagonal, qs=qs, sel=sel, slab=slab):
            blk_of_key = kt * (SLC_TK // SLC_BLOCK) + _iota((SLC_TK, LANES), 0) // SLC_BLOCK
            expand = jnp.where(_iota((SLC_TK, LANES), 1) == blk_of_key, jnp.where(valid, 1.0, 0.0), 0.0).astype(BF16)
            bias = _dot(expand, sel) * (-NEG_BIG) + NEG_BIG
            if on_diagonal:
                bias = jnp.where(kt * SLC_TK + _iota((SLC_TK, 1), 0) <= qpos, bias, NEG_BIG)
            s = _dot(ks_ref[slab, pl.ds(pl.multiple_of(kt * SLC_TK, SLC_TK), SLC_TK), :], qs)
            return s + jnp.concatenate([bias] * NSA_GROUP, axis=1)

        def slc_pair(st, ka, on_diagonal, slc_scores=slc_scores, slab=slab, head_rows=head_rows):
            kb = jnp.maximum(ka - 1, 0)
            return _osm_t_update2(st, slc_scores(ka, True, on_diagonal), slc_scores(kb, ka >= 1, False),
                                  lambda p: _dot(vst_ref[slab, ka, head_rows, :], p),
                                  lambda p: _dot(vst_ref[slab, kb, head_rows, :], p))

        st = slc_pair(_osm_t_init(HEAD_DIM, NSA_GROUP * tq), n_slc_tiles - 1, True)
        o_s = _osm_t_final(lax.fori_loop(1, (n_slc_tiles + 1) // 2,
                                         lambda i, st, slc_pair=slc_pair: slc_pair(st, n_slc_tiles - 1 - 2 * i, False), st))

        first_tile = jnp.maximum(qi * (tq // WIN_TK) - WINDOW // WIN_TK, 0)
        last_tile = qi * (tq // WIN_TK) + tq // WIN_TK - 1

        def win_scores(kt, valid, qs=qs, slab=slab):
            dist = lax.bitcast_convert_type(qpos - (kt * WIN_TK + _iota((WIN_TK, 1), 0)), jnp.uint32)
            bias = jnp.where(dist < WINDOW, jnp.where(valid, 0.0, NEG_BIG), NEG_BIG)
            s = _dot(kw_ref[slab, pl.ds(pl.multiple_of(kt * WIN_TK, WIN_TK), WIN_TK), :], qs)
            return s + jnp.concatenate([bias] * NSA_GROUP, axis=1)

        def win_body(i, st, win_scores=win_scores, slab=slab, head_rows=head_rows):
            ka = last_tile - 2 * i
            kb = jnp.maximum(ka - 1, first_tile)
            return _osm_t_update2(st, win_scores(ka, True), win_scores(kb, ka - 1 >= first_tile),
                                  lambda p: _dot(vwt_ref[slab, ka, head_rows, :], p),
                                  lambda p: _dot(vwt_ref[slab, kb, head_rows, :], p))

        o_w = _osm_t_final(lax.fori_loop(0, (last_tile - first_tile + 2) // 2, win_body,
                                         _osm_t_init(HEAD_DIM, NSA_GROUP * tq)))

        for g in range(NSA_GROUP):
            cs = slice(g * tq, (g + 1) * tq)
            gc = hk * LANES + g * 3
            outs.append(gates_t[gc:gc + 1] * o_c[:, cs] + gates_t[gc + 1:gc + 2] * o_s[:, cs]
                        + gates_t[gc + 2:gc + 3] * o_w[:, cs])
    o_ref[0] = jnp.concatenate(outs, axis=0).T


def nsa_prefill(q, kv, win, kc, vc, gates):
    b, t, _ = q.shape
    width = NSA_HKV * HEAD_DIM
    n_chunk = t // CMP_STRIDE
    n_slc = t // SLC_BLOCK
    assert t % SLC_TK == 0 and n_chunk % LANES == 0 and n_slc <= LANES and n_slc % 8 == 0
    assert NSA_TQ % WIN_TK == 0 and NSA_TQ % SLC_TK == 0
    n_slab = NSA_HKV // HEADS_PER_SLAB
    return pl.pallas_call(
        functools.partial(_nsa_prefill_kernel, t=t),
        grid=(b, t // NSA_TQ),
        in_specs=[pl.BlockSpec((1, NSA_TQ, N_HEADS * HEAD_DIM), lambda bi, qi: (bi, qi, 0)),
                  pl.BlockSpec((1, t, 2 * width), lambda bi, qi: (bi, 0, 1)),
                  pl.BlockSpec((1, t, 2 * width), lambda bi, qi: (bi, 0, 0)),
                  pl.BlockSpec((1, n_chunk, width), lambda bi, qi: (bi, 0, 0)),
                  pl.BlockSpec((1, n_chunk, width), lambda bi, qi: (bi, 0, 0)),
                  pl.BlockSpec((1, NSA_TQ, NSA_HKV * LANES), lambda bi, qi: (bi, qi, 0))],
        out_specs=pl.BlockSpec((1, NSA_TQ, N_HEADS * HEAD_DIM), lambda bi, qi: (bi, qi, 0)),
        out_shape=jax.ShapeDtypeStruct((b, t, N_HEADS * HEAD_DIM), F32),
        scratch_shapes=[pltpu.VMEM((n_slab, t, LANES), BF16), pltpu.VMEM((n_slab, t // SLC_TK, LANES, SLC_TK), BF16),
                        pltpu.VMEM((n_slab, t, LANES), BF16), pltpu.VMEM((n_slab, t // WIN_TK, LANES, WIN_TK), BF16)],
        compiler_params=_params("parallel", "arbitrary"),
        name="nsa_prefill",
    )(q, kv, win, kc, vc, gates)


SB_WIDTH = H_SB * HEAD_DIM
KV0_V_OFF = N_HEADS * HEAD_DIM
PAGES_PER_MOBA_BLOCK = MOBA_BLOCK // PAGE_SIZE


def _load_new_rows(dst_ref, new_ref, col, width, dec_seq):
    dst_ref[...] = jnp.zeros_like(dst_ref)
    dst_ref[0:dec_seq, :] = new_ref[0, :, col:col + width]


def _heads_by_dim(page_tile):
    return page_tile.reshape(page_tile.shape[0] * HEAD_DIM, page_tile.shape[2])


def _fold_heads(res, dec_seq, n_heads):
    lane_head = _iota((1, n_heads * HEAD_DIM), 1) // HEAD_DIM
    out = jnp.zeros((dec_seq, n_heads * HEAD_DIM), F32)
    for h in range(n_heads):
        out = out + jnp.where(lane_head == h, res[h * dec_seq:(h + 1) * dec_seq, :], 0.0)
    return out


def _sb_decode_kernel(pt_ref, qbd_ref, new_ref, *refs, n_steps, dec_seq):
    k_refs = refs[:PAGES_PER_STEP]
    v_refs = refs[PAGES_PER_STEP:2 * PAGES_PER_STEP]
    o_ref, kmean_ref, c_ref, acc_ref, kpad_ref, vpad_ref = refs[2 * PAGES_PER_STEP:]
    p = pl.program_id(1)
    qbd = (qbd_ref[0] * ATTN_SCALE).astype(BF16)
    rows = qbd.shape[0]
    later = _later_matrix(PAGE_SIZE)
    n_blk = n_steps * PAGES_PER_STEP // PAGES_PER_MOBA_BLOCK

    @pl.when(p == 0)
    def _():
        kmean_ref[...] = jnp.zeros_like(kmean_ref)
        _load_new_rows(kpad_ref, new_ref, 0, SB_WIDTH, dec_seq)
        _load_new_rows(vpad_ref, new_ref, KV0_V_OFF, SB_WIDTH, dec_seq)
        valid = _iota((1, PAGE_SIZE), 1) < lax.rem(_iota((rows, 1), 0), dec_seq)
        ls, lk = _log_sigmoid_pair(_dot_nt(qbd, kpad_ref[...].astype(BF16)))
        lk = jnp.where(valid, lk, 0.0)
        w = jnp.where(valid, jnp.exp(ls + _dot_exact_rhs(lk, later)), 0.0)
        acc_ref[...] = _dot(w.astype(BF16), vpad_ref[...].astype(BF16))
        c_ref[...] = jnp.sum(lk, axis=1, keepdims=True)

    zs = [_dot(qbd, _heads_by_dim(k_ref[0, 0, 0:H_SB]).astype(BF16)) for k_ref in k_refs]
    c = c_ref[...]
    acc = acc_ref[...]
    for z, v_ref in zip(zs, v_refs):
        ls, lk = _log_sigmoid_pair(z)
        w = jnp.exp(ls + _dot_exact_rhs(lk, later) + c)
        acc = acc + _dot_nt(w.astype(BF16), _heads_by_dim(v_ref[0, 0]).astype(BF16))
        c = c + jnp.sum(lk, axis=1, keepdims=True)
    c_ref[...] = c
    acc_ref[...] = acc

    lane = _iota((1, LANES), 1)
    kmean = kmean_ref[0]
    for j in range(PAGES_PER_STEP // PAGES_PER_MOBA_BLOCK):
        pages = jnp.zeros(((N_HEADS - H_SB) * HEAD_DIM, PAGE_SIZE), F32)
        for k_ref in k_refs[j * PAGES_PER_MOBA_BLOCK:(j + 1) * PAGES_PER_MOBA_BLOCK]:
            pages = pages + _heads_by_dim(k_ref[0, 0, H_SB:N_HEADS])
        total = jnp.sum(pages, axis=1, keepdims=True)
        blk = n_blk - 1 - (p * (PAGES_PER_STEP // PAGES_PER_MOBA_BLOCK) + j)
        kmean = jnp.where(lane == blk, total * (1.0 / MOBA_BLOCK), kmean)
    kmean_ref[0] = kmean

    @pl.when(p == n_steps - 1)
    def _():
        o_ref[0] = _fold_heads(acc_ref[...], dec_seq, H_SB)


def sb_decode(qbd, new_kv, pool, page_table):
    b, n_pages = page_table.shape
    dec_seq = new_kv.shape[1]
    rows = qbd.shape[1]
    assert n_pages % PAGES_PER_STEP == 0 and PAGES_PER_STEP % PAGES_PER_MOBA_BLOCK == 0
    assert n_pages // PAGES_PER_MOBA_BLOCK <= LANES

    def page_of(bi, p, pt, g):
        return pt[bi, n_pages - 1 - (p * PAGES_PER_STEP + g)]

    grid_spec = pltpu.PrefetchScalarGridSpec(
        num_scalar_prefetch=1,
        grid=(b, n_pages // PAGES_PER_STEP),
        in_specs=[pl.BlockSpec((1, rows, SB_WIDTH), lambda bi, p, pt: (bi, 0, 0)),
                  pl.BlockSpec((1, dec_seq, new_kv.shape[2]), lambda bi, p, pt: (bi, 0, 0))]
        + _page_specs((1, 1, N_HEADS, HEAD_DIM, PAGE_SIZE), lambda bi, p, pt, g: (page_of(bi, p, pt, g), 0, 0, 0, 0))
        + _page_specs((1, 1, H_SB, HEAD_DIM, PAGE_SIZE), lambda bi, p, pt, g: (page_of(bi, p, pt, g), 1, 0, 0, 0)),
        out_specs=[pl.BlockSpec((1, dec_seq, SB_WIDTH), lambda bi, p, pt: (bi, 0, 0)),
                   pl.BlockSpec((1, SB_WIDTH, LANES), lambda bi, p, pt: (bi, 0, 0))],
        scratch_shapes=[pltpu.VMEM((rows, 1), F32), pltpu.VMEM((rows, SB_WIDTH), F32),
                        pltpu.VMEM((PAGE_SIZE, SB_WIDTH), F32), pltpu.VMEM((PAGE_SIZE, SB_WIDTH), F32)],
    )
    return pl.pallas_call(
        functools.partial(_sb_decode_kernel, n_steps=n_pages // PAGES_PER_STEP, dec_seq=dec_seq),
        grid_spec=grid_spec,
        out_shape=[jax.ShapeDtypeStruct((b, dec_seq, SB_WIDTH), F32),
                   jax.ShapeDtypeStruct((b, SB_WIDTH, LANES), F32)],
        compiler_params=_params("parallel", "arbitrary"),
        name="sb_decode",
    )(page_table, qbd, new_kv, *([pool] * (2 * PAGES_PER_STEP)))


def _moba_decode_kernel(pt_ref, qbd_ref, kmean_ref, new_ref, *refs, n_steps, dec_seq, past_len):
    kv_refs = refs[:PAGES_PER_STEP]
    o_ref, sel_ref, m_ref, l_ref, acc_ref, kpad_ref, vpad_ref = refs[PAGES_PER_STEP:]
    p = pl.program_id(1)
    n_blk = n_steps * PAGES_PER_STEP // PAGES_PER_MOBA_BLOCK
    qf = qbd_ref[0]
    qb = (qf * ATTN_SCALE).astype(BF16)
    rows = qf.shape[0]
    t_of_row = lax.rem(_iota((rows, 1), 0), dec_seq)
    blk_id = _iota((1, LANES), 1)

    @pl.when(p == 0)
    def _():
        q_hi, q_lo = _split_bf16(qf)
        k_hi, k_lo = _split_bf16(kmean_ref[0])
        gate = _dot(q_hi, k_hi) + _dot(q_hi, k_lo) + _dot(q_lo, k_hi)
        fully_past = (blk_id < (past_len + t_of_row) // MOBA_BLOCK) & (blk_id < n_blk)
        gate = jnp.where(fully_past, gate, -jnp.inf)
        sel_ref[...] = jnp.where(fully_past & (_top_rank(gate, n_blk, 1) < MOBA_TOPK), 1.0, 0.0)
        _osm_reset(m_ref, l_ref, acc_ref)

    sel = sel_ref[...]
    s_parts, mask_parts = [], []
    for g, kv_ref in enumerate(kv_refs):
        s_parts.append(_dot(qb, _heads_by_dim(kv_ref[0, 0]).astype(BF16)))
        blk = (p * PAGES_PER_STEP + g) // PAGES_PER_MOBA_BLOCK
        picked = jnp.sum(jnp.where(blk_id == blk, sel, 0.0), axis=1, keepdims=True) > 0.5
        mask_parts.append(jnp.broadcast_to(picked, (rows, PAGE_SIZE)))

    def weigh_pages(pb):
        out = jnp.zeros((rows, SB_WIDTH), F32)
        for g, kv_ref in enumerate(kv_refs):
            out = out + _dot_nt(pb[:, g * PAGE_SIZE:(g + 1) * PAGE_SIZE], _heads_by_dim(kv_ref[0, 1]).astype(BF16))
        return out

    st = _osm_update((m_ref[...], l_ref[...], acc_ref[...]), jnp.concatenate(s_parts, axis=1),
                     jnp.concatenate(mask_parts, axis=1), weigh_pages)
    m_ref[...], l_ref[...], acc_ref[...] = st

    @pl.when(p == n_steps - 1)
    def _():
        _load_new_rows(kpad_ref, new_ref, SB_WIDTH, SB_WIDTH, dec_seq)
        _load_new_rows(vpad_ref, new_ref, KV0_V_OFF + SB_WIDTH, SB_WIDTH, dec_seq)
        key = _iota((1, PAGE_SIZE), 1)
        vb = vpad_ref[...].astype(BF16)
        fin = _osm_update((m_ref[...], l_ref[...], acc_ref[...]), _dot_nt(qb, kpad_ref[...].astype(BF16)),
                          (key <= t_of_row) & (key < dec_seq), lambda pb: _dot(pb, vb))
        o_ref[0] = _fold_heads(_osm_final(fin), dec_seq, N_HEADS - H_SB)


def moba_decode(qbd, kmean, new_kv, pool, page_table, past_len):
    b, n_pages = page_table.shape
    dec_seq = new_kv.shape[1]
    rows = qbd.shape[1]
    assert past_len % MOBA_BLOCK == 0 and n_pages * PAGE_SIZE == past_len and dec_seq <= PAGE_SIZE
    assert n_pages % PAGES_PER_STEP == 0
    grid_spec = pltpu.PrefetchScalarGridSpec(
        num_scalar_prefetch=1,
        grid=(b, n_pages // PAGES_PER_STEP),
        in_specs=[pl.BlockSpec((1, rows, SB_WIDTH), lambda bi, p, pt: (bi, 0, 0)),
                  pl.BlockSpec((1, SB_WIDTH, LANES), lambda bi, p, pt: (bi, 0, 0)),
                  pl.BlockSpec((1, dec_seq, new_kv.shape[2]), lambda bi, p, pt: (bi, 0, 0))]
        + _page_specs((1, 2, N_HEADS - H_SB, HEAD_DIM, PAGE_SIZE),
                      lambda bi, p, pt, g: (pt[bi, p * PAGES_PER_STEP + g], 0, 1, 0, 0)),
        out_specs=pl.BlockSpec((1, dec_seq, SB_WIDTH), lambda bi, p, pt: (bi, 0, 0)),
        scratch_shapes=[pltpu.VMEM((rows, LANES), F32), pltpu.VMEM((rows, 1), F32), pltpu.VMEM((rows, 1), F32),
                        pltpu.VMEM((rows, SB_WIDTH), F32),
                        pltpu.VMEM((PAGE_SIZE, SB_WIDTH), F32), pltpu.VMEM((PAGE_SIZE, SB_WIDTH), F32)],
    )
    return pl.pallas_call(
        functools.partial(_moba_decode_kernel, n_steps=n_pages // PAGES_PER_STEP, dec_seq=dec_seq, past_len=past_len),
        grid_spec=grid_spec,
        out_shape=jax.ShapeDtypeStruct((b, dec_seq, SB_WIDTH), F32),
        compiler_params=_params("parallel", "arbitrary"),
        name="moba_decode",
    )(page_table, qbd, kmean, new_kv, *([pool] * PAGES_PER_STEP))


KV_WIDTH = NSA_HKV * HEAD_DIM


def _nsa_decode_kernel(pt_ref, qbd_ref, kc_ref, vc_ref, state_ref, new_kv_ref, new_win_ref, g_ref, *refs,
                       n_steps, dec_seq, past_len):
    page_refs = refs[:PAGES_PER_STEP]
    o_ref, sel_ref, oc_ref, ow_ref, m_ref, l_ref, acc_ref, kpad_ref, vpad_ref = refs[PAGES_PER_STEP:]
    p = pl.program_id(1)
    qb = (qbd_ref[0] * ATTN_SCALE).astype(BF16)
    rows = qb.shape[0]
    grp_rows = NSA_HKV * dec_seq
    n_cmp_rows = kc_ref.shape[1]
    n_slc = past_len // SLC_BLOCK + 1
    slc_lanes = sel_ref.shape[1]
    t_of_row = lax.rem(_iota((rows, 1), 0), dec_seq)
    qpos = past_len + t_of_row
    key = _iota((1, PAGE_SIZE), 1)

    @pl.when(p == 0)
    def _():
        s_c = _dot_nt(qb, kc_ref[0].astype(BF16))
        ok_c = (CMP_STRIDE * _iota((1, n_cmp_rows), 1) + CMP_BLOCK - 1) <= qpos
        sm = jnp.where(ok_c, s_c, NEG_BIG)
        pc = jnp.where(ok_c, jnp.exp(sm - jnp.max(sm, axis=1, keepdims=True)), 0.0)
        pc = pc / jnp.maximum(jnp.sum(pc, axis=1, keepdims=True), 1e-30)
        oc_ref[...] = _dot(pc.astype(BF16), vc_ref[0].astype(BF16))
        psum = pc[0:grp_rows]
        for g in range(1, NSA_GROUP):
            psum = psum + pc[g * grp_rows:(g + 1) * grp_rows]
        imp = _dot_exact_rhs(psum, _cover_matrix(n_cmp_rows, slc_lanes, 0))
        score = _slc_score(imp, qpos[0:grp_rows] // SLC_BLOCK, n_slc, 1)
        sel = jnp.where(_top_rank(score, n_slc, 1) < min(SLC_TOPK, n_slc), 1.0, 0.0)
        sel_ref[...] = jnp.concatenate([sel] * NSA_GROUP, axis=0)
        vwt = _heads_by_dim(state_ref[0, 1]).astype(BF16)
        st = _osm_update(_osm_init(rows, KV_WIDTH), _dot(qb, _heads_by_dim(state_ref[0, 0]).astype(BF16)),
                         _iota((1, WINDOW), 1) > t_of_row, lambda pb: _dot_nt(pb, vwt))
        _load_new_rows(kpad_ref, new_win_ref, 0, KV_WIDTH, dec_seq)
        _load_new_rows(vpad_ref, new_win_ref, KV_WIDTH, KV_WIDTH, dec_seq)
        vb = vpad_ref[...].astype(BF16)
        st = _osm_update(st, _dot_nt(qb, kpad_ref[...].astype(BF16)), (key <= t_of_row) & (key < dec_seq),
                         lambda pb: _dot(pb, vb))
        ow_ref[...] = _osm_final(st)
        _osm_reset(m_ref, l_ref, acc_ref)

    keys_per_step = PAGES_PER_STEP * PAGE_SIZE
    blk_of_key = p * (keys_per_step // SLC_BLOCK) + _iota((slc_lanes, keys_per_step), 1) // SLC_BLOCK
    expand = jnp.where(_iota((slc_lanes, keys_per_step), 0) == blk_of_key, 1.0, 0.0).astype(BF16)
    s_all = jnp.concatenate([_dot(qb, _heads_by_dim(ref[0, 0]).astype(BF16)) for ref in page_refs], axis=1)

    def weigh_pages(pb):
        out = jnp.zeros((rows, KV_WIDTH), F32)
        for g, ref in enumerate(page_refs):
            out = out + _dot_nt(pb[:, g * PAGE_SIZE:(g + 1) * PAGE_SIZE], _heads_by_dim(ref[0, 1]).astype(BF16))
        return out

    st = _osm_update((m_ref[...], l_ref[...], acc_ref[...]), s_all,
                     _dot(sel_ref[...].astype(BF16), expand) > 0.5, weigh_pages)
    m_ref[...], l_ref[...], acc_ref[...] = st

    @pl.when(p == n_steps - 1)
    def _():
        _load_new_rows(kpad_ref, new_kv_ref, 2 * KV_WIDTH, KV_WIDTH, dec_seq)
        _load_new_rows(vpad_ref, new_kv_ref, 3 * KV_WIDTH, KV_WIDTH, dec_seq)
        cur = past_len // SLC_BLOCK
        picked = sel_ref[:, cur:cur + 1] > 0.5
        vb = vpad_ref[...].astype(BF16)
        fin = _osm_update((m_ref[...], l_ref[...], acc_ref[...]), _dot_nt(qb, kpad_ref[...].astype(BF16)),
                          picked & (key <= t_of_row) & (key < dec_seq), lambda pb: _dot(pb, vb))
        o_s = _osm_final(fin)
        o_c = oc_ref[...]
        o_w = ow_ref[...]
        gates = g_ref[0]
        for hk in range(NSA_HKV):
            for g in range(NSA_GROUP):
                rs = slice(g * grp_rows + hk * dec_seq, g * grp_rows + (hk + 1) * dec_seq)
                hs = slice(hk * HEAD_DIM, (hk + 1) * HEAD_DIM)
                gc = hk * LANES + g * 3
                h = hk * NSA_GROUP + g
                o_ref[0, :, h * HEAD_DIM:(h + 1) * HEAD_DIM] = (
                    gates[:, gc:gc + 1] * o_c[rs, hs] + gates[:, gc + 1:gc + 2] * o_s[rs, hs]
                    + gates[:, gc + 2:gc + 3] * o_w[rs, hs])


def nsa_decode(qbd, kc, vc, state_win, new_kv, new_win, gates, pool, page_table, past_len):
    b, n_pages = page_table.shape
    dec_seq = new_kv.shape[1]
    rows = qbd.shape[1]
    n_chunk = kc.shape[1]
    n_slc = past_len // SLC_BLOCK + 1
    slc_lanes = -(-n_slc // LANES) * LANES
    assert past_len % SLC_BLOCK == 0 and past_len >= WINDOW and state_win.shape[-1] == WINDOW and dec_seq <= SLC_BLOCK
    assert n_pages % PAGES_PER_STEP == 0
    grid_spec = pltpu.PrefetchScalarGridSpec(
        num_scalar_prefetch=1,
        grid=(b, n_pages // PAGES_PER_STEP),
        in_specs=[pl.BlockSpec((1, rows, KV_WIDTH), lambda bi, p, pt: (bi, 0, 0)),
                  pl.BlockSpec((1, n_chunk, KV_WIDTH), lambda bi, p, pt: (bi, 0, 0)),
                  pl.BlockSpec((1, n_chunk, KV_WIDTH), lambda bi, p, pt: (bi, 0, 0)),
                  pl.BlockSpec((1, 2, NSA_HKV, HEAD_DIM, WINDOW), lambda bi, p, pt: (bi, 0, 0, 0, 0)),
                  pl.BlockSpec((1, dec_seq, 4 * KV_WIDTH), lambda bi, p, pt: (bi, 0, 0)),
                  pl.BlockSpec((1, dec_seq, 2 * KV_WIDTH), lambda bi, p, pt: (bi, 0, 0)),
                  pl.BlockSpec((1, dec_seq, NSA_HKV * LANES), lambda bi, p, pt: (bi, 0, 0))]
        + _page_specs((1, 2, NSA_HKV, HEAD_DIM, PAGE_SIZE),
                      lambda bi, p, pt, g: (pt[bi, p * PAGES_PER_STEP + g], 1, 0, 0, 0)),
        out_specs=pl.BlockSpec((1, dec_seq, N_HEADS * HEAD_DIM), lambda bi, p, pt: (bi, 0, 0)),
        scratch_shapes=[pltpu.VMEM((rows, slc_lanes), F32), pltpu.VMEM((rows, KV_WIDTH), F32),
                        pltpu.VMEM((rows, KV_WIDTH), F32), pltpu.VMEM((rows, 1), F32), pltpu.VMEM((rows, 1), F32),
                        pltpu.VMEM((rows, KV_WIDTH), F32),
                        pltpu.VMEM((PAGE_SIZE, KV_WIDTH), F32), pltpu.VMEM((PAGE_SIZE, KV_WIDTH), F32)],
    )
    return pl.pallas_call(
        functools.partial(_nsa_decode_kernel, n_steps=n_pages // PAGES_PER_STEP, dec_seq=dec_seq, past_len=past_len),
        grid_spec=grid_spec,
        out_shape=jax.ShapeDtypeStruct((b, dec_seq, N_HEADS * HEAD_DIM), F32),
        compiler_params=_params("parallel", "arbitrary"),
        name="nsa_decode",
    )(page_table, qbd, kc, vc, state_win, new_kv, new_win, gates, *([pool] * PAGES_PER_STEP))


EVEN_PLAN = (
    (0, 512, "copy", 0, 0, 0),
    (512, 512, "rope", 0, 0, 512),
    (1024, 512, "copy", 0, 1, 0),
    (1536, 512, "rope", 1, 1, 512),
    (2048, 1024, "copy", 0, 1, 1024),
)
ODD_PLAN = (
    (0, 1024, "rope", 0, 0, 0),
    (1024, 256, "rope", 1, 1, 0),
    (1280, 256, "copy", 0, 1, 256),
    (1536, 256, "rope", 2, 1, 512),
    (1792, 256, "copy", 0, 1, 768),
    (2048, 256, "rope", 3, 2, 0),
    (2304, 256, "copy", 0, 2, 256),
    (2560, 512, "sigmoid", 0, 3, 0),
)
ODD_QKV = N_HEADS * HEAD_DIM + 6 * NSA_HKV * HEAD_DIM


def _gate_columns():
    idx = np.full((NSA_HKV * LANES,), ODD_QKV + 3 * N_HEADS, np.int32)
    for h in range(N_HEADS):
        for r in range(3):
            idx[(h // NSA_GROUP) * LANES + (h % NSA_GROUP) * 3 + r] = ODD_QKV + h * 3 + r
    return idx


def _rope_tables(pos):
    half = HEAD_DIM // 2
    inv = jnp.power(ROPE_THETA, -jnp.arange(half, dtype=F32) / half)
    ang = pos.astype(F32)[:, None] * inv[None, :]
    cos, sin = jnp.cos(ang), jnp.sin(ang)
    return (jnp.concatenate([cos] * (LANES // half), axis=1),
            jnp.concatenate([-sin, sin] * (LANES // HEAD_DIM), axis=1))


def _gain_rows(*gains):
    return jnp.stack([jnp.tile(g.astype(F32), LANES // HEAD_DIM) for g in gains])


def _block_diag_queries(q, heads_per_group, n_groups, group_major):
    b, dec, n_heads, _ = q.shape
    h = np.arange(n_heads)
    grp = h // heads_per_group
    onehot = jnp.asarray(np.eye(n_groups, dtype=np.float32)[grp])
    bd = q.transpose(0, 2, 1, 3)[:, :, :, None, :] * onehot[None, :, None, :, None]
    if group_major:
        bd = bd.reshape(b, n_groups, heads_per_group, dec, n_groups, HEAD_DIM).transpose(0, 2, 1, 3, 4, 5)
    return bd.reshape(b, n_heads * dec, n_groups * HEAD_DIM)


def _prepare_weights(w_in0, w_out0, w_in1, w_out1, cmp_w1, cmp_w2, w_gu, w_down):
    w_in1_ext = jnp.concatenate([w_in1[0], jnp.zeros((D_MODEL, 1), w_in1.dtype)], axis=1)
    cols = np.concatenate([np.arange(ODD_QKV, dtype=np.int32), _gate_columns()])
    return dict(
        w_in0=w_in0[0].astype(BF16),
        w_out0_sb=w_out0[0, :SB_WIDTH].astype(BF16),
        w_out0_mb=w_out0[0, SB_WIDTH:].astype(BF16),
        w_in1=w_in1_ext[:, cols].astype(BF16),
        w_out1=w_out1[0].astype(BF16),
        cmp_w1=cmp_w1[0].astype(BF16),
        cmp_w2=cmp_w2[0].astype(BF16),
        w_gu=w_gu.astype(BF16),
        w_down=w_down.astype(BF16),
    )


def _cmp_pe_rows(cmp_pe):
    return cmp_pe[0].reshape(2, CMP_BLOCK // CMP_STRIDE, CMP_HALF)


def _tokens_last(x):
    return x.transpose(0, 2, 3, 4, 1)


def _tokens_first(x):
    return x.transpose(0, 4, 1, 2, 3)


def _trunk(x, q0, caches, w, g_mix0, g_q0, g_k0, g_mix1, g_q1, g_k1, cmp_pe, g_ffn):
    b, t, _ = x.shape
    m = b * t
    tm = min(512, m)
    xf = x.reshape(m, D_MODEL)
    pos = q0 + jnp.arange(t)
    cos, sin = _rope_tables(pos)
    if m // tm * tm != m or t % tm != 0:
        cos, sin = jnp.tile(cos, (m // t, 1)), jnp.tile(sin, (m // t, 1))

    prompt = caches is None
    q_l0, kv_l0, *kv0_t = norm_proj_post(xf, g_mix0[0], w["w_in0"], cos, sin, _gain_rows(g_q0[0], g_k0[0]), EVEN_PLAN,
                                         (1024, 2048), tm, ((1, (2, N_HEADS)),) if prompt else (), t)
    kv0_out = _tokens_first(kv0_t[0]) if prompt else kv_l0.reshape(b, t, 2, N_HEADS, HEAD_DIM)
    if caches is None:
        o_sb = sb_prefill(q_l0.reshape(b, t, 1024), kv_l0.reshape(b, t, 2048)).reshape(m, SB_WIDTH)
        o_mb = moba_prefill(q_l0.reshape(b, t, 1024), kv_l0.reshape(b, t, 2048)).reshape(m, SB_WIDTH)
    else:
        pool0, pool1, state_win, page_table = caches
        q4 = q_l0.reshape(b, t, N_HEADS, HEAD_DIM)
        new_kv0 = kv_l0.reshape(b, t, 2048)
        o_sb, kmean = sb_decode(_block_diag_queries(q4[:, :, :H_SB], 1, H_SB, False), new_kv0, pool0, page_table)
        o_mb = moba_decode(_block_diag_queries(q4[:, :, H_SB:], 1, N_HEADS - H_SB, False), kmean, new_kv0, pool0,
                           page_table, q0)
        o_sb, o_mb = o_sb.reshape(m, SB_WIDTH), o_mb.reshape(m, SB_WIDTH)
    h2 = mixer_out_ffn([o_sb, o_mb], [w["w_out0_sb"], w["w_out0_mb"]], xf, g_ffn[0], w["w_gu"][0], w["w_down"][0],
                       tm, D_FF // 2)

    gains1 = _gain_rows(g_q1[0], g_k1[0, 0], g_k1[0, 1], g_k1[0, 2])
    q_l1, kv_l1, win_l1, gates, *kv1_t = norm_proj_post(h2, g_mix1[0], w["w_in1"], cos, sin, gains1, ODD_PLAN,
                                                        (1024, 1024, 512, 512), tm,
                                                        ((1, (4, NSA_HKV)),) if prompt else (), t)
    kv1_out = _tokens_first(kv1_t[0]) if prompt else kv_l1.reshape(b, t, 4, NSA_HKV, HEAD_DIM)
    pe_rows = _cmp_pe_rows(cmp_pe)
    new_win = win_l1.reshape(b, t, 2, NSA_HKV, HEAD_DIM)
    if caches is None:
        kc, vc = compress_pages(kv1_t[0], None, pe_rows, w["cmp_w1"], w["cmp_w2"])
        o_nsa = nsa_prefill(q_l1.reshape(b, t, 1024), kv_l1.reshape(b, t, 1024), win_l1.reshape(b, t, 512),
                            kc, vc, gates.reshape(b, t, 512)).reshape(m, 1024)
        win_state = new_win[:, t - min(WINDOW, t):]
    else:
        kc, vc = compress_pages(pool1, page_table, pe_rows, w["cmp_w1"], w["cmp_w2"])
        qbd1 = _block_diag_queries(q_l1.reshape(b, t, N_HEADS, HEAD_DIM), NSA_GROUP, NSA_HKV, True)
        o_nsa = nsa_decode(qbd1, kc, vc, _tokens_last(state_win), kv_l1.reshape(b, t, 1024), win_l1.reshape(b, t, 512),
                           gates.reshape(b, t, 512), pool1, page_table, q0).reshape(m, 1024)
        win_state = jnp.concatenate([state_win[:, t:], new_win], axis=1)
    y = mixer_out_ffn([o_nsa], [w["w_out1"]], h2, g_ffn[1], w["w_gu"][1], w["w_down"][1], tm, D_FF // 2)

    return (y.reshape(b, t, D_MODEL), kv0_out[None], kv1_out[None], win_state[None])


def kernel(x_prompt, x_sample, cache_kv0, cache_kv1, state_win, page_table, g_mix0, w_in0, g_q0, g_k0, w_out0,
           g_mix1, w_in1, g_q1, g_k1, cmp_pe, cmp_w1, cmp_w2, w_out1, g_ffn, w_gu, w_down):
    assert w_in0.shape[0] == 1 and w_in1.shape[0] == 1, "one even and one odd layer"
    w = _prepare_weights(w_in0, w_out0, w_in1, w_out1, cmp_w1, cmp_w2, w_gu, w_down)
    norms = (g_mix0, g_q0, g_k0, g_mix1, g_q1, g_k1, cmp_pe, g_ffn)
    y_p, kv0_p, kv1_p, win_p = _trunk(x_prompt, 0, None, w, *norms)
    past_len = page_table.shape[1] * cache_kv0.shape[2]
    caches = (_tokens_last(cache_kv0[0]), _tokens_last(cache_kv1[0]), state_win[0], page_table)
    y_s, kv0_s, kv1_s, win_s = _trunk(x_sample, past_len, caches, w, *norms)
    return (y_p, y_s, kv0_p, kv0_s, kv1_p, kv1_s, win_p, win_s)
```
